```python
import jax, jax.numpy as jnp
from jax import lax
import numpy as np

D_MODEL = 1024
BATCH = 32
SEQ = 2048
DEPTH = 2

PE_DIM = 256
HEAD_DIM = 64
ROPE_THETA = 10000.0
ATT_WIDTH = 3 * D_MODEL // 8
ATT_HEADS = ATT_WIDTH // HEAD_DIM
Q_RANK = D_MODEL // 4
IDX_HEADS = 8
IDX_DIM = 64
TOPK_MAX = 256
Q_BLOCK = 128
RWKV_WIDTH = 3 * D_MODEL // 8
RWKV_HEADS = RWKV_WIDTH // HEAD_DIM
DECAY_RANK = 64
ICLR_RANK = 64
GATE_RANK = 128
GN_EPS = 64e-5
POOL_WINDOWS = (2, 4, 8, 16)
POOL_WIDTH = D_MODEL - ATT_WIDTH - RWKV_WIDTH
POOL_GROUP = POOL_WIDTH // len(POOL_WINDOWS)
A_SIZES = (Q_RANK, HEAD_DIM, HEAD_DIM, IDX_DIM, IDX_HEADS)
B_SIZES = (RWKV_WIDTH, RWKV_WIDTH, RWKV_WIDTH, DECAY_RANK, ICLR_RANK, GATE_RANK)
A_COLS = sum(A_SIZES)
B_COLS = sum(B_SIZES)
N_IN = A_COLS + B_COLS + POOL_WIDTH
MIX_WIDTH = ATT_WIDTH + RWKV_WIDTH + POOL_WIDTH
D_FF = 4 * D_MODEL
ALPHA = (2 * DEPTH) ** 0.25
BETA = (8 * DEPTH) ** -0.25
LN_EPS = 1e-5
RMS_EPS = 1e-6

kernel_name = 'hymba_dsa_rwkv7_pool_deepnorm'


def _split(z, sizes):
    offs = np.cumsum(sizes)[:-1].tolist()
    return jnp.split(z, offs, axis=-1)


def layer_norm(x, g, b):
    xf = x.astype(jnp.float32)
    mu = xf.mean(-1, keepdims=True)
    var = jnp.square(xf - mu).mean(-1, keepdims=True)
    return ((xf - mu) * lax.rsqrt(var + LN_EPS) * g + b).astype(x.dtype)


def rms_norm(x, g):
    xf = x.astype(jnp.float32)
    return (xf * lax.rsqrt(jnp.square(xf).mean(-1, keepdims=True) + RMS_EPS) * g).astype(x.dtype)


def rotary_tables(positions):
    inv_freq = ROPE_THETA ** (-jnp.arange(0, HEAD_DIM, 2, dtype=jnp.float32) / HEAD_DIM)
    ang = positions.astype(jnp.float32)[..., None] * inv_freq
    return jnp.cos(ang)[:, :, None, :], jnp.sin(ang)[:, :, None, :]


def rotary(x, cos, sin):
    x1, x2 = jnp.split(x.astype(jnp.float32), 2, axis=-1)
    return jnp.concatenate([x1 * cos - x2 * sin, x2 * cos + x1 * sin], axis=-1).astype(x.dtype)


def token_shift(z):
    return jnp.pad(z, ((0, 0), (1, 0), (0, 0)))[:, :-1]


def dsa_sparse_attention(q, iq, iw, k, v, ik):
    B, S = q.shape[0], q.shape[1]
    topk = min(TOPK_MAX, S // 4)
    nblk = S // Q_BLOCK
    key_pos = jnp.arange(S)
    gather = jax.vmap(lambda a, i: a[i])

    def to_blocks(a):
        return jnp.moveaxis(a.reshape((B, nblk, Q_BLOCK) + a.shape[2:]), 1, 0)

    def block(args):
        qb, iqb, iwb, tb = args
        logits = jnp.einsum('bqhd,bsd->bqhs', iqb, ik, preferred_element_type=jnp.float32) * IDX_DIM ** -0.5
        score = jnp.einsum('bqhs,bqh->bqs', jax.nn.relu(logits), iwb.astype(jnp.float32))
        causal = key_pos[None, :] <= tb[:, None]
        score = jnp.where(causal[None], score, -jnp.inf)
        _, sel = lax.top_k(score, topk)
        k_sel = gather(k, sel)
        v_sel = gather(v, sel)
        s = jnp.einsum('bqhd,bqkd->bqhk', qb, k_sel, preferred_element_type=jnp.float32) * HEAD_DIM ** -0.5
        valid = sel <= tb[None, :, None]
        s = jnp.where(valid[:, :, None, :], s, -jnp.inf)
        pr = jax.nn.softmax(s, axis=-1)
        return jnp.einsum('bqhk,bqkd->bqhd', pr.astype(v.dtype), v_sel)

    t_blocks = jnp.arange(S).reshape(nblk, Q_BLOCK)
    out = lax.map(block, (to_blocks(q), to_blocks(iq), to_blocks(iw), t_blocks))
    return jnp.moveaxis(out, 0, 1).reshape(B, S, -1)


def rwkv7_time_mix(zb, mu, w0, w2, a0, a2, g2, k_k, k_a, r_k, lnx_g, lnx_b):
    f32 = jnp.float32
    B, S = zb.shape[0], zb.shape[1]
    zb = zb + (token_shift(zb) - zb) * mu
    r, k, v, wl, al, gl = _split(zb, B_SIZES)
    w = -jax.nn.softplus(-(w0 + jnp.tanh(wl) @ w2)) - 0.5
    decay = jnp.exp(-jnp.exp(w.astype(f32)))
    a = jax.nn.sigmoid(a0 + al @ a2)
    g = jax.nn.sigmoid(gl) @ g2
    heads = lambda t: t.astype(f32).reshape(B, S, RWKV_HEADS, HEAD_DIM)
    kk = heads(k * k_k)
    kk = kk / jnp.maximum(jnp.sqrt(jnp.sum(kk * kk, -1, keepdims=True)), 1e-12)
    k = k * (1.0 + (a - 1.0) * k_a)
    r_h, k_h, v_h, a_h, w_h = heads(r), heads(k), heads(v), heads(a), heads(decay)
    xs = tuple(jnp.moveaxis(t, 1, 0) for t in (r_h, w_h, k_h, v_h, -kk, kk * a_h))

    def step(state, inp):
        r_t, w_t, k_t, v_t, a_t, b_t = inp
        sa = jnp.einsum('bhij,bhj->bhi', state, a_t)
        state = state * w_t[:, :, None, :] + sa[..., None] * b_t[:, :, None, :] + v_t[..., None] * k_t[:, :, None, :]
        return state, jnp.einsum('bhij,bhj->bhi', state, r_t)

    s0 = jnp.zeros((B, RWKV_HEADS, HEAD_DIM, HEAD_DIM), f32)
    _, y = lax.scan(step, s0, xs)
    y = jnp.moveaxis(y, 0, 1)
    ym = y.mean(-1, keepdims=True)
    yv = jnp.square(y - ym).mean(-1, keepdims=True)
    yn = ((y - ym) * lax.rsqrt(yv + GN_EPS)).reshape(B, S, RWKV_WIDTH) * lnx_g + lnx_b
    bonus = (jnp.sum(r_h * k_h * r_k, -1, keepdims=True) * v_h).reshape(B, S, RWKV_WIDTH)
    return ((yn + bonus) * g).astype(zb.dtype)


def multiscale_pool(zc, pool_w, pool_scale):
    B, S = zc.shape[0], zc.shape[1]
    zf = zc.astype(jnp.float32).reshape(B, S, len(POOL_WINDOWS), POOL_GROUP)
    cs = jnp.pad(jnp.cumsum(zf, axis=1), ((0, 0), (1, 0), (0, 0), (0, 0)))
    t = jnp.arange(S)
    outs = []
    for gi, win in enumerate(POOL_WINDOWS):
        c = cs[:, :, gi]
        lagged = jnp.pad(c, ((0, 0), (win, 0), (0, 0)))[:, :S + 1]
        wsum = c[:, 1:] - lagged[:, 1:]
        count = jnp.minimum(t + 1, win).astype(jnp.float32)[None, :, None]
        outs.append(wsum / count - zf[:, :, gi])
    pooled = jnp.stack(outs, axis=2).astype(zc.dtype)
    mixed = jnp.einsum('bsgc,gcd->bsgd', pooled, pool_w)
    return mixed.reshape(B, S, POOL_WIDTH) * pool_scale


def hybrid_mixer(x, cos, sin, w_in, w_uq, w_uqi, attn_norm_g, rwkv_mu, rwkv_w0, rwkv_w2, rwkv_a0,
                 rwkv_a2, rwkv_g2, rwkv_k_k, rwkv_k_a, rwkv_r_k, rwkv_lnx_g, rwkv_lnx_b, pool_w,
                 pool_scale, w_out):
    B, S = x.shape[0], x.shape[1]
    z = x @ w_in
    za, zb, zc = _split(z, (A_COLS, B_COLS, POOL_WIDTH))
    cq, ka, va, ik, iw = _split(za, A_SIZES)
    q = rotary((cq @ w_uq).reshape(B, S, ATT_HEADS, HEAD_DIM), cos, sin)
    iq = rotary((cq @ w_uqi).reshape(B, S, IDX_HEADS, IDX_DIM), cos, sin)
    ka = rotary(ka[:, :, None, :], cos, sin)[:, :, 0]
    ik = rotary(ik[:, :, None, :], cos, sin)[:, :, 0]
    o_a = dsa_sparse_attention(q, iq, iw * IDX_HEADS ** -0.5, ka, va, ik)
    o_a = rms_norm(o_a, attn_norm_g)
    o_b = rwkv7_time_mix(zb, rwkv_mu, rwkv_w0, rwkv_w2, rwkv_a0, rwkv_a2, rwkv_g2, rwkv_k_k,
                         rwkv_k_a, rwkv_r_k, rwkv_lnx_g, rwkv_lnx_b)
    o_c = multiscale_pool(zc, pool_w, pool_scale)
    return jnp.concatenate([o_a, o_b, o_c], axis=-1) @ w_out


def setup_inputs(seed: int = 0) -> dict:
    key = jax.random.key(seed)
    keys = jax.random.split(key, 40)
    ctr = [0]
    f32 = jnp.float32

    def nk():
        ctr[0] += 1
        return keys[ctr[0] - 1]

    def nrm(shape, scale):
        return jax.random.normal(nk(), shape, f32) * scale

    def gain(shape):
        return 1.0 + nrm(shape, 0.02)

    def unif(shape, lo, hi):
        return jax.random.uniform(nk(), shape, f32, lo, hi)

    L = DEPTH
    return {
        'x': nrm((BATCH, SEQ, D_MODEL), 1.0),
        'p': nrm((DEPTH, BATCH, SEQ, PE_DIM), 1.0),
        'positions': jnp.tile(jnp.arange(SEQ, dtype=jnp.int32)[None, :], (BATCH, 1)),
        'ln_emb_g': gain((D_MODEL,)),
        'ln_emb_b': nrm((D_MODEL,), 0.01),
        'w_in': nrm((L, D_MODEL, N_IN), D_MODEL ** -0.5),
        'w_uq': nrm((L, Q_RANK, ATT_WIDTH), Q_RANK ** -0.5),
        'w_uqi': nrm((L, Q_RANK, IDX_HEADS * IDX_DIM), Q_RANK ** -0.5),
        'attn_norm_g': gain((L, ATT_WIDTH)),
        'rwkv_mu': unif((L, B_COLS), 0.0, 1.0),
        'rwkv_w0': unif((L, RWKV_WIDTH), -6.0, -1.0),
        'rwkv_w2': nrm((L, DECAY_RANK, RWKV_WIDTH), 0.5 * DECAY_RANK ** -0.5),
        'rwkv_a0': nrm((L, RWKV_WIDTH), 0.1),
        'rwkv_a2': nrm((L, ICLR_RANK, RWKV_WIDTH), ICLR_RANK ** -0.5),
        'rwkv_g2': nrm((L, GATE_RANK, RWKV_WIDTH), GATE_RANK ** -0.5),
        'rwkv_k_k': 0.85 + nrm((L, RWKV_WIDTH), 0.02),
        'rwkv_k_a': gain((L, RWKV_WIDTH)),
        'rwkv_r_k': nrm((L, RWKV_HEADS, HEAD_DIM), 0.1),
        'rwkv_lnx_g': gain((L, RWKV_WIDTH)),
        'rwkv_lnx_b': nrm((L, RWKV_WIDTH), 0.01),
        'pool_w': nrm((L, len(POOL_WINDOWS), POOL_GROUP, POOL_GROUP), POOL_GROUP ** -0.5),
        'pool_scale': gain((L, POOL_WIDTH)),
        'w_out': nrm((L, MIX_WIDTH, D_MODEL), BETA * MIX_WIDTH ** -0.5),
        'ln1_g': gain((L, D_MODEL)),
        'ln1_b': nrm((L, D_MODEL), 0.01),
        'mlp_w1': nrm((L, D_MODEL, D_FF), D_MODEL ** -0.5),
        'mlp_w2': nrm((L, D_FF, D_MODEL), BETA * D_FF ** -0.5),
        'pe_proj': nrm((L, PE_DIM, D_MODEL), BETA * PE_DIM ** -0.5),
        'pe_gate': nrm((L, D_MODEL, D_MODEL), D_MODEL ** -0.5),
        'pe_gate_b': nrm((L, D_MODEL), 0.01),
        'ln2_g': gain((L, D_MODEL)),
        'ln2_b': nrm((L, D_MODEL), 0.01),
    }


def reference(x, p, positions, ln_emb_g, ln_emb_b, w_in, w_uq, w_uqi, attn_norm_g, rwkv_mu, rwkv_w0,
              rwkv_w2, rwkv_a0, rwkv_a2, rwkv_g2, rwkv_k_k, rwkv_k_a, rwkv_r_k, rwkv_lnx_g, rwkv_lnx_b,
              pool_w, pool_scale, w_out, ln1_g, ln1_b, mlp_w1, mlp_w2, pe_proj, pe_gate, pe_gate_b,
              ln2_g, ln2_b):
    cos, sin = rotary_tables(positions)
    x = layer_norm(x, ln_emb_g, ln_emb_b)
    for i in range(DEPTH):
        mix = hybrid_mixer(x, cos, sin, w_in[i], w_uq[i], w_uqi[i], attn_norm_g[i], rwkv_mu[i],
                           rwkv_w0[i], rwkv_w2[i], rwkv_a0[i], rwkv_a2[i], rwkv_g2[i], rwkv_k_k[i],
                           rwkv_k_a[i], rwkv_r_k[i], rwkv_lnx_g[i], rwkv_lnx_b[i], pool_w[i],
                           pool_scale[i], w_out[i])
        x = layer_norm(ALPHA * x + mix, ln1_g[i], ln1_b[i])
        ffn = jnp.square(jax.nn.relu(x @ mlp_w1[i])) @ mlp_w2[i]
        ple = jax.nn.sigmoid(x @ pe_gate[i] + pe_gate_b[i]) * (p[i] @ pe_proj[i])
        x = layer_norm(ALPHA * x + ffn + ple, ln2_g[i], ln2_b[i])
    return x
```

```python
import functools

import numpy as np
import jax
import jax.numpy as jnp
from jax import lax
from jax.experimental import pallas as pl
from jax.experimental.pallas import tpu as pltpu

D_MODEL = 1024
PE_DIM = 256
HEAD_DIM = 64
ROPE_THETA = 10000.0
ATT_WIDTH = 384
ATT_HEADS = 6
Q_RANK = 256
IDX_HEADS = 8
IDX_DIM = 64
TOPK_MAX = 256
RWKV_WIDTH = 384
RWKV_HEADS = 6
DECAY_RANK = 64
ICLR_RANK = 64
GATE_RANK = 128
GN_EPS = 64e-5
POOL_WINDOWS = (2, 4, 8, 16)
POOL_WIDTH = 256
POOL_GROUP = 64
B_COLS = 3 * RWKV_WIDTH + DECAY_RANK + ICLR_RANK + GATE_RANK
D_FF = 4 * D_MODEL
LN_EPS = 1e-5
RMS_EPS = 1e-6

COL_CQ = 0
COL_KIK = 256
COL_KIK_P = 384
COL_VIW = 512
COL_B = 640
COL_C = COL_B + B_COLS
N_IN_P = COL_C + POOL_WIDTH

LANES = 128
INT_MIN = -2 ** 31
VMEM_LIMIT = 56 * 1024 * 1024

TM_PROJ = 512
TQ = 128
CHUNK = 64
TM_MLP = 512
TF_MLP = 1024

_NT = (((1,), (1,)), ((), ()))
_TN = (((0,), (0,)), ((), ()))


def _params(*sem):
    return pltpu.CompilerParams(dimension_semantics=sem, vmem_limit_bytes=VMEM_LIMIT)


def _ln(x, g, b):
    mu = jnp.mean(x, axis=-1, keepdims=True)
    xc = x - mu
    var = jnp.mean(xc * xc, axis=-1, keepdims=True)
    return xc * lax.rsqrt(var + LN_EPS) * g + b


def _bf(x):
    return x.astype(jnp.bfloat16)


def _dot(a, b):
    return jnp.dot(a, b, preferred_element_type=jnp.float32)


def _dot_nt(a, b):
    return lax.dot_general(a, b, _NT, preferred_element_type=jnp.float32)


def _dot_tn(a, b):
    return lax.dot_general(a, b, _TN, preferred_element_type=jnp.float32)


def _rope_kernel(pos_ref, inv_ref, cos_ref, sin_ref):
    ang = pos_ref[...] * inv_ref[...]
    cos_ref[...] = jnp.cos(ang)
    sin_ref[...] = jnp.sin(ang)


def _rope_tables(pos_f32):
    T = pos_f32.shape[0]
    tm = min(T, 2048)
    inv = ROPE_THETA ** (-np.arange(0, HEAD_DIM, 2, dtype=np.float32) / HEAD_DIM)
    inv128 = jnp.asarray(np.tile(inv, LANES // (HEAD_DIM // 2))[None, :], jnp.float32)
    return pl.pallas_call(
        _rope_kernel,
        out_shape=(jax.ShapeDtypeStruct((T, LANES), jnp.float32),) * 2,
        grid=(T // tm,),
        in_specs=[pl.BlockSpec((tm, 1), lambda i: (i, 0)),
                  pl.BlockSpec((1, LANES), lambda i: (0, 0))],
        out_specs=(pl.BlockSpec((tm, LANES), lambda i: (i, 0)),) * 2,
        compiler_params=_params("parallel"),
        name="rope_tables",
    )(pos_f32, inv128)


def _inproj_kernel(apply_ln, x_ref, g_ref, b_ref, cos_ref, sin_ref, win_ref, wq_ref, wqp_ref,
                   wi_ref, wip_ref, *out_refs):
    if apply_ln:
        xn_ref, q_ref, iq_ref, iw_ref, kr_ref, ikr_ref, va_ref, zb_ref, zc_ref = out_refs
        x = _ln(x_ref[...], g_ref[...], b_ref[...])
        xn_ref[...] = x
    else:
        q_ref, iq_ref, iw_ref, kr_ref, ikr_ref, va_ref, zb_ref, zc_ref = out_refs
        x = x_ref[...]
    z = _dot(_bf(x), win_ref[...])
    cos = cos_ref[...]
    sin = sin_ref[...]
    cq = _bf(z[:, COL_CQ:COL_CQ + Q_RANK])
    cos3 = jnp.concatenate([cos] * 3, axis=1)
    sin3 = jnp.concatenate([sin] * 3, axis=1)
    q_ref[...] = _bf(_dot(cq, wq_ref[...]) * cos3 + _dot(cq, wqp_ref[...]) * sin3)
    cos4 = jnp.concatenate([cos] * 4, axis=1)
    sin4 = jnp.concatenate([sin] * 4, axis=1)
    iq_ref[...] = _bf(_dot(cq, wi_ref[...]) * cos4 + _dot(cq, wip_ref[...]) * sin4)
    kik = z[:, COL_KIK:COL_KIK + LANES] * cos + z[:, COL_KIK_P:COL_KIK_P + LANES] * sin
    kr_ref[...] = _bf(kik[:, :HEAD_DIM])
    ikr_ref[...] = _bf(kik[:, HEAD_DIM:])
    viw = z[:, COL_VIW:COL_VIW + LANES]
    va_ref[...] = _bf(viw[:, :HEAD_DIM])
    iw_ref[...] = viw * (IDX_HEADS ** -0.5)
    zb_ref[...] = z[:, COL_B:COL_B + B_COLS]
    zc_ref[...] = z[:, COL_C:COL_C + POOL_WIDTH]


def _inproj(x, g, b, cos, sin, win, wq, wqp, wi, wip, apply_ln):
    T = x.shape[0]
    tm = min(T, TM_PROJ)
    row = lambda i: (i, 0)
    fix = lambda i: (0, 0)
    bf16, f32 = jnp.bfloat16, jnp.float32
    outs = [(ATT_WIDTH, bf16), (IDX_HEADS * IDX_DIM, bf16), (LANES, f32), (HEAD_DIM, bf16),
            (HEAD_DIM, bf16), (HEAD_DIM, bf16), (B_COLS, f32), (POOL_WIDTH, f32)]
    if apply_ln:
        outs = [(D_MODEL, f32)] + outs
    return pl.pallas_call(
        functools.partial(_inproj_kernel, apply_ln),
        out_shape=tuple(jax.ShapeDtypeStruct((T, n), dt) for n, dt in outs),
        grid=(T // tm,),
        in_specs=[pl.BlockSpec((tm, D_MODEL), row),
                  pl.BlockSpec((1, D_MODEL), fix), pl.BlockSpec((1, D_MODEL), fix),
                  pl.BlockSpec((tm, LANES), row), pl.BlockSpec((tm, LANES), row),
                  pl.BlockSpec((D_MODEL, N_IN_P), fix),
                  pl.BlockSpec((Q_RANK, ATT_WIDTH), fix), pl.BlockSpec((Q_RANK, ATT_WIDTH), fix),
                  pl.BlockSpec((Q_RANK, IDX_HEADS * IDX_DIM), fix),
                  pl.BlockSpec((Q_RANK, IDX_HEADS * IDX_DIM), fix)],
        out_specs=tuple(pl.BlockSpec((tm, n), row) for n, _ in outs),
        compiler_params=_params("parallel"),
        name="inproj",
    )(x, g, b, cos, sin, win, wq, wqp, wi, wip)


def _attn_kernel(S, q_ref, iq_ref, iw_ref, kr_ref, ikr_ref, va_ref, g_ref, o_ref, key_ref, sel_ref):
    tq = q_ref.shape[0]
    topk = min(TOPK_MAX, S // 4)
    t0 = pl.program_id(1) * tq
    row_t = t0 + lax.broadcasted_iota(jnp.int32, (tq, 1), 0)
    col = lax.broadcasted_iota(jnp.int32, (1, S), 1)
    causal = col <= row_t

    ik = ikr_ref[...]
    iw = iw_ref[...]
    score = jnp.zeros((tq, S), jnp.float32)
    for h in range(IDX_HEADS):
        lg = _dot_nt(iq_ref[:, h * IDX_DIM:(h + 1) * IDX_DIM], ik)
        score = score + jnp.maximum(lg, 0.0) * iw[:, HEAD_DIM + h:HEAD_DIM + h + 1]
    score = score + 0.0
    bits = pltpu.bitcast(score, jnp.int32)
    key = bits ^ ((bits >> 31) & 0x7FFFFFFF)
    key_ref[...] = jnp.where(causal, key, INT_MIN)

    def count_ge(cand):
        return jnp.sum(jnp.where(key_ref[...] >= cand, 1.0, 0.0), axis=1, keepdims=True)

    tau = jnp.where(count_ge(jnp.zeros((tq, 1), jnp.int32)) >= topk, 0, INT_MIN).astype(jnp.int32)

    def bit_step(i, tau):
        cand = tau | (jnp.int32(1) << (30 - i))
        return jnp.where(count_ge(cand) >= topk, cand, tau)

    tau = lax.fori_loop(0, 31, bit_step, tau)
    tau = jnp.maximum(tau, INT_MIN + 1)
    keyv = key_ref[...]
    sel = jnp.where(keyv >= tau, 1.0, 0.0)
    n_ge = jnp.sum(sel, axis=1, keepdims=True)
    sel_ref[...] = sel

    @pl.when(jnp.max(n_ge) > topk)
    def _():
        gt = keyv > tau
        eq = keyv == tau
        need = topk - jnp.sum(jnp.where(gt, 1.0, 0.0), axis=1, keepdims=True)

        def idx_step(i, m):
            cand = m | (jnp.int32(1) << (11 - i))
            cnt = jnp.sum(jnp.where(eq & (col < cand), 1.0, 0.0), axis=1, keepdims=True)
            return jnp.where(cnt <= need, cand, m)

        m = lax.fori_loop(0, 12, idx_step, jnp.zeros((tq, 1), jnp.int32))
        sel_ref[...] = jnp.where(gt | (eq & (col < m)), 1.0, 0.0)

    selected = sel_ref[...] > 0.5
    kr = kr_ref[...]
    va = va_ref[...]
    outs = []
    for h in range(ATT_HEADS):
        s = _dot_nt(q_ref[:, h * HEAD_DIM:(h + 1) * HEAD_DIM], kr)
        s = jnp.where(selected, s, -jnp.inf)
        p = jnp.exp(s - jnp.max(s, axis=1, keepdims=True))
        l = jnp.sum(p, axis=1, keepdims=True)
        outs.append(_dot(_bf(p), va) / l)
    o = jnp.concatenate(outs, axis=1)
    o = o * lax.rsqrt(jnp.mean(o * o, axis=-1, keepdims=True) + RMS_EPS) * g_ref[...]
    o_ref[...] = _bf(o)


def _attention(B, S, q, iq, iw, kr, ikr, va, g):
    tq = min(S, TQ)
    nq = S // tq
    qrow = lambda b, i: (b * nq + i, 0)
    seq = lambda b, i: (b, 0)
    fix = lambda b, i: (0, 0)
    return pl.pallas_call(
        functools.partial(_attn_kernel, S),
        out_shape=jax.ShapeDtypeStruct((B * S, ATT_WIDTH), jnp.bfloat16),
        grid=(B, nq),
        in_specs=[pl.BlockSpec((tq, ATT_WIDTH), qrow),
                  pl.BlockSpec((tq, IDX_HEADS * IDX_DIM), qrow),
                  pl.BlockSpec((tq, LANES), qrow),
                  pl.BlockSpec((S, HEAD_DIM), seq), pl.BlockSpec((S, HEAD_DIM), seq),
                  pl.BlockSpec((S, HEAD_DIM), seq),
                  pl.BlockSpec((1, ATT_WIDTH), fix)],
        out_specs=pl.BlockSpec((tq, ATT_WIDTH), qrow),
        scratch_shapes=[pltpu.VMEM((tq, S), jnp.int32), pltpu.VMEM((tq, S), jnp.float32)],
        compiler_params=_params("parallel", "parallel"),
        name="dsa_attention",
    )(q, iq, iw, kr, ikr, va, g)


def _split3(x):
    hi = _bf(x)
    r1 = x - hi.astype(jnp.float32)
    mid = _bf(r1)
    lo = _bf(r1 - mid.astype(jnp.float32))
    return hi, mid, lo


def _rwkv_kernel(zb_ref, mu_ref, w0_ref, w2_ref, a0_ref, a2_ref, g2_ref, kk_ref, ka_ref, rk_ref,
                 lg_ref, lb_ref, o_ref, prev_ref, state_ref):
    C = zb_ref.shape[0]
    W = RWKV_WIDTH

    @pl.when(pl.program_id(1) == 0)
    def _():
        prev_ref[...] = jnp.zeros_like(prev_ref)
        state_ref[...] = jnp.zeros_like(state_ref)

    x = zb_ref[...]
    rowi = lax.broadcasted_iota(jnp.int32, (C, 1), 0)
    shifted = jnp.where(rowi == 0, prev_ref[0:1, :], pltpu.roll(x, 1, 0))
    prev_ref[0:1, :] = x[C - 1:C, :]
    xm = x + (shifted - x) * mu_ref[...]
    r = xm[:, 0:W]
    k = xm[:, W:2 * W]
    v = xm[:, 2 * W:3 * W]
    wl = xm[:, 3 * W:3 * W + DECAY_RANK]
    al = xm[:, 3 * W + DECAY_RANK:3 * W + DECAY_RANK + ICLR_RANK]
    gl = xm[:, 3 * W + DECAY_RANK + ICLR_RANK:]

    wpre = w0_ref[...] + _dot(_bf(jnp.tanh(wl)), w2_ref[...])
    nw = -wpre
    w = -(jnp.maximum(nw, 0.0) + jnp.log(1.0 + jnp.exp(-jnp.abs(nw)))) - 0.5
    logd = -jnp.exp(w)
    a = jax.nn.sigmoid(a0_ref[...] + _dot(_bf(al), a2_ref[...]))
    g = _dot(_bf(jax.nn.sigmoid(gl)), g2_ref[...])

    ri = lax.broadcasted_iota(jnp.int32, (C, C), 0)
    ci = lax.broadcasted_iota(jnp.int32, (C, C), 1)
    tri = _bf(jnp.where(ri >= ci, 1.0, 0.0))
    hi, mid, lo = _split3(logd)
    cw = _dot(tri, hi) + _dot(tri, mid) + _dot(tri, lo)
    e_in = jnp.exp(cw)
    e_ex = jnp.exp(cw - logd)
    e_inv = jnp.exp(-cw)

    kk = k * kk_ref[...]
    k2 = k * (1.0 + (a - 1.0) * ka_ref[...])
    rk2 = r * k2 * rk_ref[...]
    strict = ri > ci
    ri2 = lax.broadcasted_iota(jnp.int32, (C, 2 * C), 0)
    ci2 = lax.broadcasted_iota(jnp.int32, (C, 2 * C), 1)
    incl2 = ri2 >= jnp.where(ci2 >= C, ci2 - C, ci2)
    eye = jnp.where(ri == ci, 1.0, 0.0)
    outs = []
    for h in range(RWKV_HEADS):
        sl = slice(h * HEAD_DIM, (h + 1) * HEAD_DIM)
        kkh = kk[:, sl]
        kkh = kkh / jnp.maximum(jnp.sqrt(jnp.sum(kkh * kkh, axis=-1, keepdims=True)), 1e-12)
        a_h = a[:, sl]
        v_h = v[:, sl]
        at = -kkh * e_ex[:, sl]
        rt = r[:, sl] * e_in[:, sl]
        bt = kkh * a_h * e_inv[:, sl]
        kt = k2[:, sl] * e_inv[:, sl]
        bk = jnp.concatenate([bt, kt], axis=0)
        bk16 = _bf(bk)
        v16 = _bf(v_h)
        l_ab = jnp.where(strict, _dot_nt(_bf(at), _bf(bt)), 0.0)
        a_ak = jnp.where(strict, _dot_nt(_bf(at), _bf(kt)), 0.0)
        tinv = eye + l_ab
        pw = l_ab
        n = 2
        while n < C:
            pw = _dot(pw, pw)
            tinv = tinv + _dot(tinv, pw)
            n *= 2
        s0 = state_ref[h]
        rhs = _dot_nt(_bf(at), _bf(s0)) + _dot(_bf(a_ak), v16)
        u = _dot(tinv, rhs)
        uv16 = _bf(jnp.concatenate([u, v_h], axis=0))
        g_r = _dot_nt(_bf(rt), bk16)
        g_r = jnp.where(incl2, g_r, 0.0)
        y = _dot(_bf(g_r), uv16) + _dot_nt(_bf(rt), _bf(s0))
        s_new = (s0 + _dot_tn(uv16, bk16)) * e_in[C - 1:C, sl]
        state_ref[h] = s_new
        ym = jnp.mean(y, axis=-1, keepdims=True)
        yc = y - ym
        yv = jnp.mean(yc * yc, axis=-1, keepdims=True)
        yn = yc * lax.rsqrt(yv + GN_EPS) * lg_ref[:, sl] + lb_ref[:, sl]
        bonus = jnp.sum(rk2[:, sl], axis=-1, keepdims=True) * v_h
        outs.append((yn + bonus) * g[:, sl])
    o_ref[...] = _bf(jnp.concatenate(outs, axis=1))


def _rwkv(B, S, zb, mu, w0, w2, a0, a2, g2, k_k, k_a, r_k, lnx_g, lnx_b):
    C = min(S, CHUNK)
    nc = S // C
    blk = lambda b, c: (b * nc + c, 0)
    fix = lambda b, c: (0, 0)
    vec = lambda n: pl.BlockSpec((1, n), fix)
    return pl.pallas_call(
        _rwkv_kernel,
        out_shape=jax.ShapeDtypeStruct((B * S, RWKV_WIDTH), jnp.bfloat16),
        grid=(B, nc),
        in_specs=[pl.BlockSpec((C, B_COLS), blk), vec(B_COLS), vec(RWKV_WIDTH),
                  pl.BlockSpec((DECAY_RANK, RWKV_WIDTH), fix), vec(RWKV_WIDTH),
                  pl.BlockSpec((ICLR_RANK, RWKV_WIDTH), fix),
                  pl.BlockSpec((GATE_RANK, RWKV_WIDTH), fix),
                  vec(RWKV_WIDTH), vec(RWKV_WIDTH), vec(RWKV_WIDTH), vec(RWKV_WIDTH),
                  vec(RWKV_WIDTH)],
        out_specs=pl.BlockSpec((C, RWKV_WIDTH), blk),
        scratch_shapes=[pltpu.VMEM((8, B_COLS), jnp.float32),
                        pltpu.VMEM((RWKV_HEADS, HEAD_DIM, HEAD_DIM), jnp.float32)],
        compiler_params=_params("parallel", "arbitrary"),
        name="rwkv7",
    )(zb, mu, w0, w2, a0, a2, g2, k_k, k_a, r_k, lnx_g, lnx_b)


def _pool_kernel(zc_ref, w_ref, sc_ref, o_ref):
    S = zc_ref.shape[0]
    x = zc_ref[...]
    row = lax.broadcasted_iota(jnp.int32, (S, 1), 0)
    lane_grp = lax.broadcasted_iota(jnp.int32, (1, POOL_WIDTH), 1) // POOL_GROUP

    def lag(y, n):
        return jnp.where(row >= n, pltpu.roll(y, n, 0), 0.0)

    w2 = x + lag(x, 1)
    w4 = w2 + lag(w2, 2)
    w8 = w4 + lag(w4, 4)
    w16 = w8 + lag(w8, 8)
    wsum = jnp.where(lane_grp == 0, w2, jnp.where(lane_grp == 1, w4, jnp.where(lane_grp == 2, w8, w16)))
    win = jnp.where(lane_grp == 0, 2, jnp.where(lane_grp == 1, 4, jnp.where(lane_grp == 2, 8, 16)))
    count = jnp.minimum(row + 1, win).astype(jnp.float32)
    pooled = wsum / count - x
    o_ref[...] = _bf(_dot(_bf(pooled), w_ref[...]) * sc_ref[...])


def _pool(B, S, zc, w_bd, scale):
    return pl.pallas_call(
        _pool_kernel,
        out_shape=jax.ShapeDtypeStruct((B * S, POOL_WIDTH), jnp.bfloat16),
        grid=(B,),
        in_specs=[pl.BlockSpec((S, POOL_WIDTH), lambda b: (b, 0)),
                  pl.BlockSpec((POOL_WIDTH, POOL_WIDTH), lambda b: (0, 0)),
                  pl.BlockSpec((1, POOL_WIDTH), lambda b: (0, 0))],
        out_specs=pl.BlockSpec((S, POOL_WIDTH), lambda b: (b, 0)),
        compiler_params=_params("parallel"),
        name="pool",
    )(zc, w_bd, scale)


def _outproj_kernel(alpha, oa_ref, ob_ref, oc_ref, x_ref, wa_ref, wb_ref, wc_ref, g_ref, b_ref, o_ref):
    mix = _dot(oa_ref[...], wa_ref[...]) + _dot(ob_ref[...], wb_ref[...]) + _dot(oc_ref[...], wc_ref[...])
    o_ref[...] = _ln(alpha * x_ref[...] + mix, g_ref[...], b_ref[...])


def _outproj(alpha, oa, ob, oc, x, wa, wb, wc, g, b):
    T = x.shape[0]
    tm = min(T, TM_PROJ)
    row = lambda i: (i, 0)
    fix = lambda i: (0, 0)
    return pl.pallas_call(
        functools.partial(_outproj_kernel, alpha),
        out_shape=jax.ShapeDtypeStruct((T, D_MODEL), jnp.float32),
        grid=(T // tm,),
        in_specs=[pl.BlockSpec((tm, ATT_WIDTH), row), pl.BlockSpec((tm, RWKV_WIDTH), row),
                  pl.BlockSpec((tm, POOL_WIDTH), row), pl.BlockSpec((tm, D_MODEL), row),
                  pl.BlockSpec((ATT_WIDTH, D_MODEL), fix), pl.BlockSpec((RWKV_WIDTH, D_MODEL), fix),
                  pl.BlockSpec((POOL_WIDTH, D_MODEL), fix),
                  pl.BlockSpec((1, D_MODEL), fix), pl.BlockSpec((1, D_MODEL), fix)],
        out_specs=pl.BlockSpec((tm, D_MODEL), row),
        compiler_params=_params("parallel"),
        name="outproj_ln1",
    )(oa, ob, oc, x, wa, wb, wc, g, b)


def _mlp_kernel(alpha, x_ref, p_ref, w1_ref, w2_ref, pg_ref, pgb_ref, pp_ref, g_ref, b_ref, o_ref,
                xb_ref, acc_ref):
    kf = pl.program_id(1)

    @pl.when(kf == 0)
    def _():
        x = x_ref[...]
        xb = _bf(x)
        xb_ref[...] = xb
        gate = jax.nn.sigmoid(_dot(xb, pg_ref[...]) + pgb_ref[...])
        acc_ref[...] = alpha * x + gate * _dot(_bf(p_ref[...]), pp_ref[...])

    h = jnp.maximum(_dot(xb_ref[...], w1_ref[...]), 0.0)
    acc_ref[...] += _dot(_bf(h * h), w2_ref[...])

    @pl.when(kf == pl.num_programs(1) - 1)
    def _():
        o_ref[...] = _ln(acc_ref[...], g_ref[...], b_ref[...])


def _mlp(alpha, x, p, w1, w2, pg, pgb, pp, g, b):
    T = x.shape[0]
    tm = min(T, TM_MLP)
    tf = TF_MLP
    row = lambda i, k: (i, 0)
    fix = lambda i, k: (0, 0)
    return pl.pallas_call(
        functools.partial(_mlp_kernel, alpha),
        out_shape=jax.ShapeDtypeStruct((T, D_MODEL), jnp.float32),
        grid=(T // tm, D_FF // tf),
        in_specs=[pl.BlockSpec((tm, D_MODEL), row), pl.BlockSpec((tm, PE_DIM), row),
                  pl.BlockSpec((D_MODEL, tf), lambda i, k: (0, k)),
                  pl.BlockSpec((tf, D_MODEL), lambda i, k: (k, 0)),
                  pl.BlockSpec((D_MODEL, D_MODEL), fix), pl.BlockSpec((1, D_MODEL), fix),
                  pl.BlockSpec((PE_DIM, D_MODEL), fix),
                  pl.BlockSpec((1, D_MODEL), fix), pl.BlockSpec((1, D_MODEL), fix)],
        out_specs=pl.BlockSpec((tm, D_MODEL), row),
        scratch_shapes=[pltpu.VMEM((tm, D_MODEL), jnp.bfloat16),
                        pltpu.VMEM((tm, D_MODEL), jnp.float32)],
        compiler_params=_params("parallel", "arbitrary"),
        name="mlp_ple_ln2",
    )(x, p, w1, w2, pg, pgb, pp, g, b)


def _rot_partner(w):
    half = HEAD_DIM // 2
    return jnp.concatenate([-w[..., half:], w[..., :half]], axis=-1)


def _per_head_partner(w, heads):
    r = w.reshape(w.shape[0], heads, HEAD_DIM)
    return _rot_partner(r).reshape(w.shape)


def _arrange_w_in(w):
    cq = w[:, 0:256]
    ka = w[:, 256:320]
    va = w[:, 320:384]
    ik = w[:, 384:448]
    iw = w[:, 448:456]
    rest = w[:, 456:]
    pad = jnp.zeros((w.shape[0], LANES - HEAD_DIM - IDX_HEADS), w.dtype)
    return jnp.concatenate([cq, ka, ik, _rot_partner(ka), _rot_partner(ik), va, iw, pad, rest], axis=1)


def kernel(x, p, positions, ln_emb_g, ln_emb_b, w_in, w_uq, w_uqi, attn_norm_g, rwkv_mu, rwkv_w0, rwkv_w2, rwkv_a0, rwkv_a2, rwkv_g2, rwkv_k_k, rwkv_k_a, rwkv_r_k, rwkv_lnx_g, rwkv_lnx_b, pool_w, pool_scale, w_out, ln1_g, ln1_b, mlp_w1, mlp_w2, pe_proj, pe_gate, pe_gate_b, ln2_g, ln2_b):
    B, S, _ = x.shape
    T = B * S
    depth = w_in.shape[0]
    alpha = (2 * depth) ** 0.25
    bf = lambda a: a.astype(jnp.bfloat16)
    rowv = lambda a: a.reshape(1, -1)

    cos, sin = _rope_tables(positions.astype(jnp.float32).reshape(T, 1))
    xs = x.reshape(T, D_MODEL)
    for i in range(depth):
        win = bf(_arrange_w_in(w_in[i]))
        wq = w_uq[i] * (HEAD_DIM ** -0.5)
        wi = w_uqi[i] * (IDX_DIM ** -0.5)
        res = _inproj(xs, rowv(ln_emb_g), rowv(ln_emb_b), cos, sin, win,
                      bf(wq), bf(_per_head_partner(wq, ATT_HEADS)),
                      bf(wi), bf(_per_head_partner(wi, IDX_HEADS)), apply_ln=(i == 0))
        if i == 0:
            xs, res = res[0], res[1:]
        q, iq, iw, kr, ikr, va, zb, zc = res
        o_a = _attention(B, S, q, iq, iw, kr, ikr, va, rowv(attn_norm_g[i]))
        o_b = _rwkv(B, S, zb, rowv(rwkv_mu[i]), rowv(rwkv_w0[i]), bf(rwkv_w2[i]), rowv(rwkv_a0[i]),
                    bf(rwkv_a2[i]), bf(rwkv_g2[i]), rowv(rwkv_k_k[i]), rowv(rwkv_k_a[i]),
                    rowv(rwkv_r_k[i]), rowv(rwkv_lnx_g[i]), rowv(rwkv_lnx_b[i]))
        w_bd = jax.scipy.linalg.block_diag(*[pool_w[i, gi] for gi in range(len(POOL_WINDOWS))])
        o_c = _pool(B, S, zc, bf(w_bd), rowv(pool_scale[i]))
        wo = bf(w_out[i])
        xs = _outproj(alpha, o_a, o_b, o_c, xs, wo[:ATT_WIDTH], wo[ATT_WIDTH:ATT_WIDTH + RWKV_WIDTH],
                      wo[ATT_WIDTH + RWKV_WIDTH:], rowv(ln1_g[i]), rowv(ln1_b[i]))
        xs = _mlp(alpha, xs, p[i].reshape(T, PE_DIM), bf(mlp_w1[i]), bf(mlp_w2[i]), bf(pe_gate[i]),
                  rowv(pe_gate_b[i]), bf(pe_proj[i]), rowv(ln2_g[i]), rowv(ln2_b[i]))
    return xs.reshape(B, S, D_MODEL)
```

```python
import functools

import numpy as np
import jax
import jax.numpy as jnp
from jax import lax
from jax.experimental import pallas as pl
from jax.experimental.pallas import tpu as pltpu

D_MODEL = 1024
PE_DIM = 256
HEAD_DIM = 64
ROPE_THETA = 10000.0
ATT_WIDTH = 384
ATT_HEADS = 6
Q_RANK = 256
IDX_HEADS = 8
IDX_DIM = 64
TOPK_MAX = 256
RWKV_WIDTH = 384
RWKV_HEADS = 6
DECAY_RANK = 64
ICLR_RANK = 64
GATE_RANK = 128
GN_EPS = 64e-5
POOL_WINDOWS = (2, 4, 8, 16)
POOL_WIDTH = 256
POOL_GROUP = 64
B_COLS = 3 * RWKV_WIDTH + DECAY_RANK + ICLR_RANK + GATE_RANK
D_FF = 4 * D_MODEL
LN_EPS = 1e-5
RMS_EPS = 1e-6

COL_CQ = 0
COL_KIK = 256
COL_KIK_P = 384
COL_VIW = 512
COL_B = 640
COL_C = COL_B + B_COLS
N_IN_P = COL_C + POOL_WIDTH

LANES = 128
INT_MIN = -2 ** 31
VMEM_LIMIT = 56 * 1024 * 1024

TM_PROJ = 512
TQ = 256
CHUNK = 64
RWKV_ROWS = 4
TM_MLP = 512
TF_MLP = 1024

_NT = (((1,), (1,)), ((), ()))
_TN = (((0,), (0,)), ((), ()))


def _params(*sem):
    return pltpu.CompilerParams(dimension_semantics=sem, vmem_limit_bytes=VMEM_LIMIT)


def _ln(x, g, b):
    mu = jnp.mean(x, axis=-1, keepdims=True)
    xc = x - mu
    var = jnp.mean(xc * xc, axis=-1, keepdims=True)
    return xc * lax.rsqrt(var + LN_EPS) * g + b


def _bf(x):
    return x.astype(jnp.bfloat16)


def _dot(a, b):
    return jnp.dot(a, b, preferred_element_type=jnp.float32)


def _dot_nt(a, b):
    return lax.dot_general(a, b, _NT, preferred_element_type=jnp.float32)


def _dot_tn(a, b):
    return lax.dot_general(a, b, _TN, preferred_element_type=jnp.float32)


def _rope_kernel(pos_ref, inv_ref, cos_ref, sin_ref):
    ang = pos_ref[...] * inv_ref[...]
    cos_ref[...] = jnp.cos(ang)
    sin_ref[...] = jnp.sin(ang)


def _rope_tables(pos_f32):
    T = pos_f32.shape[0]
    tm = min(T, 2048)
    inv = ROPE_THETA ** (-np.arange(0, HEAD_DIM, 2, dtype=np.float32) / HEAD_DIM)
    inv128 = jnp.asarray(np.tile(inv, LANES // (HEAD_DIM // 2))[None, :], jnp.float32)
    return pl.pallas_call(
        _rope_kernel,
        out_shape=(jax.ShapeDtypeStruct((T, LANES), jnp.float32),) * 2,
        grid=(T // tm,),
        in_specs=[pl.BlockSpec((tm, 1), lambda i: (i, 0)),
                  pl.BlockSpec((1, LANES), lambda i: (0, 0))],
        out_specs=(pl.BlockSpec((tm, LANES), lambda i: (i, 0)),) * 2,
        compiler_params=_params("parallel"),
        name="rope_tables",
    )(pos_f32, inv128)


def _inproj_kernel(apply_ln, x_ref, g_ref, b_ref, cos_ref, sin_ref, win_ref, wq_ref, wqp_ref,
                   wi_ref, wip_ref, *out_refs):
    if apply_ln:
        xn_ref, q_ref, iq_ref, iw_ref, kr_ref, ikr_ref, va_ref, zb_ref, zc_ref = out_refs
        x = _ln(x_ref[...], g_ref[...], b_ref[...])
        xn_ref[...] = x
    else:
        q_ref, iq_ref, iw_ref, kr_ref, ikr_ref, va_ref, zb_ref, zc_ref = out_refs
        x = x_ref[...]
    z = _dot(_bf(x), win_ref[...])
    cos = cos_ref[...]
    sin = sin_ref[...]
    cq = _bf(z[:, COL_CQ:COL_CQ + Q_RANK])
    cos3 = jnp.concatenate([cos] * 3, axis=1)
    sin3 = jnp.concatenate([sin] * 3, axis=1)
    q_ref[...] = _bf(_dot(cq, wq_ref[...]) * cos3 + _dot(cq, wqp_ref[...]) * sin3)
    cos4 = jnp.concatenate([cos] * 4, axis=1)
    sin4 = jnp.concatenate([sin] * 4, axis=1)
    iq_ref[...] = _bf(_dot(cq, wi_ref[...]) * cos4 + _dot(cq, wip_ref[...]) * sin4)
    kik = z[:, COL_KIK:COL_KIK + LANES] * cos + z[:, COL_KIK_P:COL_KIK_P + LANES] * sin
    kr_ref[...] = _bf(kik[:, :HEAD_DIM])
    ikr_ref[...] = _bf(kik[:, HEAD_DIM:])
    viw = z[:, COL_VIW:COL_VIW + LANES]
    va_ref[...] = _bf(viw[:, :HEAD_DIM])
    iw_ref[...] = viw * (IDX_HEADS ** -0.5)
    zb_ref[...] = z[:, COL_B:COL_B + B_COLS]
    zc_ref[...] = z[:, COL_C:COL_C + POOL_WIDTH]


def _inproj(x, g, b, cos, sin, win, wq, wqp, wi, wip, apply_ln):
    T = x.shape[0]
    tm = min(T, TM_PROJ)
    row = lambda i: (i, 0)
    fix = lambda i: (0, 0)
    bf16, f32 = jnp.bfloat16, jnp.float32
    outs = [(ATT_WIDTH, bf16), (IDX_HEADS * IDX_DIM, bf16), (LANES, f32), (HEAD_DIM, bf16),
            (HEAD_DIM, bf16), (HEAD_DIM, bf16), (B_COLS, f32), (POOL_WIDTH, f32)]
    if apply_ln:
        outs = [(D_MODEL, f32)] + outs
    return pl.pallas_call(
        functools.partial(_inproj_kernel, apply_ln),
        out_shape=tuple(jax.ShapeDtypeStruct((T, n), dt) for n, dt in outs),
        grid=(T // tm,),
        in_specs=[pl.BlockSpec((tm, D_MODEL), row),
                  pl.BlockSpec((1, D_MODEL), fix), pl.BlockSpec((1, D_MODEL), fix),
                  pl.BlockSpec((tm, LANES), row), pl.BlockSpec((tm, LANES), row),
                  pl.BlockSpec((D_MODEL, N_IN_P), fix),
                  pl.BlockSpec((Q_RANK, ATT_WIDTH), fix), pl.BlockSpec((Q_RANK, ATT_WIDTH), fix),
                  pl.BlockSpec((Q_RANK, IDX_HEADS * IDX_DIM), fix),
                  pl.BlockSpec((Q_RANK, IDX_HEADS * IDX_DIM), fix)],
        out_specs=tuple(pl.BlockSpec((tm, n), row) for n, _ in outs),
        compiler_params=_params("parallel"),
        name="inproj",
    )(x, g, b, cos, sin, win, wq, wqp, wi, wip)


def _attn_block(v, topk, q_ref, iq_ref, iw_ref, kr_ref, ikr_ref, va_ref, g_ref, o_ref, key_ref, bias_ref):
    tq = q_ref.shape[0]
    kend = (v + 1) * tq
    row_t = v * tq + lax.broadcasted_iota(jnp.int32, (tq, 1), 0)
    col = lax.broadcasted_iota(jnp.int32, (1, kend), 1)
    causal = col <= row_t

    if kend <= topk:
        bias_ref[:, :kend] = jnp.where(causal, 0.0, -jnp.inf)
    else:
        ik = ikr_ref[0:kend, :]
        iw = iw_ref[...]
        score = jnp.zeros((tq, kend), jnp.float32)
        for h in range(IDX_HEADS):
            lg = _dot_nt(iq_ref[:, h * IDX_DIM:(h + 1) * IDX_DIM], ik)
            score = score + jnp.maximum(lg, 0.0) * iw[:, HEAD_DIM + h:HEAD_DIM + h + 1]
        score = score + 0.0
        bits = pltpu.bitcast(score, jnp.int32)
        key = bits ^ ((bits >> 31) & 0x7FFFFFFF)
        key_ref[:, :kend] = jnp.where(causal, key, INT_MIN)

        def count_ge(cand):
            return jnp.sum(jnp.where(key_ref[:, :kend] >= cand, 1.0, 0.0), axis=1, keepdims=True)

        tau = jnp.where(count_ge(jnp.zeros((tq, 1), jnp.int32)) >= topk, 0, INT_MIN).astype(jnp.int32)

        def bit_step(i, tau):
            cand = tau | (jnp.int32(1) << (30 - i))
            return jnp.where(count_ge(cand) >= topk, cand, tau)

        tau = lax.fori_loop(0, 31, bit_step, tau)
        tau = jnp.maximum(tau, INT_MIN + 1)
        ge = key_ref[:, :kend] >= tau
        n_ge = jnp.sum(jnp.where(ge, 1.0, 0.0), axis=1, keepdims=True)
        bias_ref[:, :kend] = jnp.where(ge, 0.0, -jnp.inf)

        @pl.when(jnp.max(n_ge) > topk)
        def _():
            keyv = key_ref[:, :kend]
            gt = keyv > tau
            eq = keyv == tau
            need = topk - jnp.sum(jnp.where(gt, 1.0, 0.0), axis=1, keepdims=True)
            nbits = int(kend).bit_length()

            def idx_step(i, m):
                cand = m | (jnp.int32(1) << (nbits - 1 - i))
                cnt = jnp.sum(jnp.where(eq & (col < cand), 1.0, 0.0), axis=1, keepdims=True)
                return jnp.where(cnt <= need, cand, m)

            m = lax.fori_loop(0, nbits, idx_step, jnp.zeros((tq, 1), jnp.int32))
            bias_ref[:, :kend] = jnp.where(gt | (eq & (col < m)), 0.0, -jnp.inf)

    bias = bias_ref[:, :kend]
    kr = kr_ref[0:kend, :]
    va = va_ref[0:kend, :]
    outs = []
    for h in range(ATT_HEADS):
        s = _dot_nt(q_ref[:, h * HEAD_DIM:(h + 1) * HEAD_DIM], kr) + bias
        p = jnp.exp(s - jnp.max(s, axis=1, keepdims=True))
        l = jnp.sum(p, axis=1, keepdims=True)
        outs.append(_dot(_bf(p), va) / l)
    o = jnp.concatenate(outs, axis=1)
    o = o * lax.rsqrt(jnp.mean(o * o, axis=-1, keepdims=True) + RMS_EPS) * g_ref[...]
    o_ref[...] = _bf(o)


def _attn_kernel(S, *refs):
    tq = refs[0].shape[0]
    topk = min(TOPK_MAX, S // 4)
    qi = pl.program_id(1)
    for v in range(S // tq):
        pl.when(qi == v)(functools.partial(_attn_block, v, topk, *refs))


def _attention(B, S, q, iq, iw, kr, ikr, va, g):
    tq = min(S, TQ)
    nq = S // tq
    qrow = lambda b, i: (b * nq + i, 0)
    seq = lambda b, i: (b, 0)
    fix = lambda b, i: (0, 0)
    return pl.pallas_call(
        functools.partial(_attn_kernel, S),
        out_shape=jax.ShapeDtypeStruct((B * S, ATT_WIDTH), jnp.bfloat16),
        grid=(B, nq),
        in_specs=[pl.BlockSpec((tq, ATT_WIDTH), qrow),
                  pl.BlockSpec((tq, IDX_HEADS * IDX_DIM), qrow),
                  pl.BlockSpec((tq, LANES), qrow),
                  pl.BlockSpec((S, HEAD_DIM), seq), pl.BlockSpec((S, HEAD_DIM), seq),
                  pl.BlockSpec((S, HEAD_DIM), seq),
                  pl.BlockSpec((1, ATT_WIDTH), fix)],
        out_specs=pl.BlockSpec((tq, ATT_WIDTH), qrow),
        scratch_shapes=[pltpu.VMEM((tq, S), jnp.int32), pltpu.VMEM((tq, S), jnp.float32)],
        compiler_params=_params("parallel", "parallel"),
        name="dsa_attention",
    )(q, iq, iw, kr, ikr, va, g)


def _split3(x):
    hi = _bf(x)
    r1 = x - hi.astype(jnp.float32)
    mid = _bf(r1)
    lo = _bf(r1 - mid.astype(jnp.float32))
    return hi, mid, lo


def _rwkv_kernel(zb_ref, mu_ref, w0_ref, w2_ref, a0_ref, a2_ref, g2_ref, kk_ref, ka_ref, rk_ref,
                 lg_ref, lb_ref, o_ref, prev_ref, state_ref):
    G, C, _ = zb_ref.shape
    W = RWKV_WIDTH
    R = G * C
    H = RWKV_HEADS

    @pl.when(pl.program_id(1) == 0)
    def _():
        prev_ref[...] = jnp.zeros_like(prev_ref)
        state_ref[...] = jnp.zeros_like(state_ref)

    x = zb_ref[...].reshape(R, B_COLS)
    rowi = lax.broadcasted_iota(jnp.int32, (R, 1), 0)
    shifted = pltpu.roll(x, 1, 0)
    for gi in range(G):
        shifted = jnp.where(rowi == gi * C, prev_ref[8 * gi:8 * gi + 1, :], shifted)
        prev_ref[8 * gi:8 * gi + 1, :] = x[gi * C + C - 1:gi * C + C, :]
    xm = x + (shifted - x) * mu_ref[...]
    r = xm[:, 0:W]
    k = xm[:, W:2 * W]
    v = xm[:, 2 * W:3 * W]
    wl = xm[:, 3 * W:3 * W + DECAY_RANK]
    al = xm[:, 3 * W + DECAY_RANK:3 * W + DECAY_RANK + ICLR_RANK]
    gl = xm[:, 3 * W + DECAY_RANK + ICLR_RANK:]

    wpre = w0_ref[...] + _dot(_bf(jnp.tanh(wl)), w2_ref[...])
    nw = -wpre
    w = -(jnp.maximum(nw, 0.0) + jnp.log(1.0 + jnp.exp(-jnp.abs(nw)))) - 0.5
    logd = -jnp.exp(w)
    a = jax.nn.sigmoid(a0_ref[...] + _dot(_bf(al), a2_ref[...]))
    g = _dot(_bf(jax.nn.sigmoid(gl)), g2_ref[...])

    rr = lax.broadcasted_iota(jnp.int32, (R, R), 0)
    cc = lax.broadcasted_iota(jnp.int32, (R, R), 1)
    tri = _bf(jnp.where((rr >= cc) & (rr // C == cc // C), 1.0, 0.0))
    hi, mid, lo = _split3(logd)
    cw = _dot(tri, hi) + _dot(tri, mid) + _dot(tri, lo)
    e_in = jnp.exp(cw)
    e_ex = jnp.exp(cw - logd)
    e_inv = jnp.exp(-cw)

    kk = k * kk_ref[...]
    k2 = k * (1.0 + (a - 1.0) * ka_ref[...])
    rk2 = r * k2 * rk_ref[...]
    ri = lax.broadcasted_iota(jnp.int32, (C, C), 0)
    ci = lax.broadcasted_iota(jnp.int32, (C, C), 1)
    strict = ri > ci
    ri2 = lax.broadcasted_iota(jnp.int32, (C, 2 * C), 0)
    ci2 = lax.broadcasted_iota(jnp.int32, (C, 2 * C), 1)
    incl2 = ri2 >= jnp.where(ci2 >= C, ci2 - C, ci2)
    eye = jnp.where(ri == ci, 1.0, 0.0)

    units = [(gi, h) for gi in range(G) for h in range(H)]

    def cut(arr, u):
        gi, h = u
        return arr[gi * C:(gi + 1) * C, h * HEAD_DIM:(h + 1) * HEAD_DIM]

    def unit_norm(u):
        kkh = cut(kk, u)
        return kkh / jnp.maximum(jnp.sqrt(jnp.sum(kkh * kkh, axis=-1, keepdims=True)), 1e-12)

    kkn = [unit_norm(u) for u in units]
    v_u = [cut(v, u) for u in units]
    at = [_bf(-kn * cut(e_ex, u)) for kn, u in zip(kkn, units)]
    rt = [_bf(cut(r, u) * cut(e_in, u)) for u in units]
    bt = [kn * cut(a, u) * cut(e_inv, u) for kn, u in zip(kkn, units)]
    kt = [cut(k2, u) * cut(e_inv, u) for u in units]
    bk16 = [_bf(jnp.concatenate([b_, k_], axis=0)) for b_, k_ in zip(bt, kt)]
    v16 = [_bf(v_) for v_ in v_u]
    l_ab = [jnp.where(strict, _dot_nt(a_, _bf(b_)), 0.0) for a_, b_ in zip(at, bt)]
    a_ak = [_bf(jnp.where(strict, _dot_nt(a_, _bf(k_)), 0.0)) for a_, k_ in zip(at, kt)]
    tinv = [eye + l_ for l_ in l_ab]
    pw = l_ab
    n = 2
    while n < C:
        pw = [_dot(p_, p_) for p_ in pw]
        tinv = [t_ + _dot(t_, p_) for t_, p_ in zip(tinv, pw)]
        n *= 2
    s0 = [state_ref[i] for i in range(len(units))]
    s016 = [_bf(s_) for s_ in s0]
    rhs = [_dot_nt(a_, s_) + _dot(m_, v_) for a_, s_, m_, v_ in zip(at, s016, a_ak, v16)]
    uu = [_dot(t_, x_) for t_, x_ in zip(tinv, rhs)]
    uv16 = [_bf(jnp.concatenate([u_, v_], axis=0)) for u_, v_ in zip(uu, v_u)]
    g_r = [_bf(jnp.where(incl2, _dot_nt(r_, m_), 0.0)) for r_, m_ in zip(rt, bk16)]
    y = [_dot(m_, x_) + _dot_nt(r_, s_) for m_, x_, r_, s_ in zip(g_r, uv16, rt, s016)]
    for i, u in enumerate(units):
        state_ref[i] = (s0[i] + _dot_tn(uv16[i], bk16[i])) * cut(e_in, u)[C - 1:C, :]
    for gi in range(G):
        outs = []
        for h in range(H):
            i = gi * H + h
            u = units[i]
            sl = slice(h * HEAD_DIM, (h + 1) * HEAD_DIM)
            ym = jnp.mean(y[i], axis=-1, keepdims=True)
            yc = y[i] - ym
            yv = jnp.mean(yc * yc, axis=-1, keepdims=True)
            yn = yc * lax.rsqrt(yv + GN_EPS) * lg_ref[:, sl] + lb_ref[:, sl]
            bonus = jnp.sum(cut(rk2, u), axis=-1, keepdims=True) * v_u[i]
            outs.append((yn + bonus) * cut(g, u))
        o_ref[gi] = _bf(jnp.concatenate(outs, axis=1))


def _rwkv(B, S, zb, mu, w0, w2, a0, a2, g2, k_k, k_a, r_k, lnx_g, lnx_b):
    C = min(S, CHUNK)
    G = min(B, RWKV_ROWS)
    blk = lambda b, c: (b, c, 0)
    fix = lambda b, c: (0, 0)
    vec = lambda n: pl.BlockSpec((1, n), fix)
    return pl.pallas_call(
        _rwkv_kernel,
        out_shape=jax.ShapeDtypeStruct((B, S, RWKV_WIDTH), jnp.bfloat16),
        grid=(B // G, S // C),
        in_specs=[pl.BlockSpec((G, C, B_COLS), blk), vec(B_COLS), vec(RWKV_WIDTH),
                  pl.BlockSpec((DECAY_RANK, RWKV_WIDTH), fix), vec(RWKV_WIDTH),
                  pl.BlockSpec((ICLR_RANK, RWKV_WIDTH), fix),
                  pl.BlockSpec((GATE_RANK, RWKV_WIDTH), fix),
                  vec(RWKV_WIDTH), vec(RWKV_WIDTH), vec(RWKV_WIDTH), vec(RWKV_WIDTH),
                  vec(RWKV_WIDTH)],
        out_specs=pl.BlockSpec((G, C, RWKV_WIDTH), blk),
        scratch_shapes=[pltpu.VMEM((8 * G, B_COLS), jnp.float32),
                        pltpu.VMEM((G * RWKV_HEADS, HEAD_DIM, HEAD_DIM), jnp.float32)],
        compiler_params=_params("parallel", "arbitrary"),
        name="rwkv7",
    )(zb.reshape(B, S, B_COLS), mu, w0, w2, a0, a2, g2, k_k, k_a, r_k, lnx_g, lnx_b
      ).reshape(B * S, RWKV_WIDTH)


def _pool_kernel(zc_ref, w_ref, sc_ref, o_ref):
    S = zc_ref.shape[0]
    x = zc_ref[...]
    row = lax.broadcasted_iota(jnp.int32, (S, 1), 0)
    lane_grp = lax.broadcasted_iota(jnp.int32, (1, POOL_WIDTH), 1) // POOL_GROUP

    def lag(y, n):
        return jnp.where(row >= n, pltpu.roll(y, n, 0), 0.0)

    w2 = x + lag(x, 1)
    w4 = w2 + lag(w2, 2)
    w8 = w4 + lag(w4, 4)
    w16 = w8 + lag(w8, 8)
    wsum = jnp.where(lane_grp == 0, w2, jnp.where(lane_grp == 1, w4, jnp.where(lane_grp == 2, w8, w16)))
    win = jnp.where(lane_grp == 0, 2, jnp.where(lane_grp == 1, 4, jnp.where(lane_grp == 2, 8, 16)))
    count = jnp.minimum(row + 1, win).astype(jnp.float32)
    pooled = wsum / count - x
    o_ref[...] = _bf(_dot(_bf(pooled), w_ref[...]) * sc_ref[...])


def _pool(B, S, zc, w_bd, scale):
    return pl.pallas_call(
        _pool_kernel,
        out_shape=jax.ShapeDtypeStruct((B * S, POOL_WIDTH), jnp.bfloat16),
        grid=(B,),
        in_specs=[pl.BlockSpec((S, POOL_WIDTH), lambda b: (b, 0)),
                  pl.BlockSpec((POOL_WIDTH, POOL_WIDTH), lambda b: (0, 0)),
                  pl.BlockSpec((1, POOL_WIDTH), lambda b: (0, 0))],
        out_specs=pl.BlockSpec((S, POOL_WIDTH), lambda b: (b, 0)),
        compiler_params=_params("parallel"),
        name="pool",
    )(zc, w_bd, scale)


def _outproj_kernel(alpha, oa_ref, ob_ref, oc_ref, x_ref, wa_ref, wb_ref, wc_ref, g_ref, b_ref, o_ref):
    mix = _dot(oa_ref[...], wa_ref[...]) + _dot(ob_ref[...], wb_ref[...]) + _dot(oc_ref[...], wc_ref[...])
    o_ref[...] = _ln(alpha * x_ref[...] + mix, g_ref[...], b_ref[...])


def _outproj(alpha, oa, ob, oc, x, wa, wb, wc, g, b):
    T = x.shape[0]
    tm = min(T, TM_PROJ)
    row = lambda i: (i, 0)
    fix = lambda i: (0, 0)
    return pl.pallas_call(
        functools.partial(_outproj_kernel, alpha),
        out_shape=jax.ShapeDtypeStruct((T, D_MODEL), jnp.float32),
        grid=(T // tm,),
        in_specs=[pl.BlockSpec((tm, ATT_WIDTH), row), pl.BlockSpec((tm, RWKV_WIDTH), row),
                  pl.BlockSpec((tm, POOL_WIDTH), row), pl.BlockSpec((tm, D_MODEL), row),
                  pl.BlockSpec((ATT_WIDTH, D_MODEL), fix), pl.BlockSpec((RWKV_WIDTH, D_MODEL), fix),
                  pl.BlockSpec((POOL_WIDTH, D_MODEL), fix),
                  pl.BlockSpec((1, D_MODEL), fix), pl.BlockSpec((1, D_MODEL), fix)],
        out_specs=pl.BlockSpec((tm, D_MODEL), row),
        compiler_params=_params("parallel"),
        name="outproj_ln1",
    )(oa, ob, oc, x, wa, wb, wc, g, b)


def _mlp_kernel(alpha, x_ref, p_ref, w1_ref, w2_ref, pg_ref, pgb_ref, pp_ref, g_ref, b_ref, o_ref,
                xb_ref, acc_ref):
    kf = pl.program_id(1)

    @pl.when(kf == 0)
    def _():
        x = x_ref[...]
        xb = _bf(x)
        xb_ref[...] = xb
        gate = jax.nn.sigmoid(_dot(xb, pg_ref[...]) + pgb_ref[...])
        acc_ref[...] = alpha * x + gate * _dot(_bf(p_ref[...]), pp_ref[...])

    h = jnp.maximum(_dot(xb_ref[...], w1_ref[...]), 0.0)
    acc_ref[...] += _dot(_bf(h * h), w2_ref[...])

    @pl.when(kf == pl.num_programs(1) - 1)
    def _():
        o_ref[...] = _ln(acc_ref[...], g_ref[...], b_ref[...])


def _mlp(alpha, x, p, w1, w2, pg, pgb, pp, g, b):
    T = x.shape[0]
    tm = min(T, TM_MLP)
    tf = TF_MLP
    row = lambda i, k: (i, 0)
    fix = lambda i, k: (0, 0)
    return pl.pallas_call(
        functools.partial(_mlp_kernel, alpha),
        out_shape=jax.ShapeDtypeStruct((T, D_MODEL), jnp.float32),
        grid=(T // tm, D_FF // tf),
        in_specs=[pl.BlockSpec((tm, D_MODEL), row), pl.BlockSpec((tm, PE_DIM), row),
                  pl.BlockSpec((D_MODEL, tf), lambda i, k: (0, k)),
                  pl.BlockSpec((tf, D_MODEL), lambda i, k: (k, 0)),
                  pl.BlockSpec((D_MODEL, D_MODEL), fix), pl.BlockSpec((1, D_MODEL), fix),
                  pl.BlockSpec((PE_DIM, D_MODEL), fix),
                  pl.BlockSpec((1, D_MODEL), fix), pl.BlockSpec((1, D_MODEL), fix)],
        out_specs=pl.BlockSpec((tm, D_MODEL), row),
        scratch_shapes=[pltpu.VMEM((tm, D_MODEL), jnp.bfloat16),
                        pltpu.VMEM((tm, D_MODEL), jnp.float32)],
        compiler_params=_params("parallel", "arbitrary"),
        name="mlp_ple_ln2",
    )(x, p, w1, w2, pg, pgb, pp, g, b)


def _rot_partner(w):
    half = HEAD_DIM // 2
    return jnp.concatenate([-w[..., half:], w[..., :half]], axis=-1)


def _per_head_partner(w, heads):
    r = w.reshape(w.shape[0], heads, HEAD_DIM)
    return _rot_partner(r).reshape(w.shape)


def _arrange_w_in(w):
    cq = w[:, 0:256]
    ka = w[:, 256:320]
    va = w[:, 320:384]
    ik = w[:, 384:448]
    iw = w[:, 448:456]
    rest = w[:, 456:]
    pad = jnp.zeros((w.shape[0], LANES - HEAD_DIM - IDX_HEADS), w.dtype)
    return jnp.concatenate([cq, ka, ik, _rot_partner(ka), _rot_partner(ik), va, iw, pad, rest], axis=1)


def kernel(x, p, positions, ln_emb_g, ln_emb_b, w_in, w_uq, w_uqi, attn_norm_g, rwkv_mu, rwkv_w0, rwkv_w2, rwkv_a0, rwkv_a2, rwkv_g2, rwkv_k_k, rwkv_k_a, rwkv_r_k, rwkv_lnx_g, rwkv_lnx_b, pool_w, pool_scale, w_out, ln1_g, ln1_b, mlp_w1, mlp_w2, pe_proj, pe_gate, pe_gate_b, ln2_g, ln2_b):
    B, S, _ = x.shape
    T = B * S
    depth = w_in.shape[0]
    alpha = (2 * depth) ** 0.25
    bf = lambda a: a.astype(jnp.bfloat16)
    rowv = lambda a: a.reshape(1, -1)

    cos, sin = _rope_tables(positions.astype(jnp.float32).reshape(T, 1))
    xs = x.reshape(T, D_MODEL)
    for i in range(depth):
        win = bf(_arrange_w_in(w_in[i]))
        wq = w_uq[i] * (HEAD_DIM ** -0.5)
        wi = w_uqi[i] * (IDX_DIM ** -0.5)
        res = _inproj(xs, rowv(ln_emb_g), rowv(ln_emb_b), cos, sin, win,
                      bf(wq), bf(_per_head_partner(wq, ATT_HEADS)),
                      bf(wi), bf(_per_head_partner(wi, IDX_HEADS)), apply_ln=(i == 0))
        if i == 0:
            xs, res = res[0], res[1:]
        q, iq, iw, kr, ikr, va, zb, zc = res
        o_a = _attention(B, S, q, iq, iw, kr, ikr, va, rowv(attn_norm_g[i]))
        o_b = _rwkv(B, S, zb, rowv(rwkv_mu[i]), rowv(rwkv_w0[i]), bf(rwkv_w2[i]), rowv(rwkv_a0[i]),
                    bf(rwkv_a2[i]), bf(rwkv_g2[i]), rowv(rwkv_k_k[i]), rowv(rwkv_k_a[i]),
                    rowv(rwkv_r_k[i]), rowv(rwkv_lnx_g[i]), rowv(rwkv_lnx_b[i]))
        w_bd = jax.scipy.linalg.block_diag(*[pool_w[i, gi] for gi in range(len(POOL_WINDOWS))])
        o_c = _pool(B, S, zc, bf(w_bd), rowv(pool_scale[i]))
        wo = bf(w_out[i])
        xs = _outproj(alpha, o_a, o_b, o_c, xs, wo[:ATT_WIDTH], wo[ATT_WIDTH:ATT_WIDTH + RWKV_WIDTH],
                      wo[ATT_WIDTH + RWKV_WIDTH:], rowv(ln1_g[i]), rowv(ln1_b[i]))
        xs = _mlp(alpha, xs, p[i].reshape(T, PE_DIM), bf(mlp_w1[i]), bf(mlp_w2[i]), bf(pe_gate[i]),
                  rowv(pe_gate_b[i]), bf(pe_proj[i]), rowv(ln2_g[i]), rowv(ln2_b[i]))
    return xs.reshape(B, S, D_MODEL)
```

```python
import functools

import numpy as np
import jax
import jax.numpy as jnp
from jax import lax
from jax.experimental import pallas as pl
from jax.experimental.pallas import tpu as pltpu

D_MODEL = 1024
PE_DIM = 256
HEAD_DIM = 64
ROPE_THETA = 10000.0
ATT_WIDTH = 384
ATT_HEADS = 6
Q_RANK = 256
IDX_HEADS = 8
IDX_DIM = 64
TOPK_MAX = 256
RWKV_WIDTH = 384
RWKV_HEADS = 6
DECAY_RANK = 64
ICLR_RANK = 64
GATE_RANK = 128
GN_EPS = 64e-5
POOL_WINDOWS = (2, 4, 8, 16)
POOL_WIDTH = 256
POOL_GROUP = 64
B_COLS = 3 * RWKV_WIDTH + DECAY_RANK + ICLR_RANK + GATE_RANK
D_FF = 4 * D_MODEL
LN_EPS = 1e-5
RMS_EPS = 1e-6

COL_CQ = 0
COL_KIK = 256
COL_KIK_P = 384
COL_VIW = 512
COL_B = 640
COL_C = COL_B + B_COLS
N_IN_P = COL_C + POOL_WIDTH

LANES = 128
INT_MIN = -2 ** 31
VMEM_LIMIT = 56 * 1024 * 1024

TM_PROJ = 512
TQ = 256
CHUNK = 64
RWKV_ROWS = 4
TM_MLP = 512
TF_MLP = 1024

_NT = (((1,), (1,)), ((), ()))
_TN = (((0,), (0,)), ((), ()))


def _params(*sem):
    return pltpu.CompilerParams(dimension_semantics=sem, vmem_limit_bytes=VMEM_LIMIT)


def _ln(x, g, b):
    mu = jnp.mean(x, axis=-1, keepdims=True)
    xc = x - mu
    var = jnp.mean(xc * xc, axis=-1, keepdims=True)
    return xc * lax.rsqrt(var + LN_EPS) * g + b


def _bf(x):
    return x.astype(jnp.bfloat16)


def _dot(a, b):
    return jnp.dot(a, b, preferred_element_type=jnp.float32)


def _dot_nt(a, b):
    return lax.dot_general(a, b, _NT, preferred_element_type=jnp.float32)


def _dot_tn(a, b):
    return lax.dot_general(a, b, _TN, preferred_element_type=jnp.float32)


def _rope_kernel(pos_ref, inv_ref, cos_ref, sin_ref):
    ang = pos_ref[...] * inv_ref[...]
    cos_ref[...] = jnp.cos(ang)
    sin_ref[...] = jnp.sin(ang)


def _rope_tables(pos_f32):
    T = pos_f32.shape[0]
    tm = min(T, 2048)
    inv = ROPE_THETA ** (-np.arange(0, HEAD_DIM, 2, dtype=np.float32) / HEAD_DIM)
    inv128 = jnp.asarray(np.tile(inv, LANES // (HEAD_DIM // 2))[None, :], jnp.float32)
    return pl.pallas_call(
        _rope_kernel,
        out_shape=(jax.ShapeDtypeStruct((T, LANES), jnp.float32),) * 2,
        grid=(T // tm,),
        in_specs=[pl.BlockSpec((tm, 1), lambda i: (i, 0)),
                  pl.BlockSpec((1, LANES), lambda i: (0, 0))],
        out_specs=(pl.BlockSpec((tm, LANES), lambda i: (i, 0)),) * 2,
        compiler_params=_params("parallel"),
        name="rope_tables",
    )(pos_f32, inv128)


def _inproj_kernel(apply_ln, x_ref, g_ref, b_ref, cos_ref, sin_ref, win_ref, wq_ref, wqp_ref,
                   wi_ref, wip_ref, *out_refs):
    if apply_ln:
        xn_ref, q_ref, iq_ref, iw_ref, kr_ref, ikr_ref, va_ref, zb_ref, zc_ref = out_refs
        x = _ln(x_ref[...], g_ref[...], b_ref[...])
        xn_ref[...] = x
    else:
        q_ref, iq_ref, iw_ref, kr_ref, ikr_ref, va_ref, zb_ref, zc_ref = out_refs
        x = x_ref[...]
    z = _dot(_bf(x), win_ref[...])
    cos = cos_ref[...]
    sin = sin_ref[...]
    cq = _bf(z[:, COL_CQ:COL_CQ + Q_RANK])
    cos3 = jnp.concatenate([cos] * 3, axis=1)
    sin3 = jnp.concatenate([sin] * 3, axis=1)
    q = _bf(_dot(cq, wq_ref[...]) * cos3 + _dot(cq, wqp_ref[...]) * sin3)
    for j in range(q_ref.shape[0]):
        q_ref[j] = q[:, j * LANES:(j + 1) * LANES]
    cos4 = jnp.concatenate([cos] * 4, axis=1)
    sin4 = jnp.concatenate([sin] * 4, axis=1)
    iq = _bf(_dot(cq, wi_ref[...]) * cos4 + _dot(cq, wip_ref[...]) * sin4)
    for j in range(iq_ref.shape[0]):
        iq_ref[j] = iq[:, j * LANES:(j + 1) * LANES]
    kik = z[:, COL_KIK:COL_KIK + LANES] * cos + z[:, COL_KIK_P:COL_KIK_P + LANES] * sin
    kr_ref[...] = _bf(kik[:, :HEAD_DIM])
    ikr_ref[...] = _bf(kik[:, HEAD_DIM:])
    viw = z[:, COL_VIW:COL_VIW + LANES]
    va_ref[...] = _bf(viw[:, :HEAD_DIM])
    iw_ref[...] = viw * (IDX_HEADS ** -0.5)
    zb_ref[...] = z[:, COL_B:COL_B + B_COLS]
    zc_ref[...] = z[:, COL_C:COL_C + POOL_WIDTH]


def _inproj(x, g, b, cos, sin, win, wq, wqp, wi, wip, apply_ln):
    T = x.shape[0]
    tm = min(T, TM_PROJ)
    row = lambda i: (i, 0)
    fix = lambda i: (0, 0)
    bf16, f32 = jnp.bfloat16, jnp.float32
    outs = [(LANES, f32), (HEAD_DIM, bf16), (HEAD_DIM, bf16), (HEAD_DIM, bf16), (B_COLS, f32),
            (POOL_WIDTH, f32)]
    pairs = [ATT_WIDTH // LANES, IDX_HEADS * IDX_DIM // LANES]
    shapes = [jax.ShapeDtypeStruct((n, T, LANES), bf16) for n in pairs]
    shapes += [jax.ShapeDtypeStruct((T, n), dt) for n, dt in outs]
    specs = [pl.BlockSpec((n, tm, LANES), lambda i: (0, i, 0)) for n in pairs]
    specs += [pl.BlockSpec((tm, n), row) for n, _ in outs]
    if apply_ln:
        shapes = [jax.ShapeDtypeStruct((T, D_MODEL), f32)] + shapes
        specs = [pl.BlockSpec((tm, D_MODEL), row)] + specs
    return pl.pallas_call(
        functools.partial(_inproj_kernel, apply_ln),
        out_shape=tuple(shapes),
        grid=(T // tm,),
        in_specs=[pl.BlockSpec((tm, D_MODEL), row),
                  pl.BlockSpec((1, D_MODEL), fix), pl.BlockSpec((1, D_MODEL), fix),
                  pl.BlockSpec((tm, LANES), row), pl.BlockSpec((tm, LANES), row),
                  pl.BlockSpec((D_MODEL, N_IN_P), fix),
                  pl.BlockSpec((Q_RANK, ATT_WIDTH), fix), pl.BlockSpec((Q_RANK, ATT_WIDTH), fix),
                  pl.BlockSpec((Q_RANK, IDX_HEADS * IDX_DIM), fix),
                  pl.BlockSpec((Q_RANK, IDX_HEADS * IDX_DIM), fix)],
        out_specs=tuple(specs),
        compiler_params=_params("parallel"),
        name="inproj",
    )(x, g, b, cos, sin, win, wq, wqp, wi, wip)


def _attn_block(kend, search, topk, q_ref, iq_ref, iw_ref, kr_ref, ikr_ref, va_ref, g_ref, o_ref,
                key_ref, bias_ref, oacc_ref):
    tq = q_ref.shape[1]
    row_t = pl.program_id(1) * tq + lax.broadcasted_iota(jnp.int32, (tq, 1), 0)
    col = lax.broadcasted_iota(jnp.int32, (1, kend), 1)
    causal = col <= row_t

    if not search:
        bias_ref[:, :kend] = jnp.where(causal, 0.0, -jnp.inf)
    else:
        lane = lax.broadcasted_iota(jnp.int32, (1, LANES), 1)
        bias_ref[:, :kend] = jnp.zeros((tq, kend), jnp.float32)

        def idx_pair(j, carry):
            iqp = iq_ref[j]
            ik = ikr_ref[0:kend, :]
            acc = bias_ref[:, :kend]
            for hh in range(2):
                lg = _dot_nt(iqp[:, hh * IDX_DIM:(hh + 1) * IDX_DIM], ik)
                w_h = jnp.sum(jnp.where(lane == HEAD_DIM + 2 * j + hh, iw_ref[...], 0.0), axis=1, keepdims=True)
                acc = acc + jnp.maximum(lg, 0.0) * w_h
            bias_ref[:, :kend] = acc
            return carry

        lax.fori_loop(0, iq_ref.shape[0], idx_pair, 0)
        score = bias_ref[:, :kend] + 0.0
        bits = pltpu.bitcast(score, jnp.int32)
        key = bits ^ ((bits >> 31) & 0x7FFFFFFF)
        key_ref[:, :kend] = jnp.where(causal, key, INT_MIN)

        def count_ge(cand):
            return jnp.sum(jnp.where(key_ref[:, :kend] >= cand, 1.0, 0.0), axis=1, keepdims=True)

        tau = jnp.where(count_ge(jnp.zeros((tq, 1), jnp.int32)) >= topk, 0, INT_MIN).astype(jnp.int32)

        def bit_step(i, tau):
            cand = tau | (jnp.int32(1) << (30 - i))
            return jnp.where(count_ge(cand) >= topk, cand, tau)

        tau = lax.fori_loop(0, 31, bit_step, tau)
        tau = jnp.maximum(tau, INT_MIN + 1)
        ge = key_ref[:, :kend] >= tau
        n_ge = jnp.sum(jnp.where(ge, 1.0, 0.0), axis=1, keepdims=True)
        bias_ref[:, :kend] = jnp.where(ge, 0.0, -jnp.inf)

        @pl.when(jnp.max(n_ge) > topk)
        def _():
            keyv = key_ref[:, :kend]
            gt = keyv > tau
            eq = keyv == tau
            need = topk - jnp.sum(jnp.where(gt, 1.0, 0.0), axis=1, keepdims=True)
            nbits = int(kend).bit_length()

            def idx_step(i, m):
                cand = m | (jnp.int32(1) << (nbits - 1 - i))
                cnt = jnp.sum(jnp.where(eq & (col < cand), 1.0, 0.0), axis=1, keepdims=True)
                return jnp.where(cnt <= need, cand, m)

            m = lax.fori_loop(0, nbits, idx_step, jnp.zeros((tq, 1), jnp.int32))
            bias_ref[:, :kend] = jnp.where(gt | (eq & (col < m)), 0.0, -jnp.inf)

    def att_pair(j, carry):
        qp = q_ref[j]
        kr = kr_ref[0:kend, :]
        va = va_ref[0:kend, :]
        bias = bias_ref[:, :kend]
        outs = []
        for hh in range(2):
            s = _dot_nt(qp[:, hh * HEAD_DIM:(hh + 1) * HEAD_DIM], kr) + bias
            p = jnp.exp(s - jnp.max(s, axis=1, keepdims=True))
            l = jnp.sum(p, axis=1, keepdims=True)
            outs.append(_dot(_bf(p), va) / l)
        oacc_ref[j] = jnp.concatenate(outs, axis=1)
        return carry

    lax.fori_loop(0, q_ref.shape[0], att_pair, 0)
    o = jnp.concatenate([oacc_ref[j] for j in range(q_ref.shape[0])], axis=1)
    o = o * lax.rsqrt(jnp.mean(o * o, axis=-1, keepdims=True) + RMS_EPS) * g_ref[...]
    o_ref[...] = _bf(o)


def _key_extents(S, tq, topk):
    out = []
    for v in range(S // tq):
        need = (v + 1) * tq
        if need <= topk:
            out.append((need, False))
        else:
            out.append((min(S, -(-need // (2 * tq)) * 2 * tq), True))
    return out


def _attn_kernel(S, *refs):
    tq = refs[0].shape[1]
    topk = min(TOPK_MAX, S // 4)
    qi = pl.program_id(1)
    extents = _key_extents(S, tq, topk)
    for ext in sorted(set(extents)):
        blocks = [v for v, e in enumerate(extents) if e == ext]
        pl.when((qi >= blocks[0]) & (qi <= blocks[-1]))(functools.partial(_attn_block, *ext, topk, *refs))


def _attention(B, S, q, iq, iw, kr, ikr, va, g):
    tq = min(S, TQ)
    nq = S // tq
    qrow = lambda b, i: (b * nq + i, 0)
    seq = lambda b, i: (b, 0)
    fix = lambda b, i: (0, 0)
    return pl.pallas_call(
        functools.partial(_attn_kernel, S),
        out_shape=jax.ShapeDtypeStruct((B * S, ATT_WIDTH), jnp.bfloat16),
        grid=(B, nq),
        in_specs=[pl.BlockSpec((q.shape[0], tq, LANES), lambda b, i: (0, b * nq + i, 0)),
                  pl.BlockSpec((iq.shape[0], tq, LANES), lambda b, i: (0, b * nq + i, 0)),
                  pl.BlockSpec((tq, LANES), qrow),
                  pl.BlockSpec((S, HEAD_DIM), seq), pl.BlockSpec((S, HEAD_DIM), seq),
                  pl.BlockSpec((S, HEAD_DIM), seq),
                  pl.BlockSpec((1, ATT_WIDTH), fix)],
        out_specs=pl.BlockSpec((tq, ATT_WIDTH), qrow),
        scratch_shapes=[pltpu.VMEM((tq, S), jnp.int32), pltpu.VMEM((tq, S), jnp.float32),
                        pltpu.VMEM((q.shape[0], tq, LANES), jnp.float32)],
        compiler_params=_params("parallel", "parallel"),
        name="dsa_attention",
    )(q, iq, iw, kr, ikr, va, g)


def _split3(x):
    hi = _bf(x)
    r1 = x - hi.astype(jnp.float32)
    mid = _bf(r1)
    lo = _bf(r1 - mid.astype(jnp.float32))
    return hi, mid, lo


def _rwkv_kernel(zb_ref, mu_ref, w0_ref, w2_ref, a0_ref, a2_ref, g2_ref, kk_ref, ka_ref, rk_ref,
                 lg_ref, lb_ref, o_ref, prev_ref, state_ref):
    G, C, _ = zb_ref.shape
    W = RWKV_WIDTH
    R = G * C
    H = RWKV_HEADS

    @pl.when(pl.program_id(1) == 0)
    def _():
        prev_ref[...] = jnp.zeros_like(prev_ref)
        state_ref[...] = jnp.zeros_like(state_ref)

    x = zb_ref[...].reshape(R, B_COLS)
    rowi = lax.broadcasted_iota(jnp.int32, (R, 1), 0)
    shifted = pltpu.roll(x, 1, 0)
    for gi in range(G):
        shifted = jnp.where(rowi == gi * C, prev_ref[8 * gi:8 * gi + 1, :], shifted)
        prev_ref[8 * gi:8 * gi + 1, :] = x[gi * C + C - 1:gi * C + C, :]
    xm = x + (shifted - x) * mu_ref[...]
    r = xm[:, 0:W]
    k = xm[:, W:2 * W]
    v = xm[:, 2 * W:3 * W]
    wl = xm[:, 3 * W:3 * W + DECAY_RANK]
    al = xm[:, 3 * W + DECAY_RANK:3 * W + DECAY_RANK + ICLR_RANK]
    gl = xm[:, 3 * W + DECAY_RANK + ICLR_RANK:]

    wpre = w0_ref[...] + _dot(_bf(jnp.tanh(wl)), w2_ref[...])
    nw = -wpre
    w = -(jnp.maximum(nw, 0.0) + jnp.log(1.0 + jnp.exp(-jnp.abs(nw)))) - 0.5
    logd = -jnp.exp(w)
    a = jax.nn.sigmoid(a0_ref[...] + _dot(_bf(al), a2_ref[...]))
    g = _dot(_bf(jax.nn.sigmoid(gl)), g2_ref[...])

    rr = lax.broadcasted_iota(jnp.int32, (R, R), 0)
    cc = lax.broadcasted_iota(jnp.int32, (R, R), 1)
    tri = _bf(jnp.where((rr >= cc) & (rr // C == cc // C), 1.0, 0.0))
    hi, mid, lo = _split3(logd)
    cw = _dot(tri, hi) + _dot(tri, mid) + _dot(tri, lo)
    e_in = jnp.exp(cw)
    e_ex = jnp.exp(cw - logd)
    e_inv = jnp.exp(-cw)

    kk = k * kk_ref[...]
    k2 = k * (1.0 + (a - 1.0) * ka_ref[...])
    rk2 = r * k2 * rk_ref[...]
    ri = lax.broadcasted_iota(jnp.int32, (C, C), 0)
    ci = lax.broadcasted_iota(jnp.int32, (C, C), 1)
    strict = ri > ci
    ri2 = lax.broadcasted_iota(jnp.int32, (C, 2 * C), 0)
    ci2 = lax.broadcasted_iota(jnp.int32, (C, 2 * C), 1)
    incl2 = ri2 >= jnp.where(ci2 >= C, ci2 - C, ci2)
    eye = jnp.where(ri == ci, 1.0, 0.0)

    units = [(gi, h) for gi in range(G) for h in range(H)]

    def cut(arr, u):
        gi, h = u
        return arr[gi * C:(gi + 1) * C, h * HEAD_DIM:(h + 1) * HEAD_DIM]

    def unit_norm(u):
        kkh = cut(kk, u)
        return kkh / jnp.maximum(jnp.sqrt(jnp.sum(kkh * kkh, axis=-1, keepdims=True)), 1e-12)

    kkn = [unit_norm(u) for u in units]
    v_u = [cut(v, u) for u in units]
    at = [_bf(-kn * cut(e_ex, u)) for kn, u in zip(kkn, units)]
    rt = [_bf(cut(r, u) * cut(e_in, u)) for u in units]
    bt = [kn * cut(a, u) * cut(e_inv, u) for kn, u in zip(kkn, units)]
    kt = [cut(k2, u) * cut(e_inv, u) for u in units]
    bk16 = [_bf(jnp.concatenate([b_, k_], axis=0)) for b_, k_ in zip(bt, kt)]
    v16 = [_bf(v_) for v_ in v_u]
    l_ab = [jnp.where(strict, _dot_nt(a_, _bf(b_)), 0.0) for a_, b_ in zip(at, bt)]
    a_ak = [_bf(jnp.where(strict, _dot_nt(a_, _bf(k_)), 0.0)) for a_, k_ in zip(at, kt)]
    tinv = [eye + l_ for l_ in l_ab]
    pw = l_ab
    n = 2
    while n < C:
        pw = [_dot(p_, p_) for p_ in pw]
        tinv = [t_ + _dot(t_, p_) for t_, p_ in zip(tinv, pw)]
        n *= 2
    s0 = [state_ref[i] for i in range(len(units))]
    s016 = [_bf(s_) for s_ in s0]
    rhs = [_dot_nt(a_, s_) + _dot(m_, v_) for a_, s_, m_, v_ in zip(at, s016, a_ak, v16)]
    uu = [_dot(t_, x_) for t_, x_ in zip(tinv, rhs)]
    uv16 = [_bf(jnp.concatenate([u_, v_], axis=0)) for u_, v_ in zip(uu, v_u)]
    g_r = [_bf(jnp.where(incl2, _dot_nt(r_, m_), 0.0)) for r_, m_ in zip(rt, bk16)]
    y = [_dot(m_, x_) + _dot_nt(r_, s_) for m_, x_, r_, s_ in zip(g_r, uv16, rt, s016)]
    for i, u in enumerate(units):
        state_ref[i] = (s0[i] + _dot_tn(uv16[i], bk16[i])) * cut(e_in, u)[C - 1:C, :]
    for gi in range(G):
        outs = []
        for h in range(H):
            i = gi * H + h
            u = units[i]
            sl = slice(h * HEAD_DIM, (h + 1) * HEAD_DIM)
            ym = jnp.mean(y[i], axis=-1, keepdims=True)
            yc = y[i] - ym
            yv = jnp.mean(yc * yc, axis=-1, keepdims=True)
            yn = yc * lax.rsqrt(yv + GN_EPS) * lg_ref[:, sl] + lb_ref[:, sl]
            bonus = jnp.sum(cut(rk2, u), axis=-1, keepdims=True) * v_u[i]
            outs.append((yn + bonus) * cut(g, u))
        o_ref[gi] = _bf(jnp.concatenate(outs, axis=1))


def _rwkv(B, S, zb, mu, w0, w2, a0, a2, g2, k_k, k_a, r_k, lnx_g, lnx_b):
    C = min(S, CHUNK)
    G = min(B, RWKV_ROWS)
    blk = lambda b, c: (b, c, 0)
    fix = lambda b, c: (0, 0)
    vec = lambda n: pl.BlockSpec((1, n), fix)
    return pl.pallas_call(
        _rwkv_kernel,
        out_shape=jax.ShapeDtypeStruct((B, S, RWKV_WIDTH), jnp.bfloat16),
        grid=(B // G, S // C),
        in_specs=[pl.BlockSpec((G, C, B_COLS), blk), vec(B_COLS), vec(RWKV_WIDTH),
                  pl.BlockSpec((DECAY_RANK, RWKV_WIDTH), fix), vec(RWKV_WIDTH),
                  pl.BlockSpec((ICLR_RANK, RWKV_WIDTH), fix),
                  pl.BlockSpec((GATE_RANK, RWKV_WIDTH), fix),
                  vec(RWKV_WIDTH), vec(RWKV_WIDTH), vec(RWKV_WIDTH), vec(RWKV_WIDTH),
                  vec(RWKV_WIDTH)],
        out_specs=pl.BlockSpec((G, C, RWKV_WIDTH), blk),
        scratch_shapes=[pltpu.VMEM((8 * G, B_COLS), jnp.float32),
                        pltpu.VMEM((G * RWKV_HEADS, HEAD_DIM, HEAD_DIM), jnp.float32)],
        compiler_params=_params("parallel", "arbitrary"),
        name="rwkv7",
    )(zb.reshape(B, S, B_COLS), mu, w0, w2, a0, a2, g2, k_k, k_a, r_k, lnx_g, lnx_b
      ).reshape(B * S, RWKV_WIDTH)


def _pool_kernel(zc_ref, w_ref, sc_ref, o_ref):
    S = zc_ref.shape[0]
    x = zc_ref[...]
    row = lax.broadcasted_iota(jnp.int32, (S, 1), 0)
    lane_grp = lax.broadcasted_iota(jnp.int32, (1, POOL_WIDTH), 1) // POOL_GROUP

    def lag(y, n):
        return jnp.where(row >= n, pltpu.roll(y, n, 0), 0.0)

    w2 = x + lag(x, 1)
    w4 = w2 + lag(w2, 2)
    w8 = w4 + lag(w4, 4)
    w16 = w8 + lag(w8, 8)
    wsum = jnp.where(lane_grp == 0, w2, jnp.where(lane_grp == 1, w4, jnp.where(lane_grp == 2, w8, w16)))
    win = jnp.where(lane_grp == 0, 2, jnp.where(lane_grp == 1, 4, jnp.where(lane_grp == 2, 8, 16)))
    count = jnp.minimum(row + 1, win).astype(jnp.float32)
    pooled = wsum / count - x
    o_ref[...] = _bf(_dot(_bf(pooled), w_ref[...]) * sc_ref[...])


def _pool(B, S, zc, w_bd, scale):
    return pl.pallas_call(
        _pool_kernel,
        out_shape=jax.ShapeDtypeStruct((B * S, POOL_WIDTH), jnp.bfloat16),
        grid=(B,),
        in_specs=[pl.BlockSpec((S, POOL_WIDTH), lambda b: (b, 0)),
                  pl.BlockSpec((POOL_WIDTH, POOL_WIDTH), lambda b: (0, 0)),
                  pl.BlockSpec((1, POOL_WIDTH), lambda b: (0, 0))],
        out_specs=pl.BlockSpec((S, POOL_WIDTH), lambda b: (b, 0)),
        compiler_params=_params("parallel"),
        name="pool",
    )(zc, w_bd, scale)


def _outproj_kernel(alpha, oa_ref, ob_ref, oc_ref, x_ref, wa_ref, wb_ref, wc_ref, g_ref, b_ref, o_ref):
    mix = _dot(oa_ref[...], wa_ref[...]) + _dot(ob_ref[...], wb_ref[...]) + _dot(oc_ref[...], wc_ref[...])
    o_ref[...] = _ln(alpha * x_ref[...] + mix, g_ref[...], b_ref[...])


def _outproj(alpha, oa, ob, oc, x, wa, wb, wc, g, b):
    T = x.shape[0]
    tm = min(T, TM_PROJ)
    row = lambda i: (i, 0)
    fix = lambda i: (0, 0)
    return pl.pallas_call(
        functools.partial(_outproj_kernel, alpha),
        out_shape=jax.ShapeDtypeStruct((T, D_MODEL), jnp.float32),
        grid=(T // tm,),
        in_specs=[pl.BlockSpec((tm, ATT_WIDTH), row), pl.BlockSpec((tm, RWKV_WIDTH), row),
                  pl.BlockSpec((tm, POOL_WIDTH), row), pl.BlockSpec((tm, D_MODEL), row),
                  pl.BlockSpec((ATT_WIDTH, D_MODEL), fix), pl.BlockSpec((RWKV_WIDTH, D_MODEL), fix),
                  pl.BlockSpec((POOL_WIDTH, D_MODEL), fix),
                  pl.BlockSpec((1, D_MODEL), fix), pl.BlockSpec((1, D_MODEL), fix)],
        out_specs=pl.BlockSpec((tm, D_MODEL), row),
        compiler_params=_params("parallel"),
        name="outproj_ln1",
    )(oa, ob, oc, x, wa, wb, wc, g, b)


def _mlp_kernel(alpha, x_ref, p_ref, w1_ref, w2_ref, pg_ref, pgb_ref, pp_ref, g_ref, b_ref, o_ref,
                xb_ref, acc_ref):
    kf = pl.program_id(1)

    @pl.when(kf == 0)
    def _():
        x = x_ref[...]
        xb = _bf(x)
        xb_ref[...] = xb
        gate = jax.nn.sigmoid(_dot(xb, pg_ref[...]) + pgb_ref[...])
        acc_ref[...] = alpha * x + gate * _dot(_bf(p_ref[...]), pp_ref[...])

    h = jnp.maximum(_dot(xb_ref[...], w1_ref[...]), 0.0)
    acc_ref[...] += _dot(_bf(h * h), w2_ref[...])

    @pl.when(kf == pl.num_programs(1) - 1)
    def _():
        o_ref[...] = _ln(acc_ref[...], g_ref[...], b_ref[...])


def _mlp(alpha, x, p, w1, w2, pg, pgb, pp, g, b):
    T = x.shape[0]
    tm = min(T, TM_MLP)
    tf = TF_MLP
    row = lambda i, k: (i, 0)
    fix = lambda i, k: (0, 0)
    return pl.pallas_call(
        functools.partial(_mlp_kernel, alpha),
        out_shape=jax.ShapeDtypeStruct((T, D_MODEL), jnp.float32),
        grid=(T // tm, D_FF // tf),
        in_specs=[pl.BlockSpec((tm, D_MODEL), row), pl.BlockSpec((tm, PE_DIM), row),
                  pl.BlockSpec((D_MODEL, tf), lambda i, k: (0, k)),
                  pl.BlockSpec((tf, D_MODEL), lambda i, k: (k, 0)),
                  pl.BlockSpec((D_MODEL, D_MODEL), fix), pl.BlockSpec((1, D_MODEL), fix),
                  pl.BlockSpec((PE_DIM, D_MODEL), fix),
                  pl.BlockSpec((1, D_MODEL), fix), pl.BlockSpec((1, D_MODEL), fix)],
        out_specs=pl.BlockSpec((tm, D_MODEL), row),
        scratch_shapes=[pltpu.VMEM((tm, D_MODEL), jnp.bfloat16),
                        pltpu.VMEM((tm, D_MODEL), jnp.float32)],
        compiler_params=_params("parallel", "arbitrary"),
        name="mlp_ple_ln2",
    )(x, p, w1, w2, pg, pgb, pp, g, b)


def _rot_partner(w):
    half = HEAD_DIM // 2
    return jnp.concatenate([-w[..., half:], w[..., :half]], axis=-1)


def _per_head_partner(w, heads):
    r = w.reshape(w.shape[0], heads, HEAD_DIM)
    return _rot_partner(r).reshape(w.shape)


def _arrange_w_in(w):
    cq = w[:, 0:256]
    ka = w[:, 256:320]
    va = w[:, 320:384]
    ik = w[:, 384:448]
    iw = w[:, 448:456]
    rest = w[:, 456:]
    pad = jnp.zeros((w.shape[0], LANES - HEAD_DIM - IDX_HEADS), w.dtype)
    return jnp.concatenate([cq, ka, ik, _rot_partner(ka), _rot_partner(ik), va, iw, pad, rest], axis=1)


def kernel(x, p, positions, ln_emb_g, ln_emb_b, w_in, w_uq, w_uqi, attn_norm_g, rwkv_mu, rwkv_w0, rwkv_w2, rwkv_a0, rwkv_a2, rwkv_g2, rwkv_k_k, rwkv_k_a, rwkv_r_k, rwkv_lnx_g, rwkv_lnx_b, pool_w, pool_scale, w_out, ln1_g, ln1_b, mlp_w1, mlp_w2, pe_proj, pe_gate, pe_gate_b, ln2_g, ln2_b):
    B, S, _ = x.shape
    T = B * S
    depth = w_in.shape[0]
    alpha = (2 * depth) ** 0.25
    bf = lambda a: a.astype(jnp.bfloat16)
    rowv = lambda a: a.reshape(1, -1)

    cos, sin = _rope_tables(positions.astype(jnp.float32).reshape(T, 1))
    xs = x.reshape(T, D_MODEL)
    for i in range(depth):
        win = bf(_arrange_w_in(w_in[i]))
        wq = w_uq[i] * (HEAD_DIM ** -0.5)
        wi = w_uqi[i] * (IDX_DIM ** -0.5)
        res = _inproj(xs, rowv(ln_emb_g), rowv(ln_emb_b), cos, sin, win,
                      bf(wq), bf(_per_head_partner(wq, ATT_HEADS)),
                      bf(wi), bf(_per_head_partner(wi, IDX_HEADS)), apply_ln=(i == 0))
        if i == 0:
            xs, res = res[0], res[1:]
        q, iq, iw, kr, ikr, va, zb, zc = res
        o_a = _attention(B, S, q, iq, iw, kr, ikr, va, rowv(attn_norm_g[i]))
        o_b = _rwkv(B, S, zb, rowv(rwkv_mu[i]), rowv(rwkv_w0[i]), bf(rwkv_w2[i]), rowv(rwkv_a0[i]),
                    bf(rwkv_a2[i]), bf(rwkv_g2[i]), rowv(rwkv_k_k[i]), rowv(rwkv_k_a[i]),
                    rowv(rwkv_r_k[i]), rowv(rwkv_lnx_g[i]), rowv(rwkv_lnx_b[i]))
        w_bd = jax.scipy.linalg.block_diag(*[pool_w[i, gi] for gi in range(len(POOL_WINDOWS))])
        o_c = _pool(B, S, zc, bf(w_bd), rowv(pool_scale[i]))
        wo = bf(w_out[i])
        xs = _outproj(alpha, o_a, o_b, o_c, xs, wo[:ATT_WIDTH], wo[ATT_WIDTH:ATT_WIDTH + RWKV_WIDTH],
                      wo[ATT_WIDTH + RWKV_WIDTH:], rowv(ln1_g[i]), rowv(ln1_b[i]))
        xs = _mlp(alpha, xs, p[i].reshape(T, PE_DIM), bf(mlp_w1[i]), bf(mlp_w2[i]), bf(pe_gate[i]),
                  rowv(pe_gate_b[i]), bf(pe_proj[i]), rowv(ln2_g[i]), rowv(ln2_b[i]))
    return xs.reshape(B, S, D_MODEL)
```

```python
import functools

import numpy as np
import jax
import jax.numpy as jnp
from jax import lax
from jax.experimental import pallas as pl
from jax.experimental.pallas import tpu as pltpu

D_MODEL = 1024
PE_DIM = 256
HEAD_DIM = 64
ROPE_THETA = 10000.0
ATT_WIDTH = 384
ATT_HEADS = 6
Q_RANK = 256
IDX_HEADS = 8
IDX_DIM = 64
TOPK_MAX = 256
RWKV_WIDTH = 384
RWKV_HEADS = 6
DECAY_RANK = 64
ICLR_RANK = 64
GATE_RANK = 128
GN_EPS = 64e-5
POOL_WINDOWS = (2, 4, 8, 16)
POOL_WIDTH = 256
POOL_GROUP = 64
B_COLS = 3 * RWKV_WIDTH + DECAY_RANK + ICLR_RANK + GATE_RANK
D_FF = 4 * D_MODEL
LN_EPS = 1e-5
RMS_EPS = 1e-6

COL_CQ = 0
COL_KIK = 256
COL_KIK_P = 384
COL_VIW = 512
COL_B = 640
COL_C = COL_B + B_COLS
N_IN_P = COL_C + POOL_WIDTH

LANES = 128
INT_MIN = -2 ** 31
VMEM_LIMIT = 56 * 1024 * 1024

TM_PROJ = 512
TQ = 256
CHUNK = 64
RWKV_ROWS = 4
TM_MLP = 512
TF_MLP = 1024

_NT = (((1,), (1,)), ((), ()))
_TN = (((0,), (0,)), ((), ()))


def _params(*sem):
    return pltpu.CompilerParams(dimension_semantics=sem, vmem_limit_bytes=VMEM_LIMIT)


def _ln(x, g, b):
    mu = jnp.mean(x, axis=-1, keepdims=True)
    xc = x - mu
    var = jnp.mean(xc * xc, axis=-1, keepdims=True)
    return xc * lax.rsqrt(var + LN_EPS) * g + b


def _bf(x):
    return x.astype(jnp.bfloat16)


def _dot(a, b):
    return jnp.dot(a, b, preferred_element_type=jnp.float32)


def _dot_nt(a, b):
    return lax.dot_general(a, b, _NT, preferred_element_type=jnp.float32)


def _dot_tn(a, b):
    return lax.dot_general(a, b, _TN, preferred_element_type=jnp.float32)


def _rope_kernel(pos_ref, inv_ref, cos_ref, sin_ref):
    ang = pos_ref[...] * inv_ref[...]
    cos_ref[...] = jnp.cos(ang)
    sin_ref[...] = jnp.sin(ang)


def _rope_tables(pos_f32):
    T = pos_f32.shape[0]
    tm = min(T, 2048)
    inv = ROPE_THETA ** (-np.arange(0, HEAD_DIM, 2, dtype=np.float32) / HEAD_DIM)
    inv128 = jnp.asarray(np.tile(inv, LANES // (HEAD_DIM // 2))[None, :], jnp.float32)
    return pl.pallas_call(
        _rope_kernel,
        out_shape=(jax.ShapeDtypeStruct((T, LANES), jnp.float32),) * 2,
        grid=(T // tm,),
        in_specs=[pl.BlockSpec((tm, 1), lambda i: (i, 0)),
                  pl.BlockSpec((1, LANES), lambda i: (0, 0))],
        out_specs=(pl.BlockSpec((tm, LANES), lambda i: (i, 0)),) * 2,
        compiler_params=_params("parallel"),
        name="rope_tables",
    )(pos_f32, inv128)


def _inproj_kernel(apply_ln, x_ref, g_ref, b_ref, cos_ref, sin_ref, win_ref, wq_ref, wqp_ref,
                   wi_ref, wip_ref, *out_refs):
    if apply_ln:
        xn_ref, q_ref, iq_ref, iw_ref, kr_ref, ikr_ref, va_ref, zb_ref, zc_ref = out_refs
        x = _ln(x_ref[...], g_ref[...], b_ref[...])
        xn_ref[...] = x
    else:
        q_ref, iq_ref, iw_ref, kr_ref, ikr_ref, va_ref, zb_ref, zc_ref = out_refs
        x = x_ref[...]
    z = _dot(_bf(x), win_ref[...])
    cos = cos_ref[...]
    sin = sin_ref[...]
    cq = _bf(z[:, COL_CQ:COL_CQ + Q_RANK])
    cos3 = jnp.concatenate([cos] * 3, axis=1)
    sin3 = jnp.concatenate([sin] * 3, axis=1)
    q = _bf(_dot(cq, wq_ref[...]) * cos3 + _dot(cq, wqp_ref[...]) * sin3)
    for j in range(q_ref.shape[0]):
        q_ref[j] = q[:, j * LANES:(j + 1) * LANES]
    cos4 = jnp.concatenate([cos] * 4, axis=1)
    sin4 = jnp.concatenate([sin] * 4, axis=1)
    iq = _bf(_dot(cq, wi_ref[...]) * cos4 + _dot(cq, wip_ref[...]) * sin4)
    for j in range(iq_ref.shape[0]):
        iq_ref[j] = iq[:, j * LANES:(j + 1) * LANES]
    kik = z[:, COL_KIK:COL_KIK + LANES] * cos + z[:, COL_KIK_P:COL_KIK_P + LANES] * sin
    kr_ref[...] = _bf(kik[:, :HEAD_DIM])
    ikr_ref[...] = _bf(kik[:, HEAD_DIM:])
    viw = z[:, COL_VIW:COL_VIW + LANES]
    va_ref[...] = _bf(viw[:, :HEAD_DIM])
    iw_ref[...] = viw * (IDX_HEADS ** -0.5)
    zb_ref[...] = z[:, COL_B:COL_B + B_COLS]
    zc_ref[...] = z[:, COL_C:COL_C + POOL_WIDTH]


def _inproj(x, g, b, cos, sin, win, wq, wqp, wi, wip, apply_ln):
    T = x.shape[0]
    tm = min(T, TM_PROJ)
    row = lambda i: (i, 0)
    fix = lambda i: (0, 0)
    bf16, f32 = jnp.bfloat16, jnp.float32
    outs = [(LANES, f32), (HEAD_DIM, bf16), (HEAD_DIM, bf16), (HEAD_DIM, bf16), (B_COLS, f32),
            (POOL_WIDTH, f32)]
    pairs = [ATT_WIDTH // LANES, IDX_HEADS * IDX_DIM // LANES]
    shapes = [jax.ShapeDtypeStruct((n, T, LANES), bf16) for n in pairs]
    shapes += [jax.ShapeDtypeStruct((T, n), dt) for n, dt in outs]
    specs = [pl.BlockSpec((n, tm, LANES), lambda i: (0, i, 0)) for n in pairs]
    specs += [pl.BlockSpec((tm, n), row) for n, _ in outs]
    if apply_ln:
        shapes = [jax.ShapeDtypeStruct((T, D_MODEL), f32)] + shapes
        specs = [pl.BlockSpec((tm, D_MODEL), row)] + specs
    return pl.pallas_call(
        functools.partial(_inproj_kernel, apply_ln),
        out_shape=tuple(shapes),
        grid=(T // tm,),
        in_specs=[pl.BlockSpec((tm, D_MODEL), row),
                  pl.BlockSpec((1, D_MODEL), fix), pl.BlockSpec((1, D_MODEL), fix),
                  pl.BlockSpec((tm, LANES), row), pl.BlockSpec((tm, LANES), row),
                  pl.BlockSpec((D_MODEL, N_IN_P), fix),
                  pl.BlockSpec((Q_RANK, ATT_WIDTH), fix), pl.BlockSpec((Q_RANK, ATT_WIDTH), fix),
                  pl.BlockSpec((Q_RANK, IDX_HEADS * IDX_DIM), fix),
                  pl.BlockSpec((Q_RANK, IDX_HEADS * IDX_DIM), fix)],
        out_specs=tuple(specs),
        compiler_params=_params("parallel"),
        name="inproj",
    )(x, g, b, cos, sin, win, wq, wqp, wi, wip)


def _attn_block(kend, search, topk, q_ref, iq_ref, iw_ref, kr_ref, ikr_ref, va_ref, g_ref, o_ref,
                key_ref, bias_ref, oacc_ref, tau_ref, cnt_ref):
    tq = q_ref.shape[1]
    row_t = pl.program_id(1) * tq + lax.broadcasted_iota(jnp.int32, (tq, 1), 0)
    col = lax.broadcasted_iota(jnp.int32, (1, kend), 1)
    causal = col <= row_t

    if not search:
        bias_ref[:, :kend] = jnp.where(causal, 0.0, -jnp.inf)
    else:
        lane = lax.broadcasted_iota(jnp.int32, (1, LANES), 1)
        bias_ref[:, :kend] = jnp.zeros((tq, kend), jnp.float32)

        def idx_pair(j, carry):
            iqp = iq_ref[j]
            ik = ikr_ref[0:kend, :]
            acc = bias_ref[:, :kend]
            for hh in range(2):
                lg = _dot_nt(iqp[:, hh * IDX_DIM:(hh + 1) * IDX_DIM], ik)
                w_h = jnp.sum(jnp.where(lane == HEAD_DIM + 2 * j + hh, iw_ref[...], 0.0), axis=1, keepdims=True)
                acc = acc + jnp.maximum(lg, 0.0) * w_h
            bias_ref[:, :kend] = acc
            return carry

        lax.fori_loop(0, iq_ref.shape[0], idx_pair, 0)
        score = bias_ref[:, :kend] + 0.0
        bits = pltpu.bitcast(score, jnp.int32)
        key = bits ^ ((bits >> 31) & 0x7FFFFFFF)
        key_ref[:, :kend] = jnp.where(causal, key, INT_MIN)

        def count_ge(cand):
            return jnp.sum(jnp.where(key_ref[:, :kend] >= cand, 1.0, 0.0), axis=1, keepdims=True)

        c0 = count_ge(jnp.zeros((tq, 1), jnp.int32))
        tau_ref[:, 0:1] = jnp.where(c0 >= topk, 0, INT_MIN).astype(jnp.int32)
        cnt_ref[:, 0:1] = jnp.where(c0 >= topk, c0, float(kend))

        def run_bits(hi, lo):
            def bit_step(i, carry):
                tau, cnt = carry
                cand = tau | (jnp.int32(1) << (hi - i))
                c = count_ge(cand)
                return jnp.where(c >= topk, cand, tau), jnp.where(c >= topk, c, cnt)

            tau, cnt = lax.fori_loop(0, hi - lo + 1, bit_step, (tau_ref[:, 0:1], cnt_ref[:, 0:1]))
            tau_ref[:, 0:1] = tau
            cnt_ref[:, 0:1] = cnt

        run_bits(30, 5)
        few = row_t + 1 <= topk
        at_zero = (tau_ref[:, 0:1] == 0) & (count_ge(jnp.ones((tq, 1), jnp.int32)) < topk)
        settled = (cnt_ref[:, 0:1] == topk) | few | at_zero
        pl.when(jnp.min(jnp.where(settled, 1.0, 0.0)) < 0.5)(functools.partial(run_bits, 4, 0))

        tau = jnp.maximum(tau_ref[:, 0:1], INT_MIN + 1)
        bias_ref[:, :kend] = jnp.where(key_ref[:, :kend] >= tau, 0.0, -jnp.inf)
        n_ge = jnp.where(few, 0.0, cnt_ref[:, 0:1])

        @pl.when(jnp.max(n_ge) > topk)
        def _():
            need = topk - jnp.sum(jnp.where(key_ref[:, :kend] > tau, 1.0, 0.0), axis=1, keepdims=True)
            ri = lax.broadcasted_iota(jnp.int32, (LANES, LANES), 0)
            ci = lax.broadcasted_iota(jnp.int32, (LANES, LANES), 1)
            upper = _bf(jnp.where(ri <= ci, 1.0, 0.0))
            before = jnp.zeros((tq, 1), jnp.float32)
            for c in range(kend // LANES):
                keyc = key_ref[:, c * LANES:(c + 1) * LANES]
                eq = keyc == tau
                rank = _dot(_bf(jnp.where(eq, 1.0, 0.0)), upper) + before
                keep = (keyc > tau) | (eq & (rank <= need))
                bias_ref[:, c * LANES:(c + 1) * LANES] = jnp.where(keep, 0.0, -jnp.inf)
                before = rank[:, LANES - 1:LANES]

    def att_pair(j, carry):
        qp = q_ref[j]
        kr = kr_ref[0:kend, :]
        va = va_ref[0:kend, :]
        bias = bias_ref[:, :kend]
        outs = []
        for hh in range(2):
            s = _dot_nt(qp[:, hh * HEAD_DIM:(hh + 1) * HEAD_DIM], kr) + bias
            p = jnp.exp(s - jnp.max(s, axis=1, keepdims=True))
            l = jnp.sum(p, axis=1, keepdims=True)
            outs.append(_dot(_bf(p), va) / l)
        oacc_ref[j] = jnp.concatenate(outs, axis=1)
        return carry

    lax.fori_loop(0, q_ref.shape[0], att_pair, 0)
    o = jnp.concatenate([oacc_ref[j] for j in range(q_ref.shape[0])], axis=1)
    o = o * lax.rsqrt(jnp.mean(o * o, axis=-1, keepdims=True) + RMS_EPS) * g_ref[...]
    o_ref[...] = _bf(o)


def _key_extents(S, tq, topk):
    out = []
    for v in range(S // tq):
        need = (v + 1) * tq
        if need <= topk:
            out.append((need, False))
        else:
            out.append((min(S, -(-need // (2 * tq)) * 2 * tq), True))
    return out


def _attn_kernel(S, *refs):
    tq = refs[0].shape[1]
    topk = min(TOPK_MAX, S // 4)
    qi = pl.program_id(1)
    extents = _key_extents(S, tq, topk)
    for ext in sorted(set(extents)):
        blocks = [v for v, e in enumerate(extents) if e == ext]
        pl.when((qi >= blocks[0]) & (qi <= blocks[-1]))(functools.partial(_attn_block, *ext, topk, *refs))


def _attention(B, S, q, iq, iw, kr, ikr, va, g):
    tq = min(S, TQ)
    nq = S // tq
    qrow = lambda b, i: (b * nq + i, 0)
    seq = lambda b, i: (b, 0)
    fix = lambda b, i: (0, 0)
    return pl.pallas_call(
        functools.partial(_attn_kernel, S),
        out_shape=jax.ShapeDtypeStruct((B * S, ATT_WIDTH), jnp.bfloat16),
        grid=(B, nq),
        in_specs=[pl.BlockSpec((q.shape[0], tq, LANES), lambda b, i: (0, b * nq + i, 0)),
                  pl.BlockSpec((iq.shape[0], tq, LANES), lambda b, i: (0, b * nq + i, 0)),
                  pl.BlockSpec((tq, LANES), qrow),
                  pl.BlockSpec((S, HEAD_DIM), seq), pl.BlockSpec((S, HEAD_DIM), seq),
                  pl.BlockSpec((S, HEAD_DIM), seq),
                  pl.BlockSpec((1, ATT_WIDTH), fix)],
        out_specs=pl.BlockSpec((tq, ATT_WIDTH), qrow),
        scratch_shapes=[pltpu.VMEM((tq, S), jnp.int32), pltpu.VMEM((tq, S), jnp.float32),
                        pltpu.VMEM((q.shape[0], tq, LANES), jnp.float32),
                        pltpu.VMEM((tq, LANES), jnp.int32), pltpu.VMEM((tq, LANES), jnp.float32)],
        compiler_params=_params("parallel", "parallel"),
        name="dsa_attention",
    )(q, iq, iw, kr, ikr, va, g)


def _split3(x):
    hi = _bf(x)
    r1 = x - hi.astype(jnp.float32)
    mid = _bf(r1)
    lo = _bf(r1 - mid.astype(jnp.float32))
    return hi, mid, lo


def _rwkv_kernel(zb_ref, mu_ref, w0_ref, w2_ref, a0_ref, a2_ref, g2_ref, kk_ref, ka_ref, rk_ref,
                 lg_ref, lb_ref, o_ref, prev_ref, state_ref):
    G, C, _ = zb_ref.shape
    W = RWKV_WIDTH
    R = G * C
    H = RWKV_HEADS

    @pl.when(pl.program_id(1) == 0)
    def _():
        prev_ref[...] = jnp.zeros_like(prev_ref)
        state_ref[...] = jnp.zeros_like(state_ref)

    x = zb_ref[...].reshape(R, B_COLS)
    rowi = lax.broadcasted_iota(jnp.int32, (R, 1), 0)
    shifted = pltpu.roll(x, 1, 0)
    for gi in range(G):
        shifted = jnp.where(rowi == gi * C, prev_ref[8 * gi:8 * gi + 1, :], shifted)
        prev_ref[8 * gi:8 * gi + 1, :] = x[gi * C + C - 1:gi * C + C, :]
    xm = x + (shifted - x) * mu_ref[...]
    r = xm[:, 0:W]
    k = xm[:, W:2 * W]
    v = xm[:, 2 * W:3 * W]
    wl = xm[:, 3 * W:3 * W + DECAY_RANK]
    al = xm[:, 3 * W + DECAY_RANK:3 * W + DECAY_RANK + ICLR_RANK]
    gl = xm[:, 3 * W + DECAY_RANK + ICLR_RANK:]

    wpre = w0_ref[...] + _dot(_bf(jnp.tanh(wl)), w2_ref[...])
    nw = -wpre
    w = -(jnp.maximum(nw, 0.0) + jnp.log(1.0 + jnp.exp(-jnp.abs(nw)))) - 0.5
    logd = -jnp.exp(w)
    a = jax.nn.sigmoid(a0_ref[...] + _dot(_bf(al), a2_ref[...]))
    g = _dot(_bf(jax.nn.sigmoid(gl)), g2_ref[...])

    rr = lax.broadcasted_iota(jnp.int32, (R, R), 0)
    cc = lax.broadcasted_iota(jnp.int32, (R, R), 1)
    tri = _bf(jnp.where((rr >= cc) & (rr // C == cc // C), 1.0, 0.0))
    hi, mid, lo = _split3(logd)
    cw = _dot(tri, hi) + _dot(tri, mid) + _dot(tri, lo)
    e_in = jnp.exp(cw)
    e_ex = jnp.exp(cw - logd)
    e_inv = jnp.exp(-cw)

    kk = k * kk_ref[...]
    k2 = k * (1.0 + (a - 1.0) * ka_ref[...])
    rk2 = r * k2 * rk_ref[...]
    ri = lax.broadcasted_iota(jnp.int32, (C, C), 0)
    ci = lax.broadcasted_iota(jnp.int32, (C, C), 1)
    strict = ri > ci
    ri2 = lax.broadcasted_iota(jnp.int32, (C, 2 * C), 0)
    ci2 = lax.broadcasted_iota(jnp.int32, (C, 2 * C), 1)
    incl2 = ri2 >= jnp.where(ci2 >= C, ci2 - C, ci2)
    eye = jnp.where(ri == ci, 1.0, 0.0)

    units = [(gi, h) for gi in range(G) for h in range(H)]

    def cut(arr, u):
        gi, h = u
        return arr[gi * C:(gi + 1) * C, h * HEAD_DIM:(h + 1) * HEAD_DIM]

    def unit_norm(u):
        kkh = cut(kk, u)
        return kkh / jnp.maximum(jnp.sqrt(jnp.sum(kkh * kkh, axis=-1, keepdims=True)), 1e-12)

    kkn = [unit_norm(u) for u in units]
    v_u = [cut(v, u) for u in units]
    at = [_bf(-kn * cut(e_ex, u)) for kn, u in zip(kkn, units)]
    rt = [_bf(cut(r, u) * cut(e_in, u)) for u in units]
    bt = [kn * cut(a, u) * cut(e_inv, u) for kn, u in zip(kkn, units)]
    kt = [cut(k2, u) * cut(e_inv, u) for u in units]
    bk16 = [_bf(jnp.concatenate([b_, k_], axis=0)) for b_, k_ in zip(bt, kt)]
    v16 = [_bf(v_) for v_ in v_u]
    l_ab = [jnp.where(strict, _dot_nt(a_, _bf(b_)), 0.0) for a_, b_ in zip(at, bt)]
    a_ak = [_bf(jnp.where(strict, _dot_nt(a_, _bf(k_)), 0.0)) for a_, k_ in zip(at, kt)]
    tinv = [eye + l_ for l_ in l_ab]
    pw = l_ab
    n = 2
    while n < C:
        pw = [_dot(_bf(p_), _bf(p_)) for p_ in pw]
        tinv = [t_ + _dot(_bf(t_), _bf(p_)) for t_, p_ in zip(tinv, pw)]
        n *= 2
    s0 = [state_ref[i] for i in range(len(units))]
    s016 = [_bf(s_) for s_ in s0]
    rhs = [_dot_nt(a_, s_) + _dot(m_, v_) for a_, s_, m_, v_ in zip(at, s016, a_ak, v16)]
    uu = [_dot(_bf(t_), _bf(x_)) for t_, x_ in zip(tinv, rhs)]
    uv16 = [_bf(jnp.concatenate([u_, v_], axis=0)) for u_, v_ in zip(uu, v_u)]
    g_r = [_bf(jnp.where(incl2, _dot_nt(r_, m_), 0.0)) for r_, m_ in zip(rt, bk16)]
    y = [_dot(m_, x_) + _dot_nt(r_, s_) for m_, x_, r_, s_ in zip(g_r, uv16, rt, s016)]
    for i, u in enumerate(units):
        state_ref[i] = (s0[i] + _dot_tn(uv16[i], bk16[i])) * cut(e_in, u)[C - 1:C, :]
    for gi in range(G):
        outs = []
        for h in range(H):
            i = gi * H + h
            u = units[i]
            sl = slice(h * HEAD_DIM, (h + 1) * HEAD_DIM)
            ym = jnp.mean(y[i], axis=-1, keepdims=True)
            yc = y[i] - ym
            yv = jnp.mean(yc * yc, axis=-1, keepdims=True)
            yn = yc * lax.rsqrt(yv + GN_EPS) * lg_ref[:, sl] + lb_ref[:, sl]
            bonus = jnp.sum(cut(rk2, u), axis=-1, keepdims=True) * v_u[i]
            outs.append((yn + bonus) * cut(g, u))
        o_ref[gi] = _bf(jnp.concatenate(outs, axis=1))


def _rwkv(B, S, zb, mu, w0, w2, a0, a2, g2, k_k, k_a, r_k, lnx_g, lnx_b):
    C = min(S, CHUNK)
    G = min(B, RWKV_ROWS)
    blk = lambda b, c: (b, c, 0)
    fix = lambda b, c: (0, 0)
    vec = lambda n: pl.BlockSpec((1, n), fix)
    return pl.pallas_call(
        _rwkv_kernel,
        out_shape=jax.ShapeDtypeStruct((B, S, RWKV_WIDTH), jnp.bfloat16),
        grid=(B // G, S // C),
        in_specs=[pl.BlockSpec((G, C, B_COLS), blk), vec(B_COLS), vec(RWKV_WIDTH),
                  pl.BlockSpec((DECAY_RANK, RWKV_WIDTH), fix), vec(RWKV_WIDTH),
                  pl.BlockSpec((ICLR_RANK, RWKV_WIDTH), fix),
                  pl.BlockSpec((GATE_RANK, RWKV_WIDTH), fix),
                  vec(RWKV_WIDTH), vec(RWKV_WIDTH), vec(RWKV_WIDTH), vec(RWKV_WIDTH),
                  vec(RWKV_WIDTH)],
        out_specs=pl.BlockSpec((G, C, RWKV_WIDTH), blk),
        scratch_shapes=[pltpu.VMEM((8 * G, B_COLS), jnp.float32),
                        pltpu.VMEM((G * RWKV_HEADS, HEAD_DIM, HEAD_DIM), jnp.float32)],
        compiler_params=_params("parallel", "arbitrary"),
        name="rwkv7",
    )(zb.reshape(B, S, B_COLS), mu, w0, w2, a0, a2, g2, k_k, k_a, r_k, lnx_g, lnx_b
      ).reshape(B * S, RWKV_WIDTH)


def _pool_kernel(zc_ref, w_ref, sc_ref, o_ref):
    S = zc_ref.shape[0]
    x = zc_ref[...]
    row = lax.broadcasted_iota(jnp.int32, (S, 1), 0)
    lane_grp = lax.broadcasted_iota(jnp.int32, (1, POOL_WIDTH), 1) // POOL_GROUP

    def lag(y, n):
        return jnp.where(row >= n, pltpu.roll(y, n, 0), 0.0)

    w2 = x + lag(x, 1)
    w4 = w2 + lag(w2, 2)
    w8 = w4 + lag(w4, 4)
    w16 = w8 + lag(w8, 8)
    wsum = jnp.where(lane_grp == 0, w2, jnp.where(lane_grp == 1, w4, jnp.where(lane_grp == 2, w8, w16)))
    win = jnp.where(lane_grp == 0, 2, jnp.where(lane_grp == 1, 4, jnp.where(lane_grp == 2, 8, 16)))
    count = jnp.minimum(row + 1, win).astype(jnp.float32)
    pooled = wsum / count - x
    o_ref[...] = _bf(_dot(_bf(pooled), w_ref[...]) * sc_ref[...])


def _pool(B, S, zc, w_bd, scale):
    return pl.pallas_call(
        _pool_kernel,
        out_shape=jax.ShapeDtypeStruct((B * S, POOL_WIDTH), jnp.bfloat16),
        grid=(B,),
        in_specs=[pl.BlockSpec((S, POOL_WIDTH), lambda b: (b, 0)),
                  pl.BlockSpec((POOL_WIDTH, POOL_WIDTH), lambda b: (0, 0)),
                  pl.BlockSpec((1, POOL_WIDTH), lambda b: (0, 0))],
        out_specs=pl.BlockSpec((S, POOL_WIDTH), lambda b: (b, 0)),
        compiler_params=_params("parallel"),
        name="pool",
    )(zc, w_bd, scale)


def _outproj_kernel(alpha, oa_ref, ob_ref, oc_ref, x_ref, wa_ref, wb_ref, wc_ref, g_ref, b_ref, o_ref):
    mix = _dot(oa_ref[...], wa_ref[...]) + _dot(ob_ref[...], wb_ref[...]) + _dot(oc_ref[...], wc_ref[...])
    o_ref[...] = _ln(alpha * x_ref[...] + mix, g_ref[...], b_ref[...])


def _outproj(alpha, oa, ob, oc, x, wa, wb, wc, g, b):
    T = x.shape[0]
    tm = min(T, TM_PROJ)
    row = lambda i: (i, 0)
    fix = lambda i: (0, 0)
    return pl.pallas_call(
        functools.partial(_outproj_kernel, alpha),
        out_shape=jax.ShapeDtypeStruct((T, D_MODEL), jnp.float32),
        grid=(T // tm,),
        in_specs=[pl.BlockSpec((tm, ATT_WIDTH), row), pl.BlockSpec((tm, RWKV_WIDTH), row),
                  pl.BlockSpec((tm, POOL_WIDTH), row), pl.BlockSpec((tm, D_MODEL), row),
                  pl.BlockSpec((ATT_WIDTH, D_MODEL), fix), pl.BlockSpec((RWKV_WIDTH, D_MODEL), fix),
                  pl.BlockSpec((POOL_WIDTH, D_MODEL), fix),
                  pl.BlockSpec((1, D_MODEL), fix), pl.BlockSpec((1, D_MODEL), fix)],
        out_specs=pl.BlockSpec((tm, D_MODEL), row),
        compiler_params=_params("parallel"),
        name="outproj_ln1",
    )(oa, ob, oc, x, wa, wb, wc, g, b)


def _mlp_kernel(alpha, x_ref, p_ref, w1_ref, w2_ref, pg_ref, pgb_ref, pp_ref, g_ref, b_ref, o_ref,
                xb_ref, acc_ref):
    kf = pl.program_id(1)

    @pl.when(kf == 0)
    def _():
        x = x_ref[...]
        xb = _bf(x)
        xb_ref[...] = xb
        gate = jax.nn.sigmoid(_dot(xb, pg_ref[...]) + pgb_ref[...])
        acc_ref[...] = alpha * x + gate * _dot(_bf(p_ref[...]), pp_ref[...])

    h = jnp.maximum(_dot(xb_ref[...], w1_ref[...]), 0.0)
    acc_ref[...] += _dot(_bf(h * h), w2_ref[...])

    @pl.when(kf == pl.num_programs(1) - 1)
    def _():
        o_ref[...] = _ln(acc_ref[...], g_ref[...], b_ref[...])


def _mlp(alpha, x, p, w1, w2, pg, pgb, pp, g, b):
    T = x.shape[0]
    tm = min(T, TM_MLP)
    tf = TF_MLP
    row = lambda i, k: (i, 0)
    fix = lambda i, k: (0, 0)
    return pl.pallas_call(
        functools.partial(_mlp_kernel, alpha),
        out_shape=jax.ShapeDtypeStruct((T, D_MODEL), jnp.float32),
        grid=(T // tm, D_FF // tf),
        in_specs=[pl.BlockSpec((tm, D_MODEL), row), pl.BlockSpec((tm, PE_DIM), row),
                  pl.BlockSpec((D_MODEL, tf), lambda i, k: (0, k)),
                  pl.BlockSpec((tf, D_MODEL), lambda i, k: (k, 0)),
                  pl.BlockSpec((D_MODEL, D_MODEL), fix), pl.BlockSpec((1, D_MODEL), fix),
                  pl.BlockSpec((PE_DIM, D_MODEL), fix),
                  pl.BlockSpec((1, D_MODEL), fix), pl.BlockSpec((1, D_MODEL), fix)],
        out_specs=pl.BlockSpec((tm, D_MODEL), row),
        scratch_shapes=[pltpu.VMEM((tm, D_MODEL), jnp.bfloat16),
                        pltpu.VMEM((tm, D_MODEL), jnp.float32)],
        compiler_params=_params("parallel", "arbitrary"),
        name="mlp_ple_ln2",
    )(x, p, w1, w2, pg, pgb, pp, g, b)


def _rot_partner(w):
    half = HEAD_DIM // 2
    return jnp.concatenate([-w[..., half:], w[..., :half]], axis=-1)


def _per_head_partner(w, heads):
    r = w.reshape(w.shape[0], heads, HEAD_DIM)
    return _rot_partner(r).reshape(w.shape)


def _arrange_w_in(w):
    cq = w[:, 0:256]
    ka = w[:, 256:320]
    va = w[:, 320:384]
    ik = w[:, 384:448]
    iw = w[:, 448:456]
    rest = w[:, 456:]
    pad = jnp.zeros((w.shape[0], LANES - HEAD_DIM - IDX_HEADS), w.dtype)
    return jnp.concatenate([cq, ka, ik, _rot_partner(ka), _rot_partner(ik), va, iw, pad, rest], axis=1)


def kernel(x, p, positions, ln_emb_g, ln_emb_b, w_in, w_uq, w_uqi, attn_norm_g, rwkv_mu, rwkv_w0, rwkv_w2, rwkv_a0, rwkv_a2, rwkv_g2, rwkv_k_k, rwkv_k_a, rwkv_r_k, rwkv_lnx_g, rwkv_lnx_b, pool_w, pool_scale, w_out, ln1_g, ln1_b, mlp_w1, mlp_w2, pe_proj, pe_gate, pe_gate_b, ln2_g, ln2_b):
    B, S, _ = x.shape
    T = B * S
    depth = w_in.shape[0]
    alpha = (2 * depth) ** 0.25
    bf = lambda a: a.astype(jnp.bfloat16)
    rowv = lambda a: a.reshape(1, -1)

    cos, sin = _rope_tables(positions.astype(jnp.float32).reshape(T, 1))
    xs = x.reshape(T, D_MODEL)
    for i in range(depth):
        win = bf(_arrange_w_in(w_in[i]))
        wq = w_uq[i] * (HEAD_DIM ** -0.5)
        wi = w_uqi[i] * (IDX_DIM ** -0.5)
        res = _inproj(xs, rowv(ln_emb_g), rowv(ln_emb_b), cos, sin, win,
                      bf(wq), bf(_per_head_partner(wq, ATT_HEADS)),
                      bf(wi), bf(_per_head_partner(wi, IDX_HEADS)), apply_ln=(i == 0))
        if i == 0:
            xs, res = res[0], res[1:]
        q, iq, iw, kr, ikr, va, zb, zc = res
        o_a = _attention(B, S, q, iq, iw, kr, ikr, va, rowv(attn_norm_g[i]))
        o_b = _rwkv(B, S, zb, rowv(rwkv_mu[i]), rowv(rwkv_w0[i]), bf(rwkv_w2[i]), rowv(rwkv_a0[i]),
                    bf(rwkv_a2[i]), bf(rwkv_g2[i]), rowv(rwkv_k_k[i]), rowv(rwkv_k_a[i]),
                    rowv(rwkv_r_k[i]), rowv(rwkv_lnx_g[i]), rowv(rwkv_lnx_b[i]))
        w_bd = jax.scipy.linalg.block_diag(*[pool_w[i, gi] for gi in range(len(POOL_WINDOWS))])
        o_c = _pool(B, S, zc, bf(w_bd), rowv(pool_scale[i]))
        wo = bf(w_out[i])
        xs = _outproj(alpha, o_a, o_b, o_c, xs, wo[:ATT_WIDTH], wo[ATT_WIDTH:ATT_WIDTH + RWKV_WIDTH],
                      wo[ATT_WIDTH + RWKV_WIDTH:], rowv(ln1_g[i]), rowv(ln1_b[i]))
        xs = _mlp(alpha, xs, p[i].reshape(T, PE_DIM), bf(mlp_w1[i]), bf(mlp_w2[i]), bf(pe_gate[i]),
                  rowv(pe_gate_b[i]), bf(pe_proj[i]), rowv(ln2_g[i]), rowv(ln2_b[i]))
    return xs.reshape(B, S, D_MODEL)
```

```python
import functools

import numpy as np
import jax
import jax.numpy as jnp
from jax import lax
from jax.experimental import pallas as pl
from jax.experimental.pallas import tpu as pltpu

D_MODEL = 1024
PE_DIM = 256
HEAD_DIM = 64
ROPE_THETA = 10000.0
ATT_WIDTH = 384
ATT_HEADS = 6
Q_RANK = 256
IDX_HEADS = 8
IDX_DIM = 64
TOPK_MAX = 256
RWKV_WIDTH = 384
RWKV_HEADS = 6
DECAY_RANK = 64
ICLR_RANK = 64
GATE_RANK = 128
GN_EPS = 64e-5
POOL_WINDOWS = (2, 4, 8, 16)
POOL_WIDTH = 256
POOL_GROUP = 64
B_COLS = 3 * RWKV_WIDTH + DECAY_RANK + ICLR_RANK + GATE_RANK
D_FF = 4 * D_MODEL
LN_EPS = 1e-5
RMS_EPS = 1e-6

COL_CQ = 0
COL_KIK = 256
COL_KIK_P = 384
COL_VIW = 512
COL_B = 640
COL_C = COL_B + B_COLS
N_IN_P = COL_C + POOL_WIDTH

LANES = 128
INT_MIN = -2 ** 31
VMEM_LIMIT = 56 * 1024 * 1024

TM_PROJ = 512
TQ = 256
ATT_KEY_CHUNK = 512
CHUNK = 64
RWKV_ROWS = 4
TM_MLP = 512
TF_MLP = 1024

_NT = (((1,), (1,)), ((), ()))
_TN = (((0,), (0,)), ((), ()))


def _params(*sem):
    return pltpu.CompilerParams(dimension_semantics=sem, vmem_limit_bytes=VMEM_LIMIT)


def _ln(x, g, b):
    mu = jnp.mean(x, axis=-1, keepdims=True)
    xc = x - mu
    var = jnp.mean(xc * xc, axis=-1, keepdims=True)
    return xc * lax.rsqrt(var + LN_EPS) * g + b


def _bf(x):
    return x.astype(jnp.bfloat16)


def _dot(a, b):
    return jnp.dot(a, b, preferred_element_type=jnp.float32)


def _dot_nt(a, b):
    return lax.dot_general(a, b, _NT, preferred_element_type=jnp.float32)


def _dot_tn(a, b):
    return lax.dot_general(a, b, _TN, preferred_element_type=jnp.float32)


def _rope_kernel(pos_ref, inv_ref, cos_ref, sin_ref):
    ang = pos_ref[...] * inv_ref[...]
    cos_ref[...] = jnp.cos(ang)
    sin_ref[...] = jnp.sin(ang)


def _rope_tables(pos_f32):
    T = pos_f32.shape[0]
    tm = min(T, 2048)
    inv = ROPE_THETA ** (-np.arange(0, HEAD_DIM, 2, dtype=np.float32) / HEAD_DIM)
    inv128 = jnp.asarray(np.tile(inv, LANES // (HEAD_DIM // 2))[None, :], jnp.float32)
    return pl.pallas_call(
        _rope_kernel,
        out_shape=(jax.ShapeDtypeStruct((T, LANES), jnp.float32),) * 2,
        grid=(T // tm,),
        in_specs=[pl.BlockSpec((tm, 1), lambda i: (i, 0)),
                  pl.BlockSpec((1, LANES), lambda i: (0, 0))],
        out_specs=(pl.BlockSpec((tm, LANES), lambda i: (i, 0)),) * 2,
        compiler_params=_params("parallel"),
        name="rope_tables",
    )(pos_f32, inv128)


def _inproj_kernel(apply_ln, x_ref, g_ref, b_ref, cos_ref, sin_ref, win_ref, wq_ref, wqp_ref,
                   wi_ref, wip_ref, *out_refs):
    if apply_ln:
        xn_ref, q_ref, iq_ref, iw_ref, kr_ref, ikr_ref, va_ref, zb_ref, zc_ref = out_refs
        x = _ln(x_ref[...], g_ref[...], b_ref[...])
        xn_ref[...] = x
    else:
        q_ref, iq_ref, iw_ref, kr_ref, ikr_ref, va_ref, zb_ref, zc_ref = out_refs
        x = x_ref[...]
    z = _dot(_bf(x), win_ref[...])
    cos = cos_ref[...]
    sin = sin_ref[...]
    cq = _bf(z[:, COL_CQ:COL_CQ + Q_RANK])
    cos3 = jnp.concatenate([cos] * 3, axis=1)
    sin3 = jnp.concatenate([sin] * 3, axis=1)
    q = _bf(_dot(cq, wq_ref[...]) * cos3 + _dot(cq, wqp_ref[...]) * sin3)
    for j in range(q_ref.shape[0]):
        q_ref[j] = q[:, j * LANES:(j + 1) * LANES]
    cos4 = jnp.concatenate([cos] * 4, axis=1)
    sin4 = jnp.concatenate([sin] * 4, axis=1)
    iq = _bf(_dot(cq, wi_ref[...]) * cos4 + _dot(cq, wip_ref[...]) * sin4)
    for j in range(iq_ref.shape[0]):
        iq_ref[j] = iq[:, j * LANES:(j + 1) * LANES]
    kik = z[:, COL_KIK:COL_KIK + LANES] * cos + z[:, COL_KIK_P:COL_KIK_P + LANES] * sin
    kr_ref[...] = _bf(kik[:, :HEAD_DIM])
    ikr_ref[...] = _bf(kik[:, HEAD_DIM:])
    viw = z[:, COL_VIW:COL_VIW + LANES]
    va_ref[...] = _bf(viw[:, :HEAD_DIM])
    iw_ref[...] = viw * (IDX_HEADS ** -0.5)
    zb_ref[...] = z[:, COL_B:COL_B + B_COLS]
    zc_ref[...] = z[:, COL_C:COL_C + POOL_WIDTH]


def _inproj(x, g, b, cos, sin, win, wq, wqp, wi, wip, apply_ln):
    T = x.shape[0]
    tm = min(T, TM_PROJ)
    row = lambda i: (i, 0)
    fix = lambda i: (0, 0)
    bf16, f32 = jnp.bfloat16, jnp.float32
    outs = [(LANES, f32), (HEAD_DIM, bf16), (HEAD_DIM, bf16), (HEAD_DIM, bf16), (B_COLS, f32),
            (POOL_WIDTH, f32)]
    pairs = [ATT_WIDTH // LANES, IDX_HEADS * IDX_DIM // LANES]
    shapes = [jax.ShapeDtypeStruct((n, T, LANES), bf16) for n in pairs]
    shapes += [jax.ShapeDtypeStruct((T, n), dt) for n, dt in outs]
    specs = [pl.BlockSpec((n, tm, LANES), lambda i: (0, i, 0)) for n in pairs]
    specs += [pl.BlockSpec((tm, n), row) for n, _ in outs]
    if apply_ln:
        shapes = [jax.ShapeDtypeStruct((T, D_MODEL), f32)] + shapes
        specs = [pl.BlockSpec((tm, D_MODEL), row)] + specs
    return pl.pallas_call(
        functools.partial(_inproj_kernel, apply_ln),
        out_shape=tuple(shapes),
        grid=(T // tm,),
        in_specs=[pl.BlockSpec((tm, D_MODEL), row),
                  pl.BlockSpec((1, D_MODEL), fix), pl.BlockSpec((1, D_MODEL), fix),
                  pl.BlockSpec((tm, LANES), row), pl.BlockSpec((tm, LANES), row),
                  pl.BlockSpec((D_MODEL, N_IN_P), fix),
                  pl.BlockSpec((Q_RANK, ATT_WIDTH), fix), pl.BlockSpec((Q_RANK, ATT_WIDTH), fix),
                  pl.BlockSpec((Q_RANK, IDX_HEADS * IDX_DIM), fix),
                  pl.BlockSpec((Q_RANK, IDX_HEADS * IDX_DIM), fix)],
        out_specs=tuple(specs),
        compiler_params=_params("parallel"),
        name="inproj",
    )(x, g, b, cos, sin, win, wq, wqp, wi, wip)


def _attn_block(kend, search, topk, q_ref, iq_ref, iw_ref, kr_ref, ikr_ref, va_ref, g_ref, o_ref,
                key_ref, bias_ref, oacc_ref, hi_ref, lo_ref):
    tq = q_ref.shape[1]
    row_t = pl.program_id(1) * tq + lax.broadcasted_iota(jnp.int32, (tq, 1), 0)
    col = lax.broadcasted_iota(jnp.int32, (1, kend), 1)
    causal = col <= row_t

    if not search:
        bias_ref[:, :kend] = jnp.where(causal, 0.0, -jnp.inf)
    else:
        lane = lax.broadcasted_iota(jnp.int32, (1, LANES), 1)
        bias_ref[:, :kend] = jnp.zeros((tq, kend), jnp.float32)

        def idx_pair(j, carry):
            iqp = iq_ref[j]
            ik = ikr_ref[0:kend, :]
            acc = bias_ref[:, :kend]
            for hh in range(2):
                lg = _dot_nt(iqp[:, hh * IDX_DIM:(hh + 1) * IDX_DIM], ik)
                w_h = jnp.sum(jnp.where(lane == HEAD_DIM + 2 * j + hh, iw_ref[...], 0.0), axis=1, keepdims=True)
                acc = acc + jnp.maximum(lg, 0.0) * w_h
            bias_ref[:, :kend] = acc
            return carry

        lax.fori_loop(0, iq_ref.shape[0], idx_pair, 0)
        score = bias_ref[:, :kend] + 0.0
        bits = pltpu.bitcast(score, jnp.int32)
        key = bits ^ ((bits >> 31) & 0x7FFFFFFF)
        key_ref[:, :kend] = jnp.where(causal, key, INT_MIN)

        i16 = jnp.int16
        one16 = jnp.ones((tq, LANES), i16)
        zero16 = jnp.zeros((tq, LANES), i16)

        def count16(ref, cand):
            c16 = jnp.broadcast_to(cand, (tq, LANES)).astype(i16)
            acc = zero16
            for i in range(kend // LANES):
                acc = acc + jnp.where(ref[:, i * LANES:(i + 1) * LANES] >= c16, one16, zero16)
            return jnp.sum(acc.astype(jnp.float32), axis=1, keepdims=True)

        def kth_largest16(ref, k):
            t = jnp.where(count16(ref, jnp.zeros((tq, 1), jnp.int32)) >= k, 0, -32768).astype(jnp.int32)

            def bit_step(i, t):
                cand = t | (jnp.int32(1) << (14 - i))
                return jnp.where(count16(ref, cand) >= k, cand, t)

            return lax.fori_loop(0, 15, bit_step, t)

        keyv = key_ref[:, :kend]
        hi_ref[:, :kend] = (keyv >> 16).astype(i16)
        tau_hi = kth_largest16(hi_ref, jnp.full((tq, 1), float(topk), jnp.float32))
        above = jnp.where(tau_hi < 32767, count16(hi_ref, jnp.minimum(tau_hi + 1, 32767)), 0.0)
        lo = ((keyv & 0xFFFF) - 32768).astype(i16)
        same_hi = hi_ref[:, :kend] == jnp.broadcast_to(tau_hi, (tq, kend)).astype(i16)
        lo_ref[:, :kend] = jnp.where(same_hi, lo, -32768)
        tau_lo = kth_largest16(lo_ref, topk - above)
        tau = (tau_hi << 16) | (tau_lo + 32768)
        tau = jnp.maximum(tau, INT_MIN + 1)
        ge = key_ref[:, :kend] >= tau
        n_ge = jnp.sum(jnp.where(ge, 1.0, 0.0), axis=1, keepdims=True)
        bias_ref[:, :kend] = jnp.where(ge, 0.0, -jnp.inf)

        @pl.when(jnp.max(n_ge) > topk)
        def _():
            need = topk - jnp.sum(jnp.where(key_ref[:, :kend] > tau, 1.0, 0.0), axis=1, keepdims=True)
            ri = lax.broadcasted_iota(jnp.int32, (LANES, LANES), 0)
            ci = lax.broadcasted_iota(jnp.int32, (LANES, LANES), 1)
            upper = _bf(jnp.where(ri <= ci, 1.0, 0.0))
            before = jnp.zeros((tq, 1), jnp.float32)
            for c in range(kend // LANES):
                keyc = key_ref[:, c * LANES:(c + 1) * LANES]
                eq = keyc == tau
                rank = _dot(_bf(jnp.where(eq, 1.0, 0.0)), upper) + before
                keep = (keyc > tau) | (eq & (rank <= need))
                bias_ref[:, c * LANES:(c + 1) * LANES] = jnp.where(keep, 0.0, -jnp.inf)
                before = rank[:, LANES - 1:LANES]

    kc = min(kend, ATT_KEY_CHUNK)
    lane_tiles = lambda a: [a[:, i * LANES:(i + 1) * LANES] for i in range(a.shape[1] // LANES)]

    def att_pair(j, carry):
        qp = q_ref[j]
        chunks = range(0, kend, kc)
        s_h = [[_dot_nt(qp[:, hh * HEAD_DIM:(hh + 1) * HEAD_DIM], kr_ref[c:c + kc, :]) + bias_ref[:, c:c + kc]
                for c in chunks] for hh in range(2)]
        outs = []
        for s_c in s_h:
            m = jnp.max(functools.reduce(jnp.maximum, [t for s in s_c for t in lane_tiles(s)]),
                        axis=1, keepdims=True)
            p_c = [jnp.exp(s - m) for s in s_c]
            l = jnp.sum(sum(t for p in p_c for t in lane_tiles(p)), axis=1, keepdims=True)
            acc = sum(_dot(_bf(p), va_ref[c:c + kc, :]) for p, c in zip(p_c, chunks))
            outs.append(acc / l)
        oacc_ref[j] = jnp.concatenate(outs, axis=1)
        return carry

    lax.fori_loop(0, q_ref.shape[0], att_pair, 0)
    o = jnp.concatenate([oacc_ref[j] for j in range(q_ref.shape[0])], axis=1)
    o = o * lax.rsqrt(jnp.mean(o * o, axis=-1, keepdims=True) + RMS_EPS) * g_ref[...]
    o_ref[...] = _bf(o)


def _key_extents(S, tq, topk):
    out = []
    for v in range(S // tq):
        need = (v + 1) * tq
        if need <= topk:
            out.append((need, False))
        else:
            out.append((min(S, -(-need // (2 * tq)) * 2 * tq), True))
    return out


def _attn_kernel(S, *refs):
    tq = refs[0].shape[1]
    topk = min(TOPK_MAX, S // 4)
    qi = pl.program_id(1)
    extents = _key_extents(S, tq, topk)
    for ext in sorted(set(extents)):
        blocks = [v for v, e in enumerate(extents) if e == ext]
        pl.when((qi >= blocks[0]) & (qi <= blocks[-1]))(functools.partial(_attn_block, *ext, topk, *refs))


def _attention(B, S, q, iq, iw, kr, ikr, va, g):
    tq = min(S, TQ)
    nq = S // tq
    qrow = lambda b, i: (b * nq + i, 0)
    seq = lambda b, i: (b, 0)
    fix = lambda b, i: (0, 0)
    return pl.pallas_call(
        functools.partial(_attn_kernel, S),
        out_shape=jax.ShapeDtypeStruct((B * S, ATT_WIDTH), jnp.bfloat16),
        grid=(B, nq),
        in_specs=[pl.BlockSpec((q.shape[0], tq, LANES), lambda b, i: (0, b * nq + i, 0)),
                  pl.BlockSpec((iq.shape[0], tq, LANES), lambda b, i: (0, b * nq + i, 0)),
                  pl.BlockSpec((tq, LANES), qrow),
                  pl.BlockSpec((S, HEAD_DIM), seq), pl.BlockSpec((S, HEAD_DIM), seq),
                  pl.BlockSpec((S, HEAD_DIM), seq),
                  pl.BlockSpec((1, ATT_WIDTH), fix)],
        out_specs=pl.BlockSpec((tq, ATT_WIDTH), qrow),
        scratch_shapes=[pltpu.VMEM((tq, S), jnp.int32), pltpu.VMEM((tq, S), jnp.float32),
                        pltpu.VMEM((q.shape[0], tq, LANES), jnp.float32),
                        pltpu.VMEM((tq, S), jnp.int16), pltpu.VMEM((tq, S), jnp.int16)],
        compiler_params=_params("parallel", "parallel"),
        name="dsa_attention",
    )(q, iq, iw, kr, ikr, va, g)


def _split3(x):
    hi = _bf(x)
    r1 = x - hi.astype(jnp.float32)
    mid = _bf(r1)
    lo = _bf(r1 - mid.astype(jnp.float32))
    return hi, mid, lo


def _rwkv_kernel(zb_ref, mu_ref, w0_ref, w2_ref, a0_ref, a2_ref, g2_ref, kk_ref, ka_ref, rk_ref,
                 lg_ref, lb_ref, o_ref, prev_ref, state_ref):
    G, C, _ = zb_ref.shape
    W = RWKV_WIDTH
    R = G * C
    H = RWKV_HEADS

    @pl.when(pl.program_id(1) == 0)
    def _():
        prev_ref[...] = jnp.zeros_like(prev_ref)
        state_ref[...] = jnp.zeros_like(state_ref)

    x = zb_ref[...].reshape(R, B_COLS)
    rowi = lax.broadcasted_iota(jnp.int32, (R, 1), 0)
    shifted = pltpu.roll(x, 1, 0)
    for gi in range(G):
        shifted = jnp.where(rowi == gi * C, prev_ref[8 * gi:8 * gi + 1, :], shifted)
        prev_ref[8 * gi:8 * gi + 1, :] = x[gi * C + C - 1:gi * C + C, :]
    xm = x + (shifted - x) * mu_ref[...]
    r = xm[:, 0:W]
    k = xm[:, W:2 * W]
    v = xm[:, 2 * W:3 * W]
    wl = xm[:, 3 * W:3 * W + DECAY_RANK]
    al = xm[:, 3 * W + DECAY_RANK:3 * W + DECAY_RANK + ICLR_RANK]
    gl = xm[:, 3 * W + DECAY_RANK + ICLR_RANK:]

    wpre = w0_ref[...] + _dot(_bf(jnp.tanh(wl)), w2_ref[...])
    nw = -wpre
    w = -(jnp.maximum(nw, 0.0) + jnp.log(1.0 + jnp.exp(-jnp.abs(nw)))) - 0.5
    logd = -jnp.exp(w)
    a = jax.nn.sigmoid(a0_ref[...] + _dot(_bf(al), a2_ref[...]))
    g = _dot(_bf(jax.nn.sigmoid(gl)), g2_ref[...])

    rr = lax.broadcasted_iota(jnp.int32, (R, R), 0)
    cc = lax.broadcasted_iota(jnp.int32, (R, R), 1)
    tri = _bf(jnp.where((rr >= cc) & (rr // C == cc // C), 1.0, 0.0))
    hi, mid, lo = _split3(logd)
    cw = _dot(tri, hi) + _dot(tri, mid) + _dot(tri, lo)
    e_in = jnp.exp(cw)
    e_ex = jnp.exp(cw - logd)
    e_inv = jnp.exp(-cw)

    kk = k * kk_ref[...]
    k2 = k * (1.0 + (a - 1.0) * ka_ref[...])
    rk2 = r * k2 * rk_ref[...]
    ri = lax.broadcasted_iota(jnp.int32, (C, C), 0)
    ci = lax.broadcasted_iota(jnp.int32, (C, C), 1)
    strict = ri > ci
    ri2 = lax.broadcasted_iota(jnp.int32, (C, 2 * C), 0)
    ci2 = lax.broadcasted_iota(jnp.int32, (C, 2 * C), 1)
    incl2 = ri2 >= jnp.where(ci2 >= C, ci2 - C, ci2)
    eye = jnp.where(ri == ci, 1.0, 0.0)

    units = [(gi, h) for gi in range(G) for h in range(H)]

    def cut(arr, u):
        gi, h = u
        return arr[gi * C:(gi + 1) * C, h * HEAD_DIM:(h + 1) * HEAD_DIM]

    def unit_norm(u):
        kkh = cut(kk, u)
        return kkh / jnp.maximum(jnp.sqrt(jnp.sum(kkh * kkh, axis=-1, keepdims=True)), 1e-12)

    kkn = [unit_norm(u) for u in units]
    v_u = [cut(v, u) for u in units]
    at = [_bf(-kn * cut(e_ex, u)) for kn, u in zip(kkn, units)]
    rt = [_bf(cut(r, u) * cut(e_in, u)) for u in units]
    bt = [kn * cut(a, u) * cut(e_inv, u) for kn, u in zip(kkn, units)]
    kt = [cut(k2, u) * cut(e_inv, u) for u in units]
    bk16 = [_bf(jnp.concatenate([b_, k_], axis=0)) for b_, k_ in zip(bt, kt)]
    v16 = [_bf(v_) for v_ in v_u]
    l_ab = [jnp.where(strict, _dot_nt(a_, _bf(b_)), 0.0) for a_, b_ in zip(at, bt)]
    a_ak = [_bf(jnp.where(strict, _dot_nt(a_, _bf(k_)), 0.0)) for a_, k_ in zip(at, kt)]
    tinv = [eye + l_ for l_ in l_ab]
    pw = l_ab
    n = 2
    while n < C:
        pw = [_dot(_bf(p_), _bf(p_)) for p_ in pw]
        tinv = [t_ + _dot(_bf(t_), _bf(p_)) for t_, p_ in zip(tinv, pw)]
        n *= 2
    s0 = [state_ref[i] for i in range(len(units))]
    s016 = [_bf(s_) for s_ in s0]
    rhs = [_dot_nt(a_, s_) + _dot(m_, v_) for a_, s_, m_, v_ in zip(at, s016, a_ak, v16)]
    uu = [_dot(_bf(t_), _bf(x_)) for t_, x_ in zip(tinv, rhs)]
    uv16 = [_bf(jnp.concatenate([u_, v_], axis=0)) for u_, v_ in zip(uu, v_u)]
    g_r = [_bf(jnp.where(incl2, _dot_nt(r_, m_), 0.0)) for r_, m_ in zip(rt, bk16)]
    y = [_dot(m_, x_) + _dot_nt(r_, s_) for m_, x_, r_, s_ in zip(g_r, uv16, rt, s016)]
    for i, u in enumerate(units):
        state_ref[i] = (s0[i] + _dot_tn(uv16[i], bk16[i])) * cut(e_in, u)[C - 1:C, :]
    for gi in range(G):
        outs = []
        for h in range(H):
            i = gi * H + h
            u = units[i]
            sl = slice(h * HEAD_DIM, (h + 1) * HEAD_DIM)
            ym = jnp.mean(y[i], axis=-1, keepdims=True)
            yc = y[i] - ym
            yv = jnp.mean(yc * yc, axis=-1, keepdims=True)
            yn = yc * lax.rsqrt(yv + GN_EPS) * lg_ref[:, sl] + lb_ref[:, sl]
            bonus = jnp.sum(cut(rk2, u), axis=-1, keepdims=True) * v_u[i]
            outs.append((yn + bonus) * cut(g, u))
        o_ref[gi] = _bf(jnp.concatenate(outs, axis=1))


def _rwkv(B, S, zb, mu, w0, w2, a0, a2, g2, k_k, k_a, r_k, lnx_g, lnx_b):
    C = min(S, CHUNK)
    G = min(B, RWKV_ROWS)
    blk = lambda b, c: (b, c, 0)
    fix = lambda b, c: (0, 0)
    vec = lambda n: pl.BlockSpec((1, n), fix)
    return pl.pallas_call(
        _rwkv_kernel,
        out_shape=jax.ShapeDtypeStruct((B, S, RWKV_WIDTH), jnp.bfloat16),
        grid=(B // G, S // C),
        in_specs=[pl.BlockSpec((G, C, B_COLS), blk), vec(B_COLS), vec(RWKV_WIDTH),
                  pl.BlockSpec((DECAY_RANK, RWKV_WIDTH), fix), vec(RWKV_WIDTH),
                  pl.BlockSpec((ICLR_RANK, RWKV_WIDTH), fix),
                  pl.BlockSpec((GATE_RANK, RWKV_WIDTH), fix),
                  vec(RWKV_WIDTH), vec(RWKV_WIDTH), vec(RWKV_WIDTH), vec(RWKV_WIDTH),
                  vec(RWKV_WIDTH)],
        out_specs=pl.BlockSpec((G, C, RWKV_WIDTH), blk),
        scratch_shapes=[pltpu.VMEM((8 * G, B_COLS), jnp.float32),
                        pltpu.VMEM((G * RWKV_HEADS, HEAD_DIM, HEAD_DIM), jnp.float32)],
        compiler_params=_params("parallel", "arbitrary"),
        name="rwkv7",
    )(zb.reshape(B, S, B_COLS), mu, w0, w2, a0, a2, g2, k_k, k_a, r_k, lnx_g, lnx_b
      ).reshape(B * S, RWKV_WIDTH)


def _pool_kernel(zc_ref, w_ref, sc_ref, o_ref):
    S = zc_ref.shape[0]
    x = zc_ref[...]
    row = lax.broadcasted_iota(jnp.int32, (S, 1), 0)
    lane_grp = lax.broadcasted_iota(jnp.int32, (1, POOL_WIDTH), 1) // POOL_GROUP

    def lag(y, n):
        return jnp.where(row >= n, pltpu.roll(y, n, 0), 0.0)

    w2 = x + lag(x, 1)
    w4 = w2 + lag(w2, 2)
    w8 = w4 + lag(w4, 4)
    w16 = w8 + lag(w8, 8)
    wsum = jnp.where(lane_grp == 0, w2, jnp.where(lane_grp == 1, w4, jnp.where(lane_grp == 2, w8, w16)))
    win = jnp.where(lane_grp == 0, 2, jnp.where(lane_grp == 1, 4, jnp.where(lane_grp == 2, 8, 16)))
    count = jnp.minimum(row + 1, win).astype(jnp.float32)
    pooled = wsum / count - x
    o_ref[...] = _bf(_dot(_bf(pooled), w_ref[...]) * sc_ref[...])


def _pool(B, S, zc, w_bd, scale):
    return pl.pallas_call(
        _pool_kernel,
        out_shape=jax.ShapeDtypeStruct((B * S, POOL_WIDTH), jnp.bfloat16),
        grid=(B,),
        in_specs=[pl.BlockSpec((S, POOL_WIDTH), lambda b: (b, 0)),
                  pl.BlockSpec((POOL_WIDTH, POOL_WIDTH), lambda b: (0, 0)),
                  pl.BlockSpec((1, POOL_WIDTH), lambda b: (0, 0))],
        out_specs=pl.BlockSpec((S, POOL_WIDTH), lambda b: (b, 0)),
        compiler_params=_params("parallel"),
        name="pool",
    )(zc, w_bd, scale)


def _outmlp_kernel(alpha, oa_ref, ob_ref, oc_ref, x_ref, p_ref, wa_ref, wb_ref, wc_ref, g1_ref, b1_ref,
                   w1_ref, w2_ref, pg_ref, pgb_ref, pp_ref, g2_ref, b2_ref, o_ref, xb_ref, acc_ref):
    kf = pl.program_id(1)

    @pl.when(kf == 0)
    def _():
        mix = _dot(oa_ref[...], wa_ref[...]) + _dot(ob_ref[...], wb_ref[...]) + _dot(oc_ref[...], wc_ref[...])
        x = _ln(alpha * x_ref[...] + mix, g1_ref[...], b1_ref[...])
        xb = _bf(x)
        xb_ref[...] = xb
        gate = jax.nn.sigmoid(_dot(xb, pg_ref[...]) + pgb_ref[...])
        acc_ref[...] = alpha * x + gate * _dot(_bf(p_ref[...]), pp_ref[...])

    h = jnp.maximum(_dot(xb_ref[...], w1_ref[...]), 0.0)
    acc_ref[...] += _dot(_bf(h * h), w2_ref[...])

    @pl.when(kf == pl.num_programs(1) - 1)
    def _():
        o_ref[...] = _ln(acc_ref[...], g2_ref[...], b2_ref[...])


def _outmlp(alpha, oa, ob, oc, x, p, wa, wb, wc, g1, b1, w1, w2, pg, pgb, pp, g2, b2):
    T = x.shape[0]
    tm = min(T, TM_MLP)
    tf = TF_MLP
    row = lambda i, k: (i, 0)
    fix = lambda i, k: (0, 0)
    vec = pl.BlockSpec((1, D_MODEL), fix)
    return pl.pallas_call(
        functools.partial(_outmlp_kernel, alpha),
        out_shape=jax.ShapeDtypeStruct((T, D_MODEL), jnp.float32),
        grid=(T // tm, D_FF // tf),
        in_specs=[pl.BlockSpec((tm, ATT_WIDTH), row), pl.BlockSpec((tm, RWKV_WIDTH), row),
                  pl.BlockSpec((tm, POOL_WIDTH), row), pl.BlockSpec((tm, D_MODEL), row),
                  pl.BlockSpec((tm, PE_DIM), row),
                  pl.BlockSpec((ATT_WIDTH, D_MODEL), fix), pl.BlockSpec((RWKV_WIDTH, D_MODEL), fix),
                  pl.BlockSpec((POOL_WIDTH, D_MODEL), fix), vec, vec,
                  pl.BlockSpec((D_MODEL, tf), lambda i, k: (0, k)),
                  pl.BlockSpec((tf, D_MODEL), lambda i, k: (k, 0)),
                  pl.BlockSpec((D_MODEL, D_MODEL), fix), vec,
                  pl.BlockSpec((PE_DIM, D_MODEL), fix), vec, vec],
        out_specs=pl.BlockSpec((tm, D_MODEL), row),
        scratch_shapes=[pltpu.VMEM((tm, D_MODEL), jnp.bfloat16),
                        pltpu.VMEM((tm, D_MODEL), jnp.float32)],
        compiler_params=_params("parallel", "arbitrary"),
        name="outproj_mlp",
    )(oa, ob, oc, x, p, wa, wb, wc, g1, b1, w1, w2, pg, pgb, pp, g2, b2)


def _rot_partner(w):
    half = HEAD_DIM // 2
    return jnp.concatenate([-w[..., half:], w[..., :half]], axis=-1)


def _per_head_partner(w, heads):
    r = w.reshape(w.shape[0], heads, HEAD_DIM)
    return _rot_partner(r).reshape(w.shape)


def _arrange_w_in(w):
    cq = w[:, 0:256]
    ka = w[:, 256:320]
    va = w[:, 320:384]
    ik = w[:, 384:448]
    iw = w[:, 448:456]
    rest = w[:, 456:]
    pad = jnp.zeros((w.shape[0], LANES - HEAD_DIM - IDX_HEADS), w.dtype)
    return jnp.concatenate([cq, ka, ik, _rot_partner(ka), _rot_partner(ik), va, iw, pad, rest], axis=1)


def kernel(x, p, positions, ln_emb_g, ln_emb_b, w_in, w_uq, w_uqi, attn_norm_g, rwkv_mu, rwkv_w0, rwkv_w2, rwkv_a0, rwkv_a2, rwkv_g2, rwkv_k_k, rwkv_k_a, rwkv_r_k, rwkv_lnx_g, rwkv_lnx_b, pool_w, pool_scale, w_out, ln1_g, ln1_b, mlp_w1, mlp_w2, pe_proj, pe_gate, pe_gate_b, ln2_g, ln2_b):
    B, S, _ = x.shape
    T = B * S
    depth = w_in.shape[0]
    alpha = (2 * depth) ** 0.25
    bf = lambda a: a.astype(jnp.bfloat16)
    rowv = lambda a: a.reshape(1, -1)

    cos, sin = _rope_tables(positions.astype(jnp.float32).reshape(T, 1))
    xs = x.reshape(T, D_MODEL)
    for i in range(depth):
        win = bf(_arrange_w_in(w_in[i]))
        wq = w_uq[i] * (HEAD_DIM ** -0.5)
        wi = w_uqi[i] * (IDX_DIM ** -0.5)
        res = _inproj(xs, rowv(ln_emb_g), rowv(ln_emb_b), cos, sin, win,
                      bf(wq), bf(_per_head_partner(wq, ATT_HEADS)),
                      bf(wi), bf(_per_head_partner(wi, IDX_HEADS)), apply_ln=(i == 0))
        if i == 0:
            xs, res = res[0], res[1:]
        q, iq, iw, kr, ikr, va, zb, zc = res
        o_a = _attention(B, S, q, iq, iw, kr, ikr, va, rowv(attn_norm_g[i]))
        o_b = _rwkv(B, S, zb, rowv(rwkv_mu[i]), rowv(rwkv_w0[i]), bf(rwkv_w2[i]), rowv(rwkv_a0[i]),
                    bf(rwkv_a2[i]), bf(rwkv_g2[i]), rowv(rwkv_k_k[i]), rowv(rwkv_k_a[i]),
                    rowv(rwkv_r_k[i]), rowv(rwkv_lnx_g[i]), rowv(rwkv_lnx_b[i]))
        w_bd = jax.scipy.linalg.block_diag(*[pool_w[i, gi] for gi in range(len(POOL_WINDOWS))])
        o_c = _pool(B, S, zc, bf(w_bd), rowv(pool_scale[i]))
        wo = bf(w_out[i])
        xs = _outmlp(alpha, o_a, o_b, o_c, xs, p[i].reshape(T, PE_DIM), wo[:ATT_WIDTH],
                     wo[ATT_WIDTH:ATT_WIDTH + RWKV_WIDTH], wo[ATT_WIDTH + RWKV_WIDTH:], rowv(ln1_g[i]),
                     rowv(ln1_b[i]), bf(mlp_w1[i]), bf(mlp_w2[i]), bf(pe_gate[i]), rowv(pe_gate_b[i]),
                     bf(pe_proj[i]), rowv(ln2_g[i]), rowv(ln2_b[i]))
    return xs.reshape(B, S, D_MODEL)
```

```python
import functools

import numpy as np
import jax
import jax.numpy as jnp
from jax import lax
from jax.experimental import pallas as pl
from jax.experimental.pallas import tpu as pltpu

D_MODEL = 1024
PE_DIM = 256
HEAD_DIM = 64
ROPE_THETA = 10000.0
ATT_WIDTH = 384
ATT_HEADS = 6
Q_RANK = 256
IDX_HEADS = 8
IDX_DIM = 64
TOPK_MAX = 256
RWKV_WIDTH = 384
RWKV_HEADS = 6
DECAY_RANK = 64
ICLR_RANK = 64
GATE_RANK = 128
GN_EPS = 64e-5
POOL_WINDOWS = (2, 4, 8, 16)
POOL_WIDTH = 256
POOL_GROUP = 64
B_COLS = 3 * RWKV_WIDTH + DECAY_RANK + ICLR_RANK + GATE_RANK
D_FF = 4 * D_MODEL
LN_EPS = 1e-5
RMS_EPS = 1e-6

COL_CQ = 0
COL_KIK = 256
COL_KIK_P = 384
COL_VIW = 512
COL_B = 640
COL_C = COL_B + B_COLS
N_IN_P = COL_C + POOL_WIDTH

LANES = 128
INT_MIN = -2 ** 31
LOG2E = 1.4426950408889634
VMEM_LIMIT = 56 * 1024 * 1024

TM_PROJ = 512
TQ = 256
ATT_KEY_CHUNK = 512
CHUNK = 64
RWKV_ROWS = 4
TM_MLP = 512
TF_MLP = 1024

_NT = (((1,), (1,)), ((), ()))
_TN = (((0,), (0,)), ((), ()))


def _params(*sem):
    return pltpu.CompilerParams(dimension_semantics=sem, vmem_limit_bytes=VMEM_LIMIT)


def _ln(x, g, b):
    mu = jnp.mean(x, axis=-1, keepdims=True)
    xc = x - mu
    var = jnp.mean(xc * xc, axis=-1, keepdims=True)
    return xc * lax.rsqrt(var + LN_EPS) * g + b


def _bf(x):
    return x.astype(jnp.bfloat16)


def _dot(a, b):
    return jnp.dot(a, b, preferred_element_type=jnp.float32)


def _dot_nt(a, b):
    return lax.dot_general(a, b, _NT, preferred_element_type=jnp.float32)


def _dot_tn(a, b):
    return lax.dot_general(a, b, _TN, preferred_element_type=jnp.float32)


def _rope_kernel(pos_ref, inv_ref, cos_ref, sin_ref):
    for r in range(pos_ref.shape[0]):
        col = jnp.broadcast_to(pos_ref[r:r + 1, :], (LANES, LANES)).T
        ang = col * inv_ref[...]
        cos_ref[r * LANES:(r + 1) * LANES, :] = jnp.cos(ang)
        sin_ref[r * LANES:(r + 1) * LANES, :] = jnp.sin(ang)


def _rope_tables(pos_f32):
    T = pos_f32.shape[0] * LANES
    rows = min(pos_f32.shape[0], 16)
    inv = ROPE_THETA ** (-np.arange(0, HEAD_DIM, 2, dtype=np.float32) / HEAD_DIM)
    inv128 = jnp.asarray(np.tile(inv, LANES // (HEAD_DIM // 2))[None, :], jnp.float32)
    return pl.pallas_call(
        _rope_kernel,
        out_shape=(jax.ShapeDtypeStruct((T, LANES), jnp.float32),) * 2,
        grid=(pos_f32.shape[0] // rows,),
        in_specs=[pl.BlockSpec((rows, LANES), lambda i: (i, 0)),
                  pl.BlockSpec((1, LANES), lambda i: (0, 0))],
        out_specs=(pl.BlockSpec((rows * LANES, LANES), lambda i: (i, 0)),) * 2,
        compiler_params=_params("parallel"),
        name="rope_tables",
    )(pos_f32, inv128)


def _inproj_kernel(apply_ln, x_ref, g_ref, b_ref, cos_ref, sin_ref, win_ref, wq_ref, wqp_ref,
                   wi_ref, wip_ref, *out_refs):
    if apply_ln:
        xn_ref, q_ref, iq_ref, iw_ref, kr_ref, ikr_ref, va_ref, zb_ref, zc_ref = out_refs
        x = _ln(x_ref[...], g_ref[...], b_ref[...])
        xn_ref[...] = x
    else:
        q_ref, iq_ref, iw_ref, kr_ref, ikr_ref, va_ref, zb_ref, zc_ref = out_refs
        x = x_ref[...]
    z = _dot(_bf(x), win_ref[...])
    cos = cos_ref[...]
    sin = sin_ref[...]
    cq = _bf(z[:, COL_CQ:COL_CQ + Q_RANK])
    cos3 = jnp.concatenate([cos] * 3, axis=1)
    sin3 = jnp.concatenate([sin] * 3, axis=1)
    q = _bf((_dot(cq, wq_ref[...]) * cos3 + _dot(cq, wqp_ref[...]) * sin3) * LOG2E)
    for j in range(q_ref.shape[0]):
        q_ref[j] = q[:, j * LANES:(j + 1) * LANES]
    cos4 = jnp.concatenate([cos] * 4, axis=1)
    sin4 = jnp.concatenate([sin] * 4, axis=1)
    iq = _bf(_dot(cq, wi_ref[...]) * cos4 + _dot(cq, wip_ref[...]) * sin4)
    for j in range(iq_ref.shape[0]):
        iq_ref[j] = iq[:, j * LANES:(j + 1) * LANES]
    kik = z[:, COL_KIK:COL_KIK + LANES] * cos + z[:, COL_KIK_P:COL_KIK_P + LANES] * sin
    kr_ref[...] = _bf(kik[:, :HEAD_DIM])
    ikr_ref[...] = _bf(kik[:, HEAD_DIM:])
    viw = z[:, COL_VIW:COL_VIW + LANES]
    va_ref[...] = _bf(viw[:, :HEAD_DIM])
    iw_ref[...] = viw * (IDX_HEADS ** -0.5)
    zb_ref[...] = z[:, COL_B:COL_B + B_COLS]
    zc_ref[...] = z[:, COL_C:COL_C + POOL_WIDTH]


def _inproj(x, g, b, cos, sin, win, wq, wqp, wi, wip, apply_ln):
    T = x.shape[0]
    tm = min(T, TM_PROJ)
    row = lambda i: (i, 0)
    fix = lambda i: (0, 0)
    bf16, f32 = jnp.bfloat16, jnp.float32
    outs = [(LANES, f32), (HEAD_DIM, bf16), (HEAD_DIM, bf16), (HEAD_DIM, bf16), (B_COLS, f32),
            (POOL_WIDTH, f32)]
    pairs = [ATT_WIDTH // LANES, IDX_HEADS * IDX_DIM // LANES]
    shapes = [jax.ShapeDtypeStruct((n, T, LANES), bf16) for n in pairs]
    shapes += [jax.ShapeDtypeStruct((T, n), dt) for n, dt in outs]
    specs = [pl.BlockSpec((n, tm, LANES), lambda i: (0, i, 0)) for n in pairs]
    specs += [pl.BlockSpec((tm, n), row) for n, _ in outs]
    if apply_ln:
        shapes = [jax.ShapeDtypeStruct((T, D_MODEL), f32)] + shapes
        specs = [pl.BlockSpec((tm, D_MODEL), row)] + specs
    return pl.pallas_call(
        functools.partial(_inproj_kernel, apply_ln),
        out_shape=tuple(shapes),
        grid=(T // tm,),
        in_specs=[pl.BlockSpec((tm, D_MODEL), row),
                  pl.BlockSpec((1, D_MODEL), fix), pl.BlockSpec((1, D_MODEL), fix),
                  pl.BlockSpec((tm, LANES), row), pl.BlockSpec((tm, LANES), row),
                  pl.BlockSpec((D_MODEL, N_IN_P), fix),
                  pl.BlockSpec((Q_RANK, ATT_WIDTH), fix), pl.BlockSpec((Q_RANK, ATT_WIDTH), fix),
                  pl.BlockSpec((Q_RANK, IDX_HEADS * IDX_DIM), fix),
                  pl.BlockSpec((Q_RANK, IDX_HEADS * IDX_DIM), fix)],
        out_specs=tuple(specs),
        compiler_params=_params("parallel"),
        name="inproj",
    )(x, g, b, cos, sin, win, wq, wqp, wi, wip)


def _attn_block(kend, search, topk, q_ref, iq_ref, iw_ref, kr_ref, ikr_ref, va_ref, g_ref, o_ref,
                key_ref, bias_ref, oacc_ref, hi_ref, lo_ref):
    tq = q_ref.shape[1]
    row_t = pl.program_id(1) * tq + lax.broadcasted_iota(jnp.int32, (tq, 1), 0)
    col = lax.broadcasted_iota(jnp.int32, (1, kend), 1)
    causal = col <= row_t

    if not search:
        bias_ref[:, :kend] = jnp.where(causal, 0.0, -jnp.inf)
    else:
        lane = lax.broadcasted_iota(jnp.int32, (1, LANES), 1)
        bias_ref[:, :kend] = jnp.zeros((tq, kend), jnp.float32)

        def idx_pair(j, carry):
            iqp = iq_ref[j]
            ik = ikr_ref[0:kend, :]
            acc = bias_ref[:, :kend]
            lgs = [_dot_nt(iqp[:, hh * IDX_DIM:(hh + 1) * IDX_DIM], ik) for hh in range(2)]
            for hh, lg in enumerate(lgs):
                w_h = jnp.sum(jnp.where(lane == HEAD_DIM + 2 * j + hh, iw_ref[...], 0.0), axis=1, keepdims=True)
                acc = acc + jnp.maximum(lg, 0.0) * w_h
            bias_ref[:, :kend] = acc
            return carry

        lax.fori_loop(0, iq_ref.shape[0], idx_pair, 0)
        score = bias_ref[:, :kend] + 0.0
        bits = pltpu.bitcast(score, jnp.int32)
        key = bits ^ ((bits >> 31) & 0x7FFFFFFF)
        key_ref[:, :kend] = jnp.where(causal, key, INT_MIN)

        i16 = jnp.int16
        one16 = jnp.ones((tq, LANES), i16)
        zero16 = jnp.zeros((tq, LANES), i16)

        def count16(ref, cand):
            c16 = jnp.broadcast_to(cand, (tq, LANES)).astype(i16)
            acc = zero16
            for i in range(kend // LANES):
                acc = acc + jnp.where(ref[:, i * LANES:(i + 1) * LANES] >= c16, one16, zero16)
            return jnp.sum(acc.astype(jnp.float32), axis=1, keepdims=True)

        def kth_largest16(ref, k):
            t = jnp.where(count16(ref, jnp.zeros((tq, 1), jnp.int32)) >= k, 0, -32768).astype(jnp.int32)

            def bit_step(i, t):
                cand = t | (jnp.int32(1) << (14 - i))
                return jnp.where(count16(ref, cand) >= k, cand, t)

            return lax.fori_loop(0, 15, bit_step, t)

        keyv = key_ref[:, :kend]
        hi_ref[:, :kend] = (keyv >> 16).astype(i16)
        tau_hi = kth_largest16(hi_ref, jnp.full((tq, 1), float(topk), jnp.float32))
        above = jnp.where(tau_hi < 32767, count16(hi_ref, jnp.minimum(tau_hi + 1, 32767)), 0.0)
        lo = ((keyv & 0xFFFF) - 32768).astype(i16)
        same_hi = hi_ref[:, :kend] == jnp.broadcast_to(tau_hi, (tq, kend)).astype(i16)
        lo_ref[:, :kend] = jnp.where(same_hi, lo, -32768)
        tau_lo = kth_largest16(lo_ref, topk - above)
        tau = (tau_hi << 16) | (tau_lo + 32768)
        tau = jnp.maximum(tau, INT_MIN + 1)
        ge = key_ref[:, :kend] >= tau
        n_ge = jnp.sum(jnp.where(ge, 1.0, 0.0), axis=1, keepdims=True)
        bias_ref[:, :kend] = jnp.where(ge, 0.0, -jnp.inf)

        @pl.when(jnp.max(n_ge) > topk)
        def _():
            need = topk - jnp.sum(jnp.where(key_ref[:, :kend] > tau, 1.0, 0.0), axis=1, keepdims=True)
            ri = lax.broadcasted_iota(jnp.int32, (LANES, LANES), 0)
            ci = lax.broadcasted_iota(jnp.int32, (LANES, LANES), 1)
            upper = _bf(jnp.where(ri <= ci, 1.0, 0.0))
            before = jnp.zeros((tq, 1), jnp.float32)
            for c in range(kend // LANES):
                keyc = key_ref[:, c * LANES:(c + 1) * LANES]
                eq = keyc == tau
                rank = _dot(_bf(jnp.where(eq, 1.0, 0.0)), upper) + before
                keep = (keyc > tau) | (eq & (rank <= need))
                bias_ref[:, c * LANES:(c + 1) * LANES] = jnp.where(keep, 0.0, -jnp.inf)
                before = rank[:, LANES - 1:LANES]

    kc = min(kend, ATT_KEY_CHUNK)
    lane_tiles = lambda a: [a[:, i * LANES:(i + 1) * LANES] for i in range(a.shape[1] // LANES)]

    def att_pair(j, carry):
        qp = q_ref[j]
        chunks = range(0, kend, kc)
        s_h = [[_dot_nt(qp[:, hh * HEAD_DIM:(hh + 1) * HEAD_DIM], kr_ref[c:c + kc, :]) + bias_ref[:, c:c + kc]
                for c in chunks] for hh in range(2)]
        outs = []
        for s_c in s_h:
            m = jnp.max(functools.reduce(jnp.maximum, [t for s in s_c for t in lane_tiles(s)]),
                        axis=1, keepdims=True)
            p_c = [jnp.exp2(s - m) for s in s_c]
            l = jnp.sum(sum(t for p in p_c for t in lane_tiles(p)), axis=1, keepdims=True)
            acc = sum(_dot(_bf(p), va_ref[c:c + kc, :]) for p, c in zip(p_c, chunks))
            outs.append(acc / l)
        oacc_ref[j] = jnp.concatenate(outs, axis=1)
        return carry

    lax.fori_loop(0, q_ref.shape[0], att_pair, 0)
    o = jnp.concatenate([oacc_ref[j] for j in range(q_ref.shape[0])], axis=1)
    o = o * lax.rsqrt(jnp.mean(o * o, axis=-1, keepdims=True) + RMS_EPS) * g_ref[...]
    o_ref[...] = _bf(o)


def _key_extents(S, tq, topk):
    out = []
    for v in range(S // tq):
        need = (v + 1) * tq
        if need <= topk:
            out.append((need, False))
        else:
            out.append((min(S, -(-need // (2 * tq)) * 2 * tq), True))
    return out


def _attn_kernel(S, *refs):
    tq = refs[0].shape[1]
    topk = min(TOPK_MAX, S // 4)
    qi = pl.program_id(1)
    extents = _key_extents(S, tq, topk)
    for ext in sorted(set(extents)):
        blocks = [v for v, e in enumerate(extents) if e == ext]
        pl.when((qi >= blocks[0]) & (qi <= blocks[-1]))(functools.partial(_attn_block, *ext, topk, *refs))


def _attention(B, S, q, iq, iw, kr, ikr, va, g):
    tq = min(S, TQ)
    nq = S // tq
    qrow = lambda b, i: (b * nq + i, 0)
    seq = lambda b, i: (b, 0)
    fix = lambda b, i: (0, 0)
    return pl.pallas_call(
        functools.partial(_attn_kernel, S),
        out_shape=jax.ShapeDtypeStruct((B * S, ATT_WIDTH), jnp.bfloat16),
        grid=(B, nq),
        in_specs=[pl.BlockSpec((q.shape[0], tq, LANES), lambda b, i: (0, b * nq + i, 0)),
                  pl.BlockSpec((iq.shape[0], tq, LANES), lambda b, i: (0, b * nq + i, 0)),
                  pl.BlockSpec((tq, LANES), qrow),
                  pl.BlockSpec((S, HEAD_DIM), seq), pl.BlockSpec((S, HEAD_DIM), seq),
                  pl.BlockSpec((S, HEAD_DIM), seq),
                  pl.BlockSpec((1, ATT_WIDTH), fix)],
        out_specs=pl.BlockSpec((tq, ATT_WIDTH), qrow),
        scratch_shapes=[pltpu.VMEM((tq, S), jnp.int32), pltpu.VMEM((tq, S), jnp.float32),
                        pltpu.VMEM((q.shape[0], tq, LANES), jnp.float32),
                        pltpu.VMEM((tq, S), jnp.int16), pltpu.VMEM((tq, S), jnp.int16)],
        compiler_params=_params("parallel", "parallel"),
        name="dsa_attention",
    )(q, iq, iw, kr, ikr, va, g)


def _split3(x):
    hi = _bf(x)
    r1 = x - hi.astype(jnp.float32)
    mid = _bf(r1)
    lo = _bf(r1 - mid.astype(jnp.float32))
    return hi, mid, lo


def _rwkv_kernel(zb_ref, mu_ref, w0_ref, w2_ref, a0_ref, a2_ref, g2_ref, kk_ref, ka_ref, rk_ref,
                 lg_ref, lb_ref, o_ref, prev_ref, state_ref):
    G, C, _ = zb_ref.shape
    W = RWKV_WIDTH
    R = G * C
    H = RWKV_HEADS

    @pl.when(pl.program_id(1) == 0)
    def _():
        prev_ref[...] = jnp.zeros_like(prev_ref)
        state_ref[...] = jnp.zeros_like(state_ref)

    x = zb_ref[...].reshape(R, B_COLS)
    rowi = lax.broadcasted_iota(jnp.int32, (R, 1), 0)
    shifted = pltpu.roll(x, 1, 0)
    for gi in range(G):
        shifted = jnp.where(rowi == gi * C, prev_ref[8 * gi:8 * gi + 1, :], shifted)
        prev_ref[8 * gi:8 * gi + 1, :] = x[gi * C + C - 1:gi * C + C, :]
    xm = x + (shifted - x) * mu_ref[...]
    r = xm[:, 0:W]
    k = xm[:, W:2 * W]
    v = xm[:, 2 * W:3 * W]
    wl = xm[:, 3 * W:3 * W + DECAY_RANK]
    al = xm[:, 3 * W + DECAY_RANK:3 * W + DECAY_RANK + ICLR_RANK]
    gl = xm[:, 3 * W + DECAY_RANK + ICLR_RANK:]

    wpre = w0_ref[...] + _dot(_bf(jnp.tanh(wl)), w2_ref[...])
    nw = -wpre
    w = -(jnp.maximum(nw, 0.0) + jnp.log(1.0 + jnp.exp(-jnp.abs(nw)))) - 0.5
    logd = -jnp.exp(w)
    a = jax.nn.sigmoid(a0_ref[...] + _dot(_bf(al), a2_ref[...]))
    g = _dot(_bf(jax.nn.sigmoid(gl)), g2_ref[...])

    rr = lax.broadcasted_iota(jnp.int32, (R, R), 0)
    cc = lax.broadcasted_iota(jnp.int32, (R, R), 1)
    tri = _bf(jnp.where((rr >= cc) & (rr // C == cc // C), 1.0, 0.0))
    hi, mid, lo = _split3(logd)
    cw = _dot(tri, hi) + _dot(tri, mid) + _dot(tri, lo)
    e_in = jnp.exp(cw)
    e_ex = jnp.exp(cw - logd)
    e_inv = jnp.exp(-cw)

    kk = k * kk_ref[...]
    k2 = k * (1.0 + (a - 1.0) * ka_ref[...])
    rk2 = r * k2 * rk_ref[...]
    ri = lax.broadcasted_iota(jnp.int32, (C, C), 0)
    ci = lax.broadcasted_iota(jnp.int32, (C, C), 1)
    strict = ri > ci
    ri2 = lax.broadcasted_iota(jnp.int32, (C, 2 * C), 0)
    ci2 = lax.broadcasted_iota(jnp.int32, (C, 2 * C), 1)
    incl2 = ri2 >= jnp.where(ci2 >= C, ci2 - C, ci2)
    eye = jnp.where(ri == ci, 1.0, 0.0)

    units = [(gi, h) for gi in range(G) for h in range(H)]

    def cut(arr, u):
        gi, h = u
        return arr[gi * C:(gi + 1) * C, h * HEAD_DIM:(h + 1) * HEAD_DIM]

    def unit_norm(u):
        kkh = cut(kk, u)
        return kkh / jnp.maximum(jnp.sqrt(jnp.sum(kkh * kkh, axis=-1, keepdims=True)), 1e-12)

    kkn = [unit_norm(u) for u in units]
    v_u = [cut(v, u) for u in units]
    at = [_bf(-kn * cut(e_ex, u)) for kn, u in zip(kkn, units)]
    rt = [_bf(cut(r, u) * cut(e_in, u)) for u in units]
    bt = [kn * cut(a, u) * cut(e_inv, u) for kn, u in zip(kkn, units)]
    kt = [cut(k2, u) * cut(e_inv, u) for u in units]
    bk16 = [_bf(jnp.concatenate([b_, k_], axis=0)) for b_, k_ in zip(bt, kt)]
    v16 = [_bf(v_) for v_ in v_u]
    g_a = [_dot_nt(a_, m_) for a_, m_ in zip(at, bk16)]
    l_ab = [jnp.where(strict, g_[:, :C], 0.0) for g_ in g_a]
    a_ak = [_bf(jnp.where(strict, g_[:, C:], 0.0)) for g_ in g_a]
    tinv = [eye + l_ for l_ in l_ab]
    pw = l_ab
    n = 2
    while n < C:
        pw = [_dot(_bf(p_), _bf(p_)) for p_ in pw]
        tinv = [t_ + _dot(_bf(t_), _bf(p_)) for t_, p_ in zip(tinv, pw)]
        n *= 2
    s0 = [state_ref[i] for i in range(len(units))]
    s016 = [_bf(s_) for s_ in s0]
    rhs = [_dot_nt(a_, s_) + _dot(m_, v_) for a_, s_, m_, v_ in zip(at, s016, a_ak, v16)]
    uu = [_dot(_bf(t_), _bf(x_)) for t_, x_ in zip(tinv, rhs)]
    uv16 = [_bf(jnp.concatenate([u_, v_], axis=0)) for u_, v_ in zip(uu, v_u)]
    g_r = [_bf(jnp.where(incl2, _dot_nt(r_, m_), 0.0)) for r_, m_ in zip(rt, bk16)]
    y = [_dot(m_, x_) + _dot_nt(r_, s_) for m_, x_, r_, s_ in zip(g_r, uv16, rt, s016)]
    for i, u in enumerate(units):
        state_ref[i] = (s0[i] + _dot_tn(uv16[i], bk16[i])) * cut(e_in, u)[C - 1:C, :]
    for gi in range(G):
        outs = []
        for h in range(H):
            i = gi * H + h
            u = units[i]
            sl = slice(h * HEAD_DIM, (h + 1) * HEAD_DIM)
            ym = jnp.mean(y[i], axis=-1, keepdims=True)
            yc = y[i] - ym
            yv = jnp.mean(yc * yc, axis=-1, keepdims=True)
            yn = yc * lax.rsqrt(yv + GN_EPS) * lg_ref[:, sl] + lb_ref[:, sl]
            bonus = jnp.sum(cut(rk2, u), axis=-1, keepdims=True) * v_u[i]
            outs.append((yn + bonus) * cut(g, u))
        o_ref[gi] = _bf(jnp.concatenate(outs, axis=1))


def _rwkv(B, S, zb, mu, w0, w2, a0, a2, g2, k_k, k_a, r_k, lnx_g, lnx_b):
    C = min(S, CHUNK)
    G = min(B, RWKV_ROWS)
    blk = lambda b, c: (b, c, 0)
    fix = lambda b, c: (0, 0)
    vec = lambda n: pl.BlockSpec((1, n), fix)
    return pl.pallas_call(
        _rwkv_kernel,
        out_shape=jax.ShapeDtypeStruct((B, S, RWKV_WIDTH), jnp.bfloat16),
        grid=(B // G, S // C),
        in_specs=[pl.BlockSpec((G, C, B_COLS), blk), vec(B_COLS), vec(RWKV_WIDTH),
                  pl.BlockSpec((DECAY_RANK, RWKV_WIDTH), fix), vec(RWKV_WIDTH),
                  pl.BlockSpec((ICLR_RANK, RWKV_WIDTH), fix),
                  pl.BlockSpec((GATE_RANK, RWKV_WIDTH), fix),
                  vec(RWKV_WIDTH), vec(RWKV_WIDTH), vec(RWKV_WIDTH), vec(RWKV_WIDTH),
                  vec(RWKV_WIDTH)],
        out_specs=pl.BlockSpec((G, C, RWKV_WIDTH), blk),
        scratch_shapes=[pltpu.VMEM((8 * G, B_COLS), jnp.float32),
                        pltpu.VMEM((G * RWKV_HEADS, HEAD_DIM, HEAD_DIM), jnp.float32)],
        compiler_params=_params("parallel", "arbitrary"),
        name="rwkv7",
    )(zb.reshape(B, S, B_COLS), mu, w0, w2, a0, a2, g2, k_k, k_a, r_k, lnx_g, lnx_b
      ).reshape(B * S, RWKV_WIDTH)


def _pool_kernel(zc_ref, w_ref, sc_ref, o_ref):
    S = zc_ref.shape[0]
    x = zc_ref[...]
    row = lax.broadcasted_iota(jnp.int32, (S, 1), 0)
    lane_grp = lax.broadcasted_iota(jnp.int32, (1, POOL_WIDTH), 1) // POOL_GROUP

    def lag(y, n):
        return jnp.where(row >= n, pltpu.roll(y, n, 0), 0.0)

    w2 = x + lag(x, 1)
    w4 = w2 + lag(w2, 2)
    w8 = w4 + lag(w4, 4)
    w16 = w8 + lag(w8, 8)
    wsum = jnp.where(lane_grp == 0, w2, jnp.where(lane_grp == 1, w4, jnp.where(lane_grp == 2, w8, w16)))
    win = jnp.where(lane_grp == 0, 2, jnp.where(lane_grp == 1, 4, jnp.where(lane_grp == 2, 8, 16)))
    count = jnp.minimum(row + 1, win).astype(jnp.float32)
    pooled = wsum / count - x
    o_ref[...] = _bf(_dot(_bf(pooled), w_ref[...]) * sc_ref[...])


def _pool(B, S, zc, w_bd, scale):
    return pl.pallas_call(
        _pool_kernel,
        out_shape=jax.ShapeDtypeStruct((B * S, POOL_WIDTH), jnp.bfloat16),
        grid=(B,),
        in_specs=[pl.BlockSpec((S, POOL_WIDTH), lambda b: (b, 0)),
                  pl.BlockSpec((POOL_WIDTH, POOL_WIDTH), lambda b: (0, 0)),
                  pl.BlockSpec((1, POOL_WIDTH), lambda b: (0, 0))],
        out_specs=pl.BlockSpec((S, POOL_WIDTH), lambda b: (b, 0)),
        compiler_params=_params("parallel"),
        name="pool",
    )(zc, w_bd, scale)


def _outmlp_kernel(alpha, oa_ref, ob_ref, oc_ref, x_ref, p_ref, wa_ref, wb_ref, wc_ref, g1_ref, b1_ref,
                   w1_ref, w2_ref, pg_ref, pgb_ref, pp_ref, g2_ref, b2_ref, o_ref, xb_ref, acc_ref):
    kf = pl.program_id(1)

    @pl.when(kf == 0)
    def _():
        mix = _dot(oa_ref[...], wa_ref[...]) + _dot(ob_ref[...], wb_ref[...]) + _dot(oc_ref[...], wc_ref[...])
        x = _ln(alpha * x_ref[...] + mix, g1_ref[...], b1_ref[...])
        xb = _bf(x)
        xb_ref[...] = xb
        gate = jax.nn.sigmoid(_dot(xb, pg_ref[...]) + pgb_ref[...])
        acc_ref[...] = alpha * x + gate * _dot(_bf(p_ref[...]), pp_ref[...])

    h = jnp.maximum(_dot(xb_ref[...], w1_ref[...]), 0.0)
    acc_ref[...] += _dot(_bf(h * h), w2_ref[...])

    @pl.when(kf == pl.num_programs(1) - 1)
    def _():
        o_ref[...] = _ln(acc_ref[...], g2_ref[...], b2_ref[...])


def _outmlp(alpha, oa, ob, oc, x, p, wa, wb, wc, g1, b1, w1, w2, pg, pgb, pp, g2, b2):
    T = x.shape[0]
    tm = min(T, TM_MLP)
    tf = TF_MLP
    row = lambda i, k: (i, 0)
    fix = lambda i, k: (0, 0)
    vec = pl.BlockSpec((1, D_MODEL), fix)
    return pl.pallas_call(
        functools.partial(_outmlp_kernel, alpha),
        out_shape=jax.ShapeDtypeStruct((T, D_MODEL), jnp.float32),
        grid=(T // tm, D_FF // tf),
        in_specs=[pl.BlockSpec((tm, ATT_WIDTH), row), pl.BlockSpec((tm, RWKV_WIDTH), row),
                  pl.BlockSpec((tm, POOL_WIDTH), row), pl.BlockSpec((tm, D_MODEL), row),
                  pl.BlockSpec((tm, PE_DIM), row),
                  pl.BlockSpec((ATT_WIDTH, D_MODEL), fix), pl.BlockSpec((RWKV_WIDTH, D_MODEL), fix),
                  pl.BlockSpec((POOL_WIDTH, D_MODEL), fix), vec, vec,
                  pl.BlockSpec((D_MODEL, tf), lambda i, k: (0, k)),
                  pl.BlockSpec((tf, D_MODEL), lambda i, k: (k, 0)),
                  pl.BlockSpec((D_MODEL, D_MODEL), fix), vec,
                  pl.BlockSpec((PE_DIM, D_MODEL), fix), vec, vec],
        out_specs=pl.BlockSpec((tm, D_MODEL), row),
        scratch_shapes=[pltpu.VMEM((tm, D_MODEL), jnp.bfloat16),
                        pltpu.VMEM((tm, D_MODEL), jnp.float32)],
        compiler_params=_params("parallel", "arbitrary"),
        name="outproj_mlp",
    )(oa, ob, oc, x, p, wa, wb, wc, g1, b1, w1, w2, pg, pgb, pp, g2, b2)


def _rot_partner(w):
    half = HEAD_DIM // 2
    return jnp.concatenate([-w[..., half:], w[..., :half]], axis=-1)


def _per_head_partner(w, heads):
    r = w.reshape(w.shape[0], heads, HEAD_DIM)
    return _rot_partner(r).reshape(w.shape)


def _arrange_w_in(w):
    cq = w[:, 0:256]
    ka = w[:, 256:320]
    va = w[:, 320:384]
    ik = w[:, 384:448]
    iw = w[:, 448:456]
    rest = w[:, 456:]
    pad = jnp.zeros((w.shape[0], LANES - HEAD_DIM - IDX_HEADS), w.dtype)
    return jnp.concatenate([cq, ka, ik, _rot_partner(ka), _rot_partner(ik), va, iw, pad, rest], axis=1)


def kernel(x, p, positions, ln_emb_g, ln_emb_b, w_in, w_uq, w_uqi, attn_norm_g, rwkv_mu, rwkv_w0, rwkv_w2, rwkv_a0, rwkv_a2, rwkv_g2, rwkv_k_k, rwkv_k_a, rwkv_r_k, rwkv_lnx_g, rwkv_lnx_b, pool_w, pool_scale, w_out, ln1_g, ln1_b, mlp_w1, mlp_w2, pe_proj, pe_gate, pe_gate_b, ln2_g, ln2_b):
    B, S, _ = x.shape
    T = B * S
    depth = w_in.shape[0]
    alpha = (2 * depth) ** 0.25
    bf = lambda a: a.astype(jnp.bfloat16)
    rowv = lambda a: a.reshape(1, -1)

    cos, sin = _rope_tables(positions.astype(jnp.float32).reshape(T // LANES, LANES))
    xs = x.reshape(T, D_MODEL)
    for i in range(depth):
        win = bf(_arrange_w_in(w_in[i]))
        wq = w_uq[i] * (HEAD_DIM ** -0.5)
        wi = w_uqi[i] * (IDX_DIM ** -0.5)
        res = _inproj(xs, rowv(ln_emb_g), rowv(ln_emb_b), cos, sin, win,
                      bf(wq), bf(_per_head_partner(wq, ATT_HEADS)),
                      bf(wi), bf(_per_head_partner(wi, IDX_HEADS)), apply_ln=(i == 0))
        if i == 0:
            xs, res = res[0], res[1:]
        q, iq, iw, kr, ikr, va, zb, zc = res
        o_a = _attention(B, S, q, iq, iw, kr, ikr, va, rowv(attn_norm_g[i]))
        o_b = _rwkv(B, S, zb, rowv(rwkv_mu[i]), rowv(rwkv_w0[i]), bf(rwkv_w2[i]), rowv(rwkv_a0[i]),
                    bf(rwkv_a2[i]), bf(rwkv_g2[i]), rowv(rwkv_k_k[i]), rowv(rwkv_k_a[i]),
                    rowv(rwkv_r_k[i]), rowv(rwkv_lnx_g[i]), rowv(rwkv_lnx_b[i]))
        w_bd = jax.scipy.linalg.block_diag(*[pool_w[i, gi] for gi in range(len(POOL_WINDOWS))])
        o_c = _pool(B, S, zc, bf(w_bd), rowv(pool_scale[i]))
        wo = bf(w_out[i])
        xs = _outmlp(alpha, o_a, o_b, o_c, xs, p[i].reshape(T, PE_DIM), wo[:ATT_WIDTH],
                     wo[ATT_WIDTH:ATT_WIDTH + RWKV_WIDTH], wo[ATT_WIDTH + RWKV_WIDTH:], rowv(ln1_g[i]),
                     rowv(ln1_b[i]), bf(mlp_w1[i]), bf(mlp_w2[i]), bf(pe_gate[i]), rowv(pe_gate_b[i]),
                     bf(pe_proj[i]), rowv(ln2_g[i]), rowv(ln2_b[i]))
    return xs.reshape(B, S, D_MODEL)
```

```python
import functools

import numpy as np
import jax
import jax.numpy as jnp
from jax import lax
from jax.experimental import pallas as pl
from jax.experimental.pallas import tpu as pltpu

D_MODEL = 1024
PE_DIM = 256
HEAD_DIM = 64
ROPE_THETA = 10000.0
ATT_WIDTH = 384
ATT_HEADS = 6
Q_RANK = 256
IDX_HEADS = 8
IDX_DIM = 64
TOPK_MAX = 256
RWKV_WIDTH = 384
RWKV_HEADS = 6
DECAY_RANK = 64
ICLR_RANK = 64
GATE_RANK = 128
GN_EPS = 64e-5
POOL_WINDOWS = (2, 4, 8, 16)
POOL_WIDTH = 256
POOL_GROUP = 64
B_COLS = 3 * RWKV_WIDTH + DECAY_RANK + ICLR_RANK + GATE_RANK
D_FF = 4 * D_MODEL
LN_EPS = 1e-5
RMS_EPS = 1e-6

COL_CQ = 0
COL_KIK = 256
COL_KIK_P = 384
COL_VIW = 512
COL_B = 640
COL_C = COL_B + B_COLS
N_IN_P = COL_C + POOL_WIDTH

LANES = 128
INT_MIN = -2 ** 31
LOG2E = 1.4426950408889634
VMEM_LIMIT = 56 * 1024 * 1024

TM_PROJ = 512
TQ = 256
ATT_KEY_CHUNK = 512
CHUNK = 64
RWKV_ROWS = 4
TM_MLP = 512
TF_MLP = 1024

_NT = (((1,), (1,)), ((), ()))
_TN = (((0,), (0,)), ((), ()))


def _params(*sem):
    return pltpu.CompilerParams(dimension_semantics=sem, vmem_limit_bytes=VMEM_LIMIT)


def _ln(x, g, b):
    mu = jnp.mean(x, axis=-1, keepdims=True)
    xc = x - mu
    var = jnp.mean(xc * xc, axis=-1, keepdims=True)
    return xc * lax.rsqrt(var + LN_EPS) * g + b


def _bf(x):
    return x.astype(jnp.bfloat16)


def _dot(a, b):
    return jnp.dot(a, b, preferred_element_type=jnp.float32)


def _dot_nt(a, b):
    return lax.dot_general(a, b, _NT, preferred_element_type=jnp.float32)


def _dot_tn(a, b):
    return lax.dot_general(a, b, _TN, preferred_element_type=jnp.float32)


def _rope_kernel(pos_ref, inv_ref, cos_ref, sin_ref):
    for r in range(pos_ref.shape[0]):
        col = jnp.broadcast_to(pos_ref[r:r + 1, :], (LANES, LANES)).T
        ang = col * inv_ref[...]
        cos_ref[r * LANES:(r + 1) * LANES, :] = jnp.cos(ang)
        sin_ref[r * LANES:(r + 1) * LANES, :] = jnp.sin(ang)


def _rope_tables(pos_f32):
    T = pos_f32.shape[0] * LANES
    rows = min(pos_f32.shape[0], 16)
    inv = ROPE_THETA ** (-np.arange(0, HEAD_DIM, 2, dtype=np.float32) / HEAD_DIM)
    inv128 = jnp.asarray(np.tile(inv, LANES // (HEAD_DIM // 2))[None, :], jnp.float32)
    return pl.pallas_call(
        _rope_kernel,
        out_shape=(jax.ShapeDtypeStruct((T, LANES), jnp.float32),) * 2,
        grid=(pos_f32.shape[0] // rows,),
        in_specs=[pl.BlockSpec((rows, LANES), lambda i: (i, 0)),
                  pl.BlockSpec((1, LANES), lambda i: (0, 0))],
        out_specs=(pl.BlockSpec((rows * LANES, LANES), lambda i: (i, 0)),) * 2,
        compiler_params=_params("parallel"),
        name="rope_tables",
    )(pos_f32, inv128)


def _inproj_kernel(apply_ln, x_ref, g_ref, b_ref, cos_ref, sin_ref, win_ref, wq_ref, wqp_ref,
                   wi_ref, wip_ref, *out_refs):
    if apply_ln:
        xn_ref, q_ref, iq_ref, iw_ref, kr_ref, ikr_ref, va_ref, zb_ref, zc_ref = out_refs
        x = _ln(x_ref[...], g_ref[...], b_ref[...])
        xn_ref[...] = x
    else:
        q_ref, iq_ref, iw_ref, kr_ref, ikr_ref, va_ref, zb_ref, zc_ref = out_refs
        x = x_ref[...]
    z = _dot(_bf(x), win_ref[...])
    cos = cos_ref[...]
    sin = sin_ref[...]
    cq = _bf(z[:, COL_CQ:COL_CQ + Q_RANK])
    cos3 = jnp.concatenate([cos] * 3, axis=1)
    sin3 = jnp.concatenate([sin] * 3, axis=1)
    q = _bf((_dot(cq, wq_ref[...]) * cos3 + _dot(cq, wqp_ref[...]) * sin3) * LOG2E)
    for j in range(q_ref.shape[0]):
        q_ref[j] = q[:, j * LANES:(j + 1) * LANES]
    cos4 = jnp.concatenate([cos] * 4, axis=1)
    sin4 = jnp.concatenate([sin] * 4, axis=1)
    iq = _bf(_dot(cq, wi_ref[...]) * cos4 + _dot(cq, wip_ref[...]) * sin4)
    for j in range(iq_ref.shape[0]):
        iq_ref[j] = iq[:, j * LANES:(j + 1) * LANES]
    kik = z[:, COL_KIK:COL_KIK + LANES] * cos + z[:, COL_KIK_P:COL_KIK_P + LANES] * sin
    kr_ref[...] = _bf(kik[:, :HEAD_DIM])
    ikr_ref[...] = _bf(kik[:, HEAD_DIM:])
    viw = z[:, COL_VIW:COL_VIW + LANES]
    va_ref[...] = _bf(viw[:, :HEAD_DIM])
    iw_ref[...] = viw * (IDX_HEADS ** -0.5)
    zb_ref[...] = z[:, COL_B:COL_B + B_COLS]
    zc_ref[...] = z[:, COL_C:COL_C + POOL_WIDTH]


def _inproj(x, g, b, cos, sin, win, wq, wqp, wi, wip, apply_ln):
    T = x.shape[0]
    tm = min(T, TM_PROJ)
    row = lambda i: (i, 0)
    fix = lambda i: (0, 0)
    bf16, f32 = jnp.bfloat16, jnp.float32
    outs = [(LANES, f32), (HEAD_DIM, bf16), (HEAD_DIM, bf16), (HEAD_DIM, bf16), (B_COLS, f32),
            (POOL_WIDTH, f32)]
    pairs = [ATT_WIDTH // LANES, IDX_HEADS * IDX_DIM // LANES]
    shapes = [jax.ShapeDtypeStruct((n, T, LANES), bf16) for n in pairs]
    shapes += [jax.ShapeDtypeStruct((T, n), dt) for n, dt in outs]
    specs = [pl.BlockSpec((n, tm, LANES), lambda i: (0, i, 0)) for n in pairs]
    specs += [pl.BlockSpec((tm, n), row) for n, _ in outs]
    if apply_ln:
        shapes = [jax.ShapeDtypeStruct((T, D_MODEL), f32)] + shapes
        specs = [pl.BlockSpec((tm, D_MODEL), row)] + specs
    return pl.pallas_call(
        functools.partial(_inproj_kernel, apply_ln),
        out_shape=tuple(shapes),
        grid=(T // tm,),
        in_specs=[pl.BlockSpec((tm, D_MODEL), row),
                  pl.BlockSpec((1, D_MODEL), fix), pl.BlockSpec((1, D_MODEL), fix),
                  pl.BlockSpec((tm, LANES), row), pl.BlockSpec((tm, LANES), row),
                  pl.BlockSpec((D_MODEL, N_IN_P), fix),
                  pl.BlockSpec((Q_RANK, ATT_WIDTH), fix), pl.BlockSpec((Q_RANK, ATT_WIDTH), fix),
                  pl.BlockSpec((Q_RANK, IDX_HEADS * IDX_DIM), fix),
                  pl.BlockSpec((Q_RANK, IDX_HEADS * IDX_DIM), fix)],
        out_specs=tuple(specs),
        compiler_params=_params("parallel"),
        name="inproj",
    )(x, g, b, cos, sin, win, wq, wqp, wi, wip)


def _attn_block(kend, search, topk, q_ref, iq_ref, iw_ref, kr_ref, ikr_ref, va_ref, g_ref, o_ref,
                key_ref, bias_ref, oacc_ref, hi_ref, lo_ref):
    tq = q_ref.shape[1]
    row_t = pl.program_id(1) * tq + lax.broadcasted_iota(jnp.int32, (tq, 1), 0)
    col = lax.broadcasted_iota(jnp.int32, (1, kend), 1)
    causal = col <= row_t

    if not search:
        bias_ref[:, :kend] = jnp.where(causal, 0.0, -jnp.inf)
    else:
        lane = lax.broadcasted_iota(jnp.int32, (1, LANES), 1)
        bias_ref[:, :kend] = jnp.zeros((tq, kend), jnp.float32)

        def idx_pair(j, carry):
            iqp = iq_ref[j]
            ik = ikr_ref[0:kend, :]
            acc = bias_ref[:, :kend]
            lgs = [_dot_nt(iqp[:, hh * IDX_DIM:(hh + 1) * IDX_DIM], ik) for hh in range(2)]
            for hh, lg in enumerate(lgs):
                w_h = jnp.sum(jnp.where(lane == HEAD_DIM + 2 * j + hh, iw_ref[...], 0.0), axis=1, keepdims=True)
                acc = acc + jnp.maximum(lg, 0.0) * w_h
            bias_ref[:, :kend] = acc
            return carry

        lax.fori_loop(0, iq_ref.shape[0], idx_pair, 0)
        score = bias_ref[:, :kend] + 0.0
        bits = pltpu.bitcast(score, jnp.int32)
        key = bits ^ ((bits >> 31) & 0x7FFFFFFF)
        key_ref[:, :kend] = jnp.where(causal, key, INT_MIN)

        i16 = jnp.int16
        one16 = jnp.ones((tq, LANES), i16)
        zero16 = jnp.zeros((tq, LANES), i16)

        def count16(ref, cand):
            c16 = jnp.broadcast_to(cand, (tq, LANES)).astype(i16)
            acc = zero16
            for i in range(kend // LANES):
                acc = acc + jnp.where(ref[:, i * LANES:(i + 1) * LANES] >= c16, one16, zero16)
            return jnp.sum(acc.astype(jnp.float32), axis=1, keepdims=True)

        def kth_largest16(ref, k):
            t = jnp.where(count16(ref, jnp.zeros((tq, 1), jnp.int32)) >= k, 0, -32768).astype(jnp.int32)

            def bit_step(i, t):
                cand = t | (jnp.int32(1) << (14 - i))
                return jnp.where(count16(ref, cand) >= k, cand, t)

            return lax.fori_loop(0, 15, bit_step, t)

        keyv = key_ref[:, :kend]
        hi_ref[:, :kend] = (keyv >> 16).astype(i16)
        tau_hi = kth_largest16(hi_ref, jnp.full((tq, 1), float(topk), jnp.float32))
        above = jnp.where(tau_hi < 32767, count16(hi_ref, jnp.minimum(tau_hi + 1, 32767)), 0.0)
        lo = ((keyv & 0xFFFF) - 32768).astype(i16)
        same_hi = hi_ref[:, :kend] == jnp.broadcast_to(tau_hi, (tq, kend)).astype(i16)
        lo_ref[:, :kend] = jnp.where(same_hi, lo, -32768)
        tau_lo = kth_largest16(lo_ref, topk - above)
        tau = (tau_hi << 16) | (tau_lo + 32768)
        tau = jnp.maximum(tau, INT_MIN + 1)
        ge = key_ref[:, :kend] >= tau
        n_ge = jnp.sum(jnp.where(ge, 1.0, 0.0), axis=1, keepdims=True)
        bias_ref[:, :kend] = jnp.where(ge, 0.0, -jnp.inf)

        @pl.when(jnp.max(n_ge) > topk)
        def _():
            need = topk - jnp.sum(jnp.where(key_ref[:, :kend] > tau, 1.0, 0.0), axis=1, keepdims=True)
            ri = lax.broadcasted_iota(jnp.int32, (LANES, LANES), 0)
            ci = lax.broadcasted_iota(jnp.int32, (LANES, LANES), 1)
            upper = _bf(jnp.where(ri <= ci, 1.0, 0.0))
            before = jnp.zeros((tq, 1), jnp.float32)
            for c in range(kend // LANES):
                keyc = key_ref[:, c * LANES:(c + 1) * LANES]
                eq = keyc == tau
                rank = _dot(_bf(jnp.where(eq, 1.0, 0.0)), upper) + before
                keep = (keyc > tau) | (eq & (rank <= need))
                bias_ref[:, c * LANES:(c + 1) * LANES] = jnp.where(keep, 0.0, -jnp.inf)
                before = rank[:, LANES - 1:LANES]

    kc = min(kend, ATT_KEY_CHUNK)
    lane_tiles = lambda a: [a[:, i * LANES:(i + 1) * LANES] for i in range(a.shape[1] // LANES)]

    def att_pair(j, carry):
        qp = q_ref[j]
        chunks = range(0, kend, kc)
        s_h = [[_dot_nt(qp[:, hh * HEAD_DIM:(hh + 1) * HEAD_DIM], kr_ref[c:c + kc, :]) + bias_ref[:, c:c + kc]
                for c in chunks] for hh in range(2)]
        outs = []
        for s_c in s_h:
            m = jnp.max(functools.reduce(jnp.maximum, [t for s in s_c for t in lane_tiles(s)]),
                        axis=1, keepdims=True)
            p_c = [jnp.exp2(s - m) for s in s_c]
            l = jnp.sum(sum(t for p in p_c for t in lane_tiles(p)), axis=1, keepdims=True)
            acc = sum(_dot(_bf(p), va_ref[c:c + kc, :]) for p, c in zip(p_c, chunks))
            outs.append(acc / l)
        oacc_ref[j] = jnp.concatenate(outs, axis=1)
        return carry

    lax.fori_loop(0, q_ref.shape[0], att_pair, 0)
    o = jnp.concatenate([oacc_ref[j] for j in range(q_ref.shape[0])], axis=1)
    o = o * lax.rsqrt(jnp.mean(o * o, axis=-1, keepdims=True) + RMS_EPS) * g_ref[...]
    o_ref[...] = _bf(o)


def _key_extents(S, tq, topk):
    out = []
    for v in range(S // tq):
        need = (v + 1) * tq
        if need <= topk:
            out.append((need, False))
        else:
            out.append((min(S, -(-need // (2 * tq)) * 2 * tq), True))
    return out


def _attn_kernel(S, *refs):
    tq = refs[0].shape[1]
    topk = min(TOPK_MAX, S // 4)
    qi = pl.program_id(1)
    extents = _key_extents(S, tq, topk)
    for ext in sorted(set(extents)):
        blocks = [v for v, e in enumerate(extents) if e == ext]
        pl.when((qi >= blocks[0]) & (qi <= blocks[-1]))(functools.partial(_attn_block, *ext, topk, *refs))


def _attention(B, S, q, iq, iw, kr, ikr, va, g):
    tq = min(S, TQ)
    nq = S // tq
    qrow = lambda b, i: (b * nq + i, 0)
    seq = lambda b, i: (b, 0)
    fix = lambda b, i: (0, 0)
    return pl.pallas_call(
        functools.partial(_attn_kernel, S),
        out_shape=jax.ShapeDtypeStruct((B * S, ATT_WIDTH), jnp.bfloat16),
        grid=(B, nq),
        in_specs=[pl.BlockSpec((q.shape[0], tq, LANES), lambda b, i: (0, b * nq + i, 0)),
                  pl.BlockSpec((iq.shape[0], tq, LANES), lambda b, i: (0, b * nq + i, 0)),
                  pl.BlockSpec((tq, LANES), qrow),
                  pl.BlockSpec((S, HEAD_DIM), seq), pl.BlockSpec((S, HEAD_DIM), seq),
                  pl.BlockSpec((S, HEAD_DIM), seq),
                  pl.BlockSpec((1, ATT_WIDTH), fix)],
        out_specs=pl.BlockSpec((tq, ATT_WIDTH), qrow),
        scratch_shapes=[pltpu.VMEM((tq, S), jnp.int32), pltpu.VMEM((tq, S), jnp.float32),
                        pltpu.VMEM((q.shape[0], tq, LANES), jnp.float32),
                        pltpu.VMEM((tq, S), jnp.int16), pltpu.VMEM((tq, S), jnp.int16)],
        compiler_params=_params("parallel", "parallel"),
        name="dsa_attention",
    )(q, iq, iw, kr, ikr, va, g)


def _split3(x):
    hi = _bf(x)
    r1 = x - hi.astype(jnp.float32)
    mid = _bf(r1)
    lo = _bf(r1 - mid.astype(jnp.float32))
    return hi, mid, lo


def _rwkv_kernel(zb_ref, mu_ref, w0_ref, w2_ref, a0_ref, a2_ref, g2_ref, kk_ref, ka_ref, rk_ref,
                 lg_ref, lb_ref, o_ref, prev_ref, state_ref):
    G, C, _ = zb_ref.shape
    W = RWKV_WIDTH
    R = G * C
    H = RWKV_HEADS

    @pl.when(pl.program_id(1) == 0)
    def _():
        prev_ref[...] = jnp.zeros_like(prev_ref)
        state_ref[...] = jnp.zeros_like(state_ref)

    x = zb_ref[...].reshape(R, B_COLS)
    rowi = lax.broadcasted_iota(jnp.int32, (R, 1), 0)
    shifted = pltpu.roll(x, 1, 0)
    for gi in range(G):
        shifted = jnp.where(rowi == gi * C, prev_ref[8 * gi:8 * gi + 1, :], shifted)
        prev_ref[8 * gi:8 * gi + 1, :] = x[gi * C + C - 1:gi * C + C, :]
    xm = x + (shifted - x) * mu_ref[...]
    r = xm[:, 0:W]
    k = xm[:, W:2 * W]
    v = xm[:, 2 * W:3 * W]
    wl = xm[:, 3 * W:3 * W + DECAY_RANK]
    al = xm[:, 3 * W + DECAY_RANK:3 * W + DECAY_RANK + ICLR_RANK]
    gl = xm[:, 3 * W + DECAY_RANK + ICLR_RANK:]

    wpre = w0_ref[...] + _dot(_bf(jnp.tanh(wl)), w2_ref[...])
    nw = -wpre
    w = -(jnp.maximum(nw, 0.0) + jnp.log(1.0 + jnp.exp(-jnp.abs(nw)))) - 0.5
    logd = -jnp.exp(w)
    a = jax.nn.sigmoid(a0_ref[...] + _dot(_bf(al), a2_ref[...]))
    g = _dot(_bf(jax.nn.sigmoid(gl)), g2_ref[...])

    rr = lax.broadcasted_iota(jnp.int32, (R, R), 0)
    cc = lax.broadcasted_iota(jnp.int32, (R, R), 1)
    tri = _bf(jnp.where((rr >= cc) & (rr // C == cc // C), 1.0, 0.0))
    hi, mid, lo = _split3(logd)
    cw = _dot(tri, hi) + _dot(tri, mid) + _dot(tri, lo)
    e_in = jnp.exp(cw)
    e_ex = jnp.exp(cw - logd)
    e_inv = jnp.exp(-cw)

    kk = k * kk_ref[...]
    k2 = k * (1.0 + (a - 1.0) * ka_ref[...])
    rk2 = r * k2 * rk_ref[...]
    ri = lax.broadcasted_iota(jnp.int32, (C, C), 0)
    ci = lax.broadcasted_iota(jnp.int32, (C, C), 1)
    strict = ri > ci
    ri2 = lax.broadcasted_iota(jnp.int32, (C, 2 * C), 0)
    ci2 = lax.broadcasted_iota(jnp.int32, (C, 2 * C), 1)
    incl2 = ri2 >= jnp.where(ci2 >= C, ci2 - C, ci2)
    eye = jnp.where(ri == ci, 1.0, 0.0)

    units = [(gi, h) for gi in range(G) for h in range(H)]

    def cut(arr, u):
        gi, h = u
        return arr[gi * C:(gi + 1) * C, h * HEAD_DIM:(h + 1) * HEAD_DIM]

    def unit_norm(u):
        kkh = cut(kk, u)
        return kkh / jnp.maximum(jnp.sqrt(jnp.sum(kkh * kkh, axis=-1, keepdims=True)), 1e-12)

    kkn = [unit_norm(u) for u in units]
    v_u = [cut(v, u) for u in units]
    at = [_bf(-kn * cut(e_ex, u)) for kn, u in zip(kkn, units)]
    rt = [_bf(cut(r, u) * cut(e_in, u)) for u in units]
    bt = [kn * cut(a, u) * cut(e_inv, u) for kn, u in zip(kkn, units)]
    kt = [cut(k2, u) * cut(e_inv, u) for u in units]
    bk16 = [_bf(jnp.concatenate([b_, k_], axis=0)) for b_, k_ in zip(bt, kt)]
    v16 = [_bf(v_) for v_ in v_u]
    g_a = [_dot_nt(a_, m_) for a_, m_ in zip(at, bk16)]
    l_ab = [jnp.where(strict, g_[:, :C], 0.0) for g_ in g_a]
    a_ak = [_bf(jnp.where(strict, g_[:, C:], 0.0)) for g_ in g_a]
    tinv = [eye + l_ for l_ in l_ab]
    pw = l_ab
    n = 2
    while n < C:
        pw = [_dot(_bf(p_), _bf(p_)) for p_ in pw]
        tinv = [t_ + _dot(_bf(t_), _bf(p_)) for t_, p_ in zip(tinv, pw)]
        n *= 2
    s0 = [state_ref[i] for i in range(len(units))]
    s016 = [_bf(s_) for s_ in s0]
    rhs = [_dot_nt(a_, s_) + _dot(m_, v_) for a_, s_, m_, v_ in zip(at, s016, a_ak, v16)]
    uu = [_dot(_bf(t_), _bf(x_)) for t_, x_ in zip(tinv, rhs)]
    uv16 = [_bf(jnp.concatenate([u_, v_], axis=0)) for u_, v_ in zip(uu, v_u)]
    g_r = [_bf(jnp.where(incl2, _dot_nt(r_, m_), 0.0)) for r_, m_ in zip(rt, bk16)]
    y = [_dot(m_, x_) + _dot_nt(r_, s_) for m_, x_, r_, s_ in zip(g_r, uv16, rt, s016)]
    for i, u in enumerate(units):
        state_ref[i] = (s0[i] + _dot_tn(uv16[i], bk16[i])) * cut(e_in, u)[C - 1:C, :]
    for gi in range(G):
        outs = []
        for h in range(H):
            i = gi * H + h
            u = units[i]
            sl = slice(h * HEAD_DIM, (h + 1) * HEAD_DIM)
            ym = jnp.mean(y[i], axis=-1, keepdims=True)
            yc = y[i] - ym
            yv = jnp.mean(yc * yc, axis=-1, keepdims=True)
            yn = yc * lax.rsqrt(yv + GN_EPS) * lg_ref[:, sl] + lb_ref[:, sl]
            bonus = jnp.sum(cut(rk2, u), axis=-1, keepdims=True) * v_u[i]
            outs.append((yn + bonus) * cut(g, u))
        o_ref[gi] = _bf(jnp.concatenate(outs, axis=1))


def _rwkv(B, S, zb, mu, w0, w2, a0, a2, g2, k_k, k_a, r_k, lnx_g, lnx_b):
    C = min(S, CHUNK)
    G = min(B, RWKV_ROWS)
    blk = lambda b, c: (b, c, 0)
    fix = lambda b, c: (0, 0)
    vec = lambda n: pl.BlockSpec((1, n), fix)
    return pl.pallas_call(
        _rwkv_kernel,
        out_shape=jax.ShapeDtypeStruct((B, S, RWKV_WIDTH), jnp.bfloat16),
        grid=(B // G, S // C),
        in_specs=[pl.BlockSpec((G, C, B_COLS), blk), vec(B_COLS), vec(RWKV_WIDTH),
                  pl.BlockSpec((DECAY_RANK, RWKV_WIDTH), fix), vec(RWKV_WIDTH),
                  pl.BlockSpec((ICLR_RANK, RWKV_WIDTH), fix),
                  pl.BlockSpec((GATE_RANK, RWKV_WIDTH), fix),
                  vec(RWKV_WIDTH), vec(RWKV_WIDTH), vec(RWKV_WIDTH), vec(RWKV_WIDTH),
                  vec(RWKV_WIDTH)],
        out_specs=pl.BlockSpec((G, C, RWKV_WIDTH), blk),
        scratch_shapes=[pltpu.VMEM((8 * G, B_COLS), jnp.float32),
                        pltpu.VMEM((G * RWKV_HEADS, HEAD_DIM, HEAD_DIM), jnp.float32)],
        compiler_params=_params("parallel", "arbitrary"),
        name="rwkv7",
    )(zb.reshape(B, S, B_COLS), mu, w0, w2, a0, a2, g2, k_k, k_a, r_k, lnx_g, lnx_b
      ).reshape(B * S, RWKV_WIDTH)


def _pool_kernel(zc_ref, w_ref, sc_ref, o_ref):
    S = zc_ref.shape[0]
    x = zc_ref[...]
    row = lax.broadcasted_iota(jnp.int32, (S, 1), 0)
    lane_grp = lax.broadcasted_iota(jnp.int32, (1, POOL_WIDTH), 1) // POOL_GROUP

    def lag(y, n):
        return jnp.where(row >= n, pltpu.roll(y, n, 0), 0.0)

    w2 = x + lag(x, 1)
    w4 = w2 + lag(w2, 2)
    w8 = w4 + lag(w4, 4)
    w16 = w8 + lag(w8, 8)
    wsum = jnp.where(lane_grp == 0, w2, jnp.where(lane_grp == 1, w4, jnp.where(lane_grp == 2, w8, w16)))
    win = jnp.where(lane_grp == 0, 2, jnp.where(lane_grp == 1, 4, jnp.where(lane_grp == 2, 8, 16)))
    count = jnp.minimum(row + 1, win).astype(jnp.float32)
    pooled = wsum / count - x
    o_ref[...] = _bf(_dot(_bf(pooled), w_ref[...]) * sc_ref[...])


def _pool(B, S, zc, w_bd, scale):
    return pl.pallas_call(
        _pool_kernel,
        out_shape=jax.ShapeDtypeStruct((B * S, POOL_WIDTH), jnp.bfloat16),
        grid=(B,),
        in_specs=[pl.BlockSpec((S, POOL_WIDTH), lambda b: (b, 0)),
                  pl.BlockSpec((POOL_WIDTH, POOL_WIDTH), lambda b: (0, 0)),
                  pl.BlockSpec((1, POOL_WIDTH), lambda b: (0, 0))],
        out_specs=pl.BlockSpec((S, POOL_WIDTH), lambda b: (b, 0)),
        compiler_params=_params("parallel"),
        name="pool",
    )(zc, w_bd, scale)


def _outmlp_kernel(alpha, oa_ref, ob_ref, oc_ref, x_ref, p_ref, wa_ref, wb_ref, wc_ref, g1_ref, b1_ref,
                   w1_ref, w2_ref, pg_ref, pgb_ref, pp_ref, g2_ref, b2_ref, o_ref):
    tm = x_ref.shape[0]
    halves = [slice(0, tm // 2), slice(tm // 2, tm)]
    mix = [_dot(oa_ref[h, :], wa_ref[...]) + _dot(ob_ref[h, :], wb_ref[...]) + _dot(oc_ref[h, :], wc_ref[...])
           for h in halves]
    x1 = [_ln(alpha * x_ref[h, :] + m, g1_ref[...], b1_ref[...]) for h, m in zip(halves, mix)]
    xb = [_bf(v) for v in x1]
    gate = [jax.nn.sigmoid(_dot(b_, pg_ref[...]) + pgb_ref[...]) for b_ in xb]
    pe = [_dot(_bf(p_ref[0, h, :]), pp_ref[...]) for h in halves]
    acc = [alpha * v + g_ * e_ for v, g_, e_ in zip(x1, gate, pe)]
    xb_all = jnp.concatenate(xb, axis=0)
    nk = D_FF // TF_MLP
    for kf in range(nk):
        cols = slice(kf * TF_MLP, (kf + 1) * TF_MLP)
        hk = jnp.maximum(_dot(xb_all, w1_ref[:, cols]), 0.0)
        h2 = _bf(hk * hk)
        if kf < nk - 1:
            part = _dot(h2, w2_ref[cols, :])
            acc = [a + part[h, :] for a, h in zip(acc, halves)]
        else:
            acc = [a + _dot(h2[h, :], w2_ref[cols, :]) for a, h in zip(acc, halves)]
    for h, a in zip(halves, acc):
        o_ref[h, :] = _ln(a, g2_ref[...], b2_ref[...])


def _outmlp(alpha, layer, oa, ob, oc, x, p, wa, wb, wc, g1, b1, w1, w2, pg, pgb, pp, g2, b2):
    T = x.shape[0]
    tm = min(T, TM_MLP)
    row = lambda i: (i, 0)
    fix = lambda i: (0, 0)
    once = pl.Buffered(1)
    wspec = lambda r, c: pl.BlockSpec((r, c), fix, pipeline_mode=once)
    vec = pl.BlockSpec((1, D_MODEL), fix)
    return pl.pallas_call(
        functools.partial(_outmlp_kernel, alpha),
        out_shape=jax.ShapeDtypeStruct((T, D_MODEL), jnp.float32),
        grid=(T // tm,),
        in_specs=[pl.BlockSpec((tm, ATT_WIDTH), row), pl.BlockSpec((tm, RWKV_WIDTH), row),
                  pl.BlockSpec((tm, POOL_WIDTH), row), pl.BlockSpec((tm, D_MODEL), row),
                  pl.BlockSpec((1, tm, PE_DIM), lambda i: (layer, i, 0)),
                  wspec(ATT_WIDTH, D_MODEL), wspec(RWKV_WIDTH, D_MODEL), wspec(POOL_WIDTH, D_MODEL), vec, vec,
                  wspec(D_MODEL, D_FF), wspec(D_FF, D_MODEL), wspec(D_MODEL, D_MODEL), vec,
                  wspec(PE_DIM, D_MODEL), vec, vec],
        out_specs=pl.BlockSpec((tm, D_MODEL), row),
        compiler_params=_params("parallel"),
        name="outproj_mlp",
    )(oa, ob, oc, x, p, wa, wb, wc, g1, b1, w1, w2, pg, pgb, pp, g2, b2)


def _rot_partner(w):
    half = HEAD_DIM // 2
    return jnp.concatenate([-w[..., half:], w[..., :half]], axis=-1)


def _per_head_partner(w, heads):
    r = w.reshape(w.shape[0], heads, HEAD_DIM)
    return _rot_partner(r).reshape(w.shape)


def _arrange_w_in(w):
    cq = w[:, 0:256]
    ka = w[:, 256:320]
    va = w[:, 320:384]
    ik = w[:, 384:448]
    iw = w[:, 448:456]
    rest = w[:, 456:]
    pad = jnp.zeros((w.shape[0], LANES - HEAD_DIM - IDX_HEADS), w.dtype)
    return jnp.concatenate([cq, ka, ik, _rot_partner(ka), _rot_partner(ik), va, iw, pad, rest], axis=1)


def kernel(x, p, positions, ln_emb_g, ln_emb_b, w_in, w_uq, w_uqi, attn_norm_g, rwkv_mu, rwkv_w0, rwkv_w2, rwkv_a0, rwkv_a2, rwkv_g2, rwkv_k_k, rwkv_k_a, rwkv_r_k, rwkv_lnx_g, rwkv_lnx_b, pool_w, pool_scale, w_out, ln1_g, ln1_b, mlp_w1, mlp_w2, pe_proj, pe_gate, pe_gate_b, ln2_g, ln2_b):
    B, S, _ = x.shape
    T = B * S
    depth = w_in.shape[0]
    alpha = (2 * depth) ** 0.25
    bf = lambda a: a.astype(jnp.bfloat16)
    rowv = lambda a: a.reshape(1, -1)

    cos, sin = _rope_tables(positions.astype(jnp.float32).reshape(T // LANES, LANES))
    xs = x.reshape(T, D_MODEL)
    for i in range(depth):
        win = bf(_arrange_w_in(w_in[i]))
        wq = w_uq[i] * (HEAD_DIM ** -0.5)
        wi = w_uqi[i] * (IDX_DIM ** -0.5)
        res = _inproj(xs, rowv(ln_emb_g), rowv(ln_emb_b), cos, sin, win,
                      bf(wq), bf(_per_head_partner(wq, ATT_HEADS)),
                      bf(wi), bf(_per_head_partner(wi, IDX_HEADS)), apply_ln=(i == 0))
        if i == 0:
            xs, res = res[0], res[1:]
        q, iq, iw, kr, ikr, va, zb, zc = res
        o_a = _attention(B, S, q, iq, iw, kr, ikr, va, rowv(attn_norm_g[i]))
        o_b = _rwkv(B, S, zb, rowv(rwkv_mu[i]), rowv(rwkv_w0[i]), bf(rwkv_w2[i]), rowv(rwkv_a0[i]),
                    bf(rwkv_a2[i]), bf(rwkv_g2[i]), rowv(rwkv_k_k[i]), rowv(rwkv_k_a[i]),
                    rowv(rwkv_r_k[i]), rowv(rwkv_lnx_g[i]), rowv(rwkv_lnx_b[i]))
        w_bd = jax.scipy.linalg.block_diag(*[pool_w[i, gi] for gi in range(len(POOL_WINDOWS))])
        o_c = _pool(B, S, zc, bf(w_bd), rowv(pool_scale[i]))
        wo = bf(w_out[i])
        xs = _outmlp(alpha, i, o_a, o_b, o_c, xs, p.reshape(depth, T, PE_DIM), wo[:ATT_WIDTH],
                     wo[ATT_WIDTH:ATT_WIDTH + RWKV_WIDTH], wo[ATT_WIDTH + RWKV_WIDTH:], rowv(ln1_g[i]),
                     rowv(ln1_b[i]), bf(mlp_w1[i]), bf(mlp_w2[i]), bf(pe_gate[i]), rowv(pe_gate_b[i]),
                     bf(pe_proj[i]), rowv(ln2_g[i]), rowv(ln2_b[i]))
    return xs.reshape(B, S, D_MODEL)
```

```python
import functools

import numpy as np
import jax
import jax.numpy as jnp
from jax import lax
from jax.experimental import pallas as pl
from jax.experimental.pallas import tpu as pltpu

D_MODEL = 1024
PE_DIM = 256
HEAD_DIM = 64
ROPE_THETA = 10000.0
ATT_WIDTH = 384
ATT_HEADS = 6
Q_RANK = 256
IDX_HEADS = 8
IDX_DIM = 64
TOPK_MAX = 256
RWKV_WIDTH = 384
RWKV_HEADS = 6
DECAY_RANK = 64
ICLR_RANK = 64
GATE_RANK = 128
GN_EPS = 64e-5
POOL_WINDOWS = (2, 4, 8, 16)
POOL_WIDTH = 256
POOL_GROUP = 64
B_COLS = 3 * RWKV_WIDTH + DECAY_RANK + ICLR_RANK + GATE_RANK
D_FF = 4 * D_MODEL
LN_EPS = 1e-5
RMS_EPS = 1e-6

COL_CQ = 0
COL_KIK = 256
COL_KIK_P = 384
COL_VIW = 512
COL_B = 640
COL_C = COL_B + B_COLS
N_IN_P = COL_C + POOL_WIDTH

LANES = 128
INT_MIN = -2 ** 31
LOG2E = 1.4426950408889634
VMEM_LIMIT = 56 * 1024 * 1024

TM_PROJ = 512
TQ = 256
ATT_KEY_CHUNK = 512
CHUNK = 64
RWKV_ROWS = 4
TM_MLP = 512
TF_MLP = 1024

_NT = (((1,), (1,)), ((), ()))
_TN = (((0,), (0,)), ((), ()))


def _params(*sem):
    return pltpu.CompilerParams(dimension_semantics=sem, vmem_limit_bytes=VMEM_LIMIT)


def _ln(x, g, b):
    mu = jnp.mean(x, axis=-1, keepdims=True)
    xc = x - mu
    var = jnp.mean(xc * xc, axis=-1, keepdims=True)
    return xc * lax.rsqrt(var + LN_EPS) * g + b


def _bf(x):
    return x.astype(jnp.bfloat16)


def _dot(a, b):
    return jnp.dot(a, b, preferred_element_type=jnp.float32)


def _dot_nt(a, b):
    return lax.dot_general(a, b, _NT, preferred_element_type=jnp.float32)


def _dot_tn(a, b):
    return lax.dot_general(a, b, _TN, preferred_element_type=jnp.float32)


def _rope_kernel(pos_ref, inv_ref, cos_ref, sin_ref):
    for r in range(pos_ref.shape[0]):
        col = jnp.broadcast_to(pos_ref[r:r + 1, :], (LANES, LANES)).T
        ang = col * inv_ref[...]
        cos_ref[r * LANES:(r + 1) * LANES, :] = jnp.cos(ang)
        sin_ref[r * LANES:(r + 1) * LANES, :] = jnp.sin(ang)


def _rope_tables(pos_f32):
    T = pos_f32.shape[0] * LANES
    rows = min(pos_f32.shape[0], 16)
    inv = ROPE_THETA ** (-np.arange(0, HEAD_DIM, 2, dtype=np.float32) / HEAD_DIM)
    inv128 = jnp.asarray(np.tile(inv, LANES // (HEAD_DIM // 2))[None, :], jnp.float32)
    return pl.pallas_call(
        _rope_kernel,
        out_shape=(jax.ShapeDtypeStruct((T, LANES), jnp.float32),) * 2,
        grid=(pos_f32.shape[0] // rows,),
        in_specs=[pl.BlockSpec((rows, LANES), lambda i: (i, 0)),
                  pl.BlockSpec((1, LANES), lambda i: (0, 0))],
        out_specs=(pl.BlockSpec((rows * LANES, LANES), lambda i: (i, 0)),) * 2,
        compiler_params=_params("parallel"),
        name="rope_tables",
    )(pos_f32, inv128)


def _inproj_kernel(apply_ln, x_ref, g_ref, b_ref, cos_ref, sin_ref, win_ref, wq_ref, wqp_ref,
                   wi_ref, wip_ref, *out_refs):
    if apply_ln:
        xn_ref, q_ref, iq_ref, iw_ref, kr_ref, ikr_ref, va_ref, zb_ref, zc_ref = out_refs
        x = _ln(x_ref[...], g_ref[...], b_ref[...])
        xn_ref[...] = x
    else:
        q_ref, iq_ref, iw_ref, kr_ref, ikr_ref, va_ref, zb_ref, zc_ref = out_refs
        x = x_ref[...]
    z = _dot(_bf(x), win_ref[...])
    cos = cos_ref[...]
    sin = sin_ref[...]
    cq = _bf(z[:, COL_CQ:COL_CQ + Q_RANK])
    cos3 = jnp.concatenate([cos] * 3, axis=1)
    sin3 = jnp.concatenate([sin] * 3, axis=1)
    q = _bf((_dot(cq, wq_ref[...]) * cos3 + _dot(cq, wqp_ref[...]) * sin3) * LOG2E)
    for j in range(q_ref.shape[0]):
        q_ref[j] = q[:, j * LANES:(j + 1) * LANES]
    cos4 = jnp.concatenate([cos] * 4, axis=1)
    sin4 = jnp.concatenate([sin] * 4, axis=1)
    iq = _bf(_dot(cq, wi_ref[...]) * cos4 + _dot(cq, wip_ref[...]) * sin4)
    for j in range(iq_ref.shape[0]):
        iq_ref[j] = iq[:, j * LANES:(j + 1) * LANES]
    kik = z[:, COL_KIK:COL_KIK + LANES] * cos + z[:, COL_KIK_P:COL_KIK_P + LANES] * sin
    kr_ref[...] = _bf(kik[:, :HEAD_DIM])
    ikr_ref[...] = _bf(kik[:, HEAD_DIM:])
    viw = z[:, COL_VIW:COL_VIW + LANES]
    va_ref[...] = _bf(viw[:, :HEAD_DIM])
    iw_ref[...] = viw * (IDX_HEADS ** -0.5)
    zb_ref[...] = z[:, COL_B:COL_B + B_COLS]
    zc_ref[...] = z[:, COL_C:COL_C + POOL_WIDTH]


def _inproj(x, g, b, cos, sin, win, wq, wqp, wi, wip, apply_ln):
    T = x.shape[0]
    tm = min(T, TM_PROJ)
    row = lambda i: (i, 0)
    fix = lambda i: (0, 0)
    bf16, f32 = jnp.bfloat16, jnp.float32
    outs = [(LANES, f32), (HEAD_DIM, bf16), (HEAD_DIM, bf16), (HEAD_DIM, bf16), (B_COLS, f32),
            (POOL_WIDTH, f32)]
    pairs = [ATT_WIDTH // LANES, IDX_HEADS * IDX_DIM // LANES]
    shapes = [jax.ShapeDtypeStruct((n, T, LANES), bf16) for n in pairs]
    shapes += [jax.ShapeDtypeStruct((T, n), dt) for n, dt in outs]
    specs = [pl.BlockSpec((n, tm, LANES), lambda i: (0, i, 0)) for n in pairs]
    specs += [pl.BlockSpec((tm, n), row) for n, _ in outs]
    if apply_ln:
        shapes = [jax.ShapeDtypeStruct((T, D_MODEL), f32)] + shapes
        specs = [pl.BlockSpec((tm, D_MODEL), row)] + specs
    return pl.pallas_call(
        functools.partial(_inproj_kernel, apply_ln),
        out_shape=tuple(shapes),
        grid=(T // tm,),
        in_specs=[pl.BlockSpec((tm, D_MODEL), row),
                  pl.BlockSpec((1, D_MODEL), fix), pl.BlockSpec((1, D_MODEL), fix),
                  pl.BlockSpec((tm, LANES), row), pl.BlockSpec((tm, LANES), row),
                  pl.BlockSpec((D_MODEL, N_IN_P), fix),
                  pl.BlockSpec((Q_RANK, ATT_WIDTH), fix), pl.BlockSpec((Q_RANK, ATT_WIDTH), fix),
                  pl.BlockSpec((Q_RANK, IDX_HEADS * IDX_DIM), fix),
                  pl.BlockSpec((Q_RANK, IDX_HEADS * IDX_DIM), fix)],
        out_specs=tuple(specs),
        compiler_params=_params("parallel"),
        name="inproj",
    )(x, g, b, cos, sin, win, wq, wqp, wi, wip)


def _attn_block(kend, search, topk, q_ref, iq_ref, iw_ref, kr_ref, ikr_ref, va_ref, g_ref, o_ref,
                key_ref, bias_ref, oacc_ref, hi_ref, lo_ref):
    tq = q_ref.shape[1]
    row_t = pl.program_id(1) * tq + lax.broadcasted_iota(jnp.int32, (tq, 1), 0)
    col = lax.broadcasted_iota(jnp.int32, (1, kend), 1)
    causal = col <= row_t

    if not search:
        bias_ref[:, :kend] = jnp.where(causal, 0.0, -jnp.inf)
    else:
        lane = lax.broadcasted_iota(jnp.int32, (1, LANES), 1)
        bias_ref[:, :kend] = jnp.zeros((tq, kend), jnp.float32)

        def idx_pair(j, carry):
            iqp = iq_ref[j]
            ik = ikr_ref[0:kend, :]
            acc = bias_ref[:, :kend]
            lgs = [_dot_nt(iqp[:, hh * IDX_DIM:(hh + 1) * IDX_DIM], ik) for hh in range(2)]
            for hh, lg in enumerate(lgs):
                w_h = jnp.sum(jnp.where(lane == HEAD_DIM + 2 * j + hh, iw_ref[...], 0.0), axis=1, keepdims=True)
                acc = acc + jnp.maximum(lg, 0.0) * w_h
            bias_ref[:, :kend] = acc
            return carry

        lax.fori_loop(0, iq_ref.shape[0], idx_pair, 0)
        score = bias_ref[:, :kend] + 0.0
        bits = pltpu.bitcast(score, jnp.int32)
        key = bits ^ ((bits >> 31) & 0x7FFFFFFF)
        key_ref[:, :kend] = jnp.where(causal, key, INT_MIN)

        i16 = jnp.int16
        one_b = jnp.ones((tq, LANES), jnp.bfloat16)
        zero_b = jnp.zeros((tq, LANES), jnp.bfloat16)

        def count16(ref, cand):
            c16 = jnp.broadcast_to(cand, (tq, LANES)).astype(i16)
            acc = zero_b
            for i in range(kend // LANES):
                acc = acc + jnp.where(ref[:, i * LANES:(i + 1) * LANES] >= c16, one_b, zero_b)
            return jnp.sum(acc.astype(jnp.float32), axis=1, keepdims=True)

        def kth_largest16(ref, k):
            t = jnp.where(count16(ref, jnp.zeros((tq, 1), jnp.int32)) >= k, 0, -32768).astype(jnp.int32)

            def bit_step(i, t):
                cand = t | (jnp.int32(1) << (14 - i))
                return jnp.where(count16(ref, cand) >= k, cand, t)

            return lax.fori_loop(0, 15, bit_step, t)

        keyv = key_ref[:, :kend]
        hi_ref[:, :kend] = (keyv >> 16).astype(i16)
        tau_hi = kth_largest16(hi_ref, jnp.full((tq, 1), float(topk), jnp.float32))
        above = jnp.where(tau_hi < 32767, count16(hi_ref, jnp.minimum(tau_hi + 1, 32767)), 0.0)
        lo = ((keyv & 0xFFFF) - 32768).astype(i16)
        same_hi = hi_ref[:, :kend] == jnp.broadcast_to(tau_hi, (tq, kend)).astype(i16)
        lo_ref[:, :kend] = jnp.where(same_hi, lo, -32768)
        tau_lo = kth_largest16(lo_ref, topk - above)
        tau = (tau_hi << 16) | (tau_lo + 32768)
        tau = jnp.maximum(tau, INT_MIN + 1)
        ge = key_ref[:, :kend] >= tau
        n_ge = jnp.sum(jnp.where(ge, 1.0, 0.0), axis=1, keepdims=True)
        bias_ref[:, :kend] = jnp.where(ge, 0.0, -jnp.inf)

        @pl.when(jnp.max(n_ge) > topk)
        def _():
            need = topk - jnp.sum(jnp.where(key_ref[:, :kend] > tau, 1.0, 0.0), axis=1, keepdims=True)
            ri = lax.broadcasted_iota(jnp.int32, (LANES, LANES), 0)
            ci = lax.broadcasted_iota(jnp.int32, (LANES, LANES), 1)
            upper = _bf(jnp.where(ri <= ci, 1.0, 0.0))
            before = jnp.zeros((tq, 1), jnp.float32)
            for c in range(kend // LANES):
                keyc = key_ref[:, c * LANES:(c + 1) * LANES]
                eq = keyc == tau
                rank = _dot(_bf(jnp.where(eq, 1.0, 0.0)), upper) + before
                keep = (keyc > tau) | (eq & (rank <= need))
                bias_ref[:, c * LANES:(c + 1) * LANES] = jnp.where(keep, 0.0, -jnp.inf)
                before = rank[:, LANES - 1:LANES]

    kc = min(kend, ATT_KEY_CHUNK)
    lane_tiles = lambda a: [a[:, i * LANES:(i + 1) * LANES] for i in range(a.shape[1] // LANES)]

    def att_pair(j, carry):
        qp = q_ref[j]
        chunks = range(0, kend, kc)
        s_h = [[_dot_nt(qp[:, hh * HEAD_DIM:(hh + 1) * HEAD_DIM], kr_ref[c:c + kc, :]) + bias_ref[:, c:c + kc]
                for c in chunks] for hh in range(2)]
        outs = []
        for s_c in s_h:
            m = jnp.max(functools.reduce(jnp.maximum, [t for s in s_c for t in lane_tiles(s)]),
                        axis=1, keepdims=True)
            p_c = [jnp.exp2(s - m) for s in s_c]
            l = jnp.sum(sum(t for p in p_c for t in lane_tiles(p)), axis=1, keepdims=True)
            acc = sum(_dot(_bf(p), va_ref[c:c + kc, :]) for p, c in zip(p_c, chunks))
            outs.append(acc / l)
        oacc_ref[j] = jnp.concatenate(outs, axis=1)
        return carry

    lax.fori_loop(0, q_ref.shape[0], att_pair, 0)
    o = jnp.concatenate([oacc_ref[j] for j in range(q_ref.shape[0])], axis=1)
    o = o * lax.rsqrt(jnp.mean(o * o, axis=-1, keepdims=True) + RMS_EPS) * g_ref[...]
    o_ref[...] = _bf(o)


def _key_extents(S, tq, topk):
    out = []
    for v in range(S // tq):
        need = (v + 1) * tq
        if need <= topk:
            out.append((need, False))
        else:
            out.append((min(S, -(-need // (2 * tq)) * 2 * tq), True))
    return out


def _attn_kernel(S, *refs):
    tq = refs[0].shape[1]
    topk = min(TOPK_MAX, S // 4)
    qi = pl.program_id(1)
    extents = _key_extents(S, tq, topk)
    for ext in sorted(set(extents)):
        blocks = [v for v, e in enumerate(extents) if e == ext]
        pl.when((qi >= blocks[0]) & (qi <= blocks[-1]))(functools.partial(_attn_block, *ext, topk, *refs))


def _attention(B, S, q, iq, iw, kr, ikr, va, g):
    tq = min(S, TQ)
    nq = S // tq
    qrow = lambda b, i: (b * nq + i, 0)
    seq = lambda b, i: (b, 0)
    fix = lambda b, i: (0, 0)
    return pl.pallas_call(
        functools.partial(_attn_kernel, S),
        out_shape=jax.ShapeDtypeStruct((B * S, ATT_WIDTH), jnp.bfloat16),
        grid=(B, nq),
        in_specs=[pl.BlockSpec((q.shape[0], tq, LANES), lambda b, i: (0, b * nq + i, 0)),
                  pl.BlockSpec((iq.shape[0], tq, LANES), lambda b, i: (0, b * nq + i, 0)),
                  pl.BlockSpec((tq, LANES), qrow),
                  pl.BlockSpec((S, HEAD_DIM), seq), pl.BlockSpec((S, HEAD_DIM), seq),
                  pl.BlockSpec((S, HEAD_DIM), seq),
                  pl.BlockSpec((1, ATT_WIDTH), fix)],
        out_specs=pl.BlockSpec((tq, ATT_WIDTH), qrow),
        scratch_shapes=[pltpu.VMEM((tq, S), jnp.int32), pltpu.VMEM((tq, S), jnp.float32),
                        pltpu.VMEM((q.shape[0], tq, LANES), jnp.float32),
                        pltpu.VMEM((tq, S), jnp.int16), pltpu.VMEM((tq, S), jnp.int16)],
        compiler_params=_params("parallel", "parallel"),
        name="dsa_attention",
    )(q, iq, iw, kr, ikr, va, g)


def _split3(x):
    hi = _bf(x)
    r1 = x - hi.astype(jnp.float32)
    mid = _bf(r1)
    lo = _bf(r1 - mid.astype(jnp.float32))
    return hi, mid, lo


def _rwkv_kernel(zb_ref, mu_ref, w0_ref, w2_ref, a0_ref, a2_ref, g2_ref, kk_ref, ka_ref, rk_ref,
                 lg_ref, lb_ref, o_ref, prev_ref, state_ref):
    G, C, _ = zb_ref.shape
    W = RWKV_WIDTH
    R = G * C
    H = RWKV_HEADS

    @pl.when(pl.program_id(1) == 0)
    def _():
        prev_ref[...] = jnp.zeros_like(prev_ref)
        state_ref[...] = jnp.zeros_like(state_ref)

    x = zb_ref[...].reshape(R, B_COLS)
    rowi = lax.broadcasted_iota(jnp.int32, (R, 1), 0)
    shifted = pltpu.roll(x, 1, 0)
    for gi in range(G):
        shifted = jnp.where(rowi == gi * C, prev_ref[8 * gi:8 * gi + 1, :], shifted)
        prev_ref[8 * gi:8 * gi + 1, :] = x[gi * C + C - 1:gi * C + C, :]
    xm = x + (shifted - x) * mu_ref[...]
    r = xm[:, 0:W]
    k = xm[:, W:2 * W]
    v = xm[:, 2 * W:3 * W]
    wl = xm[:, 3 * W:3 * W + DECAY_RANK]
    al = xm[:, 3 * W + DECAY_RANK:3 * W + DECAY_RANK + ICLR_RANK]
    gl = xm[:, 3 * W + DECAY_RANK + ICLR_RANK:]

    wpre = w0_ref[...] + _dot(_bf(jnp.tanh(wl)), w2_ref[...])
    nw = -wpre
    w = -(jnp.maximum(nw, 0.0) + jnp.log(1.0 + jnp.exp(-jnp.abs(nw)))) - 0.5
    logd = -jnp.exp(w)
    a = jax.nn.sigmoid(a0_ref[...] + _dot(_bf(al), a2_ref[...]))
    g = _dot(_bf(jax.nn.sigmoid(gl)), g2_ref[...])

    rr = lax.broadcasted_iota(jnp.int32, (R, R), 0)
    cc = lax.broadcasted_iota(jnp.int32, (R, R), 1)
    tri = _bf(jnp.where((rr >= cc) & (rr // C == cc // C), 1.0, 0.0))
    hi, mid, lo = _split3(logd)
    cw = _dot(tri, hi) + _dot(tri, mid) + _dot(tri, lo)
    e_in = jnp.exp(cw)
    e_ex = jnp.exp(cw - logd)
    e_inv = jnp.exp(-cw)

    kk = k * kk_ref[...]
    k2 = k * (1.0 + (a - 1.0) * ka_ref[...])
    rk2 = r * k2 * rk_ref[...]
    ri = lax.broadcasted_iota(jnp.int32, (C, C), 0)
    ci = lax.broadcasted_iota(jnp.int32, (C, C), 1)
    strict = ri > ci
    ri2 = lax.broadcasted_iota(jnp.int32, (C, 2 * C), 0)
    ci2 = lax.broadcasted_iota(jnp.int32, (C, 2 * C), 1)
    incl2 = ri2 >= jnp.where(ci2 >= C, ci2 - C, ci2)
    eye = jnp.where(ri == ci, 1.0, 0.0)

    units = [(gi, h) for gi in range(G) for h in range(H)]

    def cut(arr, u):
        gi, h = u
        return arr[gi * C:(gi + 1) * C, h * HEAD_DIM:(h + 1) * HEAD_DIM]

    def unit_norm(u):
        kkh = cut(kk, u)
        return kkh / jnp.maximum(jnp.sqrt(jnp.sum(kkh * kkh, axis=-1, keepdims=True)), 1e-12)

    kkn = [unit_norm(u) for u in units]
    v_u = [cut(v, u) for u in units]
    at = [_bf(-kn * cut(e_ex, u)) for kn, u in zip(kkn, units)]
    rt = [_bf(cut(r, u) * cut(e_in, u)) for u in units]
    bt = [kn * cut(a, u) * cut(e_inv, u) for kn, u in zip(kkn, units)]
    kt = [cut(k2, u) * cut(e_inv, u) for u in units]
    bk16 = [_bf(jnp.concatenate([b_, k_], axis=0)) for b_, k_ in zip(bt, kt)]
    v16 = [_bf(v_) for v_ in v_u]
    g_a = [_dot_nt(a_, m_) for a_, m_ in zip(at, bk16)]
    l_ab = [jnp.where(strict, g_[:, :C], 0.0) for g_ in g_a]
    a_ak = [_bf(jnp.where(strict, g_[:, C:], 0.0)) for g_ in g_a]
    tinv = [eye + l_ for l_ in l_ab]
    pw = l_ab
    n = 2
    while n < C:
        pw = [_dot(_bf(p_), _bf(p_)) for p_ in pw]
        tinv = [t_ + _dot(_bf(t_), _bf(p_)) for t_, p_ in zip(tinv, pw)]
        n *= 2
    s0 = [state_ref[i] for i in range(len(units))]
    s016 = [_bf(s_) for s_ in s0]
    rhs = [_dot_nt(a_, s_) + _dot(m_, v_) for a_, s_, m_, v_ in zip(at, s016, a_ak, v16)]
    uu = [_dot(_bf(t_), _bf(x_)) for t_, x_ in zip(tinv, rhs)]
    uv16 = [_bf(jnp.concatenate([u_, v_], axis=0)) for u_, v_ in zip(uu, v_u)]
    g_r = [_bf(jnp.where(incl2, _dot_nt(r_, m_), 0.0)) for r_, m_ in zip(rt, bk16)]
    y = [_dot(m_, x_) + _dot_nt(r_, s_) for m_, x_, r_, s_ in zip(g_r, uv16, rt, s016)]
    for i, u in enumerate(units):
        state_ref[i] = (s0[i] + _dot_tn(uv16[i], bk16[i])) * cut(e_in, u)[C - 1:C, :]
    for gi in range(G):
        outs = []
        for h in range(H):
            i = gi * H + h
            u = units[i]
            sl = slice(h * HEAD_DIM, (h + 1) * HEAD_DIM)
            ym = jnp.mean(y[i], axis=-1, keepdims=True)
            yc = y[i] - ym
            yv = jnp.mean(yc * yc, axis=-1, keepdims=True)
            yn = yc * lax.rsqrt(yv + GN_EPS) * lg_ref[:, sl] + lb_ref[:, sl]
            bonus = jnp.sum(cut(rk2, u), axis=-1, keepdims=True) * v_u[i]
            outs.append((yn + bonus) * cut(g, u))
        o_ref[gi] = _bf(jnp.concatenate(outs, axis=1))


def _rwkv(B, S, zb, mu, w0, w2, a0, a2, g2, k_k, k_a, r_k, lnx_g, lnx_b):
    C = min(S, CHUNK)
    G = min(B, RWKV_ROWS)
    blk = lambda b, c: (b, c, 0)
    fix = lambda b, c: (0, 0)
    vec = lambda n: pl.BlockSpec((1, n), fix)
    return pl.pallas_call(
        _rwkv_kernel,
        out_shape=jax.ShapeDtypeStruct((B, S, RWKV_WIDTH), jnp.bfloat16),
        grid=(B // G, S // C),
        in_specs=[pl.BlockSpec((G, C, B_COLS), blk), vec(B_COLS), vec(RWKV_WIDTH),
                  pl.BlockSpec((DECAY_RANK, RWKV_WIDTH), fix), vec(RWKV_WIDTH),
                  pl.BlockSpec((ICLR_RANK, RWKV_WIDTH), fix),
                  pl.BlockSpec((GATE_RANK, RWKV_WIDTH), fix),
                  vec(RWKV_WIDTH), vec(RWKV_WIDTH), vec(RWKV_WIDTH), vec(RWKV_WIDTH),
                  vec(RWKV_WIDTH)],
        out_specs=pl.BlockSpec((G, C, RWKV_WIDTH), blk),
        scratch_shapes=[pltpu.VMEM((8 * G, B_COLS), jnp.float32),
                        pltpu.VMEM((G * RWKV_HEADS, HEAD_DIM, HEAD_DIM), jnp.float32)],
        compiler_params=_params("parallel", "arbitrary"),
        name="rwkv7",
    )(zb.reshape(B, S, B_COLS), mu, w0, w2, a0, a2, g2, k_k, k_a, r_k, lnx_g, lnx_b
      ).reshape(B * S, RWKV_WIDTH)


def _pool_kernel(zc_ref, w_ref, sc_ref, o_ref):
    S = zc_ref.shape[0]
    x = zc_ref[...]
    row = lax.broadcasted_iota(jnp.int32, (S, 1), 0)
    lane_grp = lax.broadcasted_iota(jnp.int32, (1, POOL_WIDTH), 1) // POOL_GROUP

    def lag(y, n):
        return jnp.where(row >= n, pltpu.roll(y, n, 0), 0.0)

    w2 = x + lag(x, 1)
    w4 = w2 + lag(w2, 2)
    w8 = w4 + lag(w4, 4)
    w16 = w8 + lag(w8, 8)
    wsum = jnp.where(lane_grp == 0, w2, jnp.where(lane_grp == 1, w4, jnp.where(lane_grp == 2, w8, w16)))
    win = jnp.where(lane_grp == 0, 2, jnp.where(lane_grp == 1, 4, jnp.where(lane_grp == 2, 8, 16)))
    count = jnp.minimum(row + 1, win).astype(jnp.float32)
    pooled = wsum / count - x
    o_ref[...] = _bf(_dot(_bf(pooled), w_ref[...]) * sc_ref[...])


def _pool(B, S, zc, w_bd, scale):
    return pl.pallas_call(
        _pool_kernel,
        out_shape=jax.ShapeDtypeStruct((B * S, POOL_WIDTH), jnp.bfloat16),
        grid=(B,),
        in_specs=[pl.BlockSpec((S, POOL_WIDTH), lambda b: (b, 0)),
                  pl.BlockSpec((POOL_WIDTH, POOL_WIDTH), lambda b: (0, 0)),
                  pl.BlockSpec((1, POOL_WIDTH), lambda b: (0, 0))],
        out_specs=pl.BlockSpec((S, POOL_WIDTH), lambda b: (b, 0)),
        compiler_params=_params("parallel"),
        name="pool",
    )(zc, w_bd, scale)


def _outmlp_kernel(alpha, oa_ref, ob_ref, oc_ref, x_ref, p_ref, wa_ref, wb_ref, wc_ref, g1_ref, b1_ref,
                   w1_ref, w2_ref, pg_ref, pgb_ref, pp_ref, g2_ref, b2_ref, o_ref):
    tm = x_ref.shape[0]
    halves = [slice(0, tm // 2), slice(tm // 2, tm)]
    mix = [_dot(oa_ref[h, :], wa_ref[...]) + _dot(ob_ref[h, :], wb_ref[...]) + _dot(oc_ref[h, :], wc_ref[...])
           for h in halves]
    x1 = [_ln(alpha * x_ref[h, :] + m, g1_ref[...], b1_ref[...]) for h, m in zip(halves, mix)]
    xb = [_bf(v) for v in x1]
    gate = [jax.nn.sigmoid(_dot(b_, pg_ref[...]) + pgb_ref[...]) for b_ in xb]
    pe = [_dot(_bf(p_ref[0, h, :]), pp_ref[...]) for h in halves]
    acc = [alpha * v + g_ * e_ for v, g_, e_ in zip(x1, gate, pe)]
    xb_all = jnp.concatenate(xb, axis=0)
    nk = D_FF // TF_MLP
    for kf in range(nk):
        cols = slice(kf * TF_MLP, (kf + 1) * TF_MLP)
        hk = jnp.maximum(_dot(xb_all, w1_ref[:, cols]), 0.0)
        h2 = _bf(hk * hk)
        if kf < nk - 1:
            part = _dot(h2, w2_ref[cols, :])
            acc = [a + part[h, :] for a, h in zip(acc, halves)]
        else:
            acc = [a + _dot(h2[h, :], w2_ref[cols, :]) for a, h in zip(acc, halves)]
    for h, a in zip(halves, acc):
        o_ref[h, :] = _ln(a, g2_ref[...], b2_ref[...])


def _outmlp(alpha, layer, oa, ob, oc, x, p, wa, wb, wc, g1, b1, w1, w2, pg, pgb, pp, g2, b2):
    T = x.shape[0]
    tm = min(T, TM_MLP)
    row = lambda i: (i, 0)
    fix = lambda i: (0, 0)
    once = pl.Buffered(1)
    wspec = lambda r, c: pl.BlockSpec((r, c), fix, pipeline_mode=once)
    vec = pl.BlockSpec((1, D_MODEL), fix)
    return pl.pallas_call(
        functools.partial(_outmlp_kernel, alpha),
        out_shape=jax.ShapeDtypeStruct((T, D_MODEL), jnp.float32),
        grid=(T // tm,),
        in_specs=[pl.BlockSpec((tm, ATT_WIDTH), row), pl.BlockSpec((tm, RWKV_WIDTH), row),
                  pl.BlockSpec((tm, POOL_WIDTH), row), pl.BlockSpec((tm, D_MODEL), row),
                  pl.BlockSpec((1, tm, PE_DIM), lambda i: (layer, i, 0)),
                  wspec(ATT_WIDTH, D_MODEL), wspec(RWKV_WIDTH, D_MODEL), wspec(POOL_WIDTH, D_MODEL), vec, vec,
                  wspec(D_MODEL, D_FF), wspec(D_FF, D_MODEL), wspec(D_MODEL, D_MODEL), vec,
                  wspec(PE_DIM, D_MODEL), vec, vec],
        out_specs=pl.BlockSpec((tm, D_MODEL), row),
        compiler_params=_params("parallel"),
        name="outproj_mlp",
    )(oa, ob, oc, x, p, wa, wb, wc, g1, b1, w1, w2, pg, pgb, pp, g2, b2)


def _rot_partner(w):
    half = HEAD_DIM // 2
    return jnp.concatenate([-w[..., half:], w[..., :half]], axis=-1)


def _per_head_partner(w, heads):
    r = w.reshape(w.shape[0], heads, HEAD_DIM)
    return _rot_partner(r).reshape(w.shape)


def _arrange_w_in(w):
    cq = w[:, 0:256]
    ka = w[:, 256:320]
    va = w[:, 320:384]
    ik = w[:, 384:448]
    iw = w[:, 448:456]
    rest = w[:, 456:]
    pad = jnp.zeros((w.shape[0], LANES - HEAD_DIM - IDX_HEADS), w.dtype)
    return jnp.concatenate([cq, ka, ik, _rot_partner(ka), _rot_partner(ik), va, iw, pad, rest], axis=1)


def kernel(x, p, positions, ln_emb_g, ln_emb_b, w_in, w_uq, w_uqi, attn_norm_g, rwkv_mu, rwkv_w0, rwkv_w2, rwkv_a0, rwkv_a2, rwkv_g2, rwkv_k_k, rwkv_k_a, rwkv_r_k, rwkv_lnx_g, rwkv_lnx_b, pool_w, pool_scale, w_out, ln1_g, ln1_b, mlp_w1, mlp_w2, pe_proj, pe_gate, pe_gate_b, ln2_g, ln2_b):
    B, S, _ = x.shape
    T = B * S
    depth = w_in.shape[0]
    alpha = (2 * depth) ** 0.25
    bf = lambda a: a.astype(jnp.bfloat16)
    rowv = lambda a: a.reshape(1, -1)

    cos, sin = _rope_tables(positions.astype(jnp.float32).reshape(T // LANES, LANES))
    xs = x.reshape(T, D_MODEL)
    for i in range(depth):
        win = bf(_arrange_w_in(w_in[i]))
        wq = w_uq[i] * (HEAD_DIM ** -0.5)
        wi = w_uqi[i] * (IDX_DIM ** -0.5)
        res = _inproj(xs, rowv(ln_emb_g), rowv(ln_emb_b), cos, sin, win,
                      bf(wq), bf(_per_head_partner(wq, ATT_HEADS)),
                      bf(wi), bf(_per_head_partner(wi, IDX_HEADS)), apply_ln=(i == 0))
        if i == 0:
            xs, res = res[0], res[1:]
        q, iq, iw, kr, ikr, va, zb, zc = res
        o_a = _attention(B, S, q, iq, iw, kr, ikr, va, rowv(attn_norm_g[i]))
        o_b = _rwkv(B, S, zb, rowv(rwkv_mu[i]), rowv(rwkv_w0[i]), bf(rwkv_w2[i]), rowv(rwkv_a0[i]),
                    bf(rwkv_a2[i]), bf(rwkv_g2[i]), rowv(rwkv_k_k[i]), rowv(rwkv_k_a[i]),
                    rowv(rwkv_r_k[i]), rowv(rwkv_lnx_g[i]), rowv(rwkv_lnx_b[i]))
        w_bd = jax.scipy.linalg.block_diag(*[pool_w[i, gi] for gi in range(len(POOL_WINDOWS))])
        o_c = _pool(B, S, zc, bf(w_bd), rowv(pool_scale[i]))
        wo = bf(w_out[i])
        xs = _outmlp(alpha, i, o_a, o_b, o_c, xs, p.reshape(depth, T, PE_DIM), wo[:ATT_WIDTH],
                     wo[ATT_WIDTH:ATT_WIDTH + RWKV_WIDTH], wo[ATT_WIDTH + RWKV_WIDTH:], rowv(ln1_g[i]),
                     rowv(ln1_b[i]), bf(mlp_w1[i]), bf(mlp_w2[i]), bf(pe_gate[i]), rowv(pe_gate_b[i]),
                     bf(pe_proj[i]), rowv(ln2_g[i]), rowv(ln2_b[i]))
    return xs.reshape(B, S, D_MODEL)
```

```python
import functools

import numpy as np
import jax
import jax.numpy as jnp
from jax import lax
from jax.experimental import pallas as pl
from jax.experimental.pallas import tpu as pltpu

D_MODEL = 1024
PE_DIM = 256
HEAD_DIM = 64
ROPE_THETA = 10000.0
ATT_WIDTH = 384
ATT_HEADS = 6
Q_RANK = 256
IDX_HEADS = 8
IDX_DIM = 64
TOPK_MAX = 256
RWKV_WIDTH = 384
RWKV_HEADS = 6
DECAY_RANK = 64
ICLR_RANK = 64
GATE_RANK = 128
GN_EPS = 64e-5
POOL_WINDOWS = (2, 4, 8, 16)
POOL_WIDTH = 256
POOL_GROUP = 64
B_COLS = 3 * RWKV_WIDTH + DECAY_RANK + ICLR_RANK + GATE_RANK
D_FF = 4 * D_MODEL
LN_EPS = 1e-5
RMS_EPS = 1e-6

COL_CQ = 0
COL_KIK = 256
COL_KIK_P = 384
COL_VIW = 512
COL_B = 640
COL_C = COL_B + B_COLS
N_IN_P = COL_C + POOL_WIDTH

LANES = 128
INT_MIN = -2 ** 31
LOG2E = 1.4426950408889634
VMEM_LIMIT = 56 * 1024 * 1024

TM_PROJ = 512
TQ = 256
ATT_KEY_CHUNK = 512
CHUNK = 64
RWKV_ROWS = 4
TM_MLP = 512
TF_MLP = 1024

_NT = (((1,), (1,)), ((), ()))
_TN = (((0,), (0,)), ((), ()))


def _params(*sem):
    return pltpu.CompilerParams(dimension_semantics=sem, vmem_limit_bytes=VMEM_LIMIT)


def _ln(x, g, b):
    mu = jnp.mean(x, axis=-1, keepdims=True)
    xc = x - mu
    var = jnp.mean(xc * xc, axis=-1, keepdims=True)
    return xc * lax.rsqrt(var + LN_EPS) * g + b


def _bf(x):
    return x.astype(jnp.bfloat16)


def _dot(a, b):
    return jnp.dot(a, b, preferred_element_type=jnp.float32)


def _dot_nt(a, b):
    return lax.dot_general(a, b, _NT, preferred_element_type=jnp.float32)


def _dot_tn(a, b):
    return lax.dot_general(a, b, _TN, preferred_element_type=jnp.float32)


def _rope_kernel(pos_ref, inv_ref, cos_ref, sin_ref):
    for r in range(pos_ref.shape[0]):
        col = jnp.broadcast_to(pos_ref[r:r + 1, :], (LANES, LANES)).T
        ang = col * inv_ref[...]
        cos_ref[r * LANES:(r + 1) * LANES, :] = jnp.cos(ang)
        sin_ref[r * LANES:(r + 1) * LANES, :] = jnp.sin(ang)


def _rope_tables(pos_f32):
    T = pos_f32.shape[0] * LANES
    rows = min(pos_f32.shape[0], 16)
    inv = ROPE_THETA ** (-np.arange(0, HEAD_DIM, 2, dtype=np.float32) / HEAD_DIM)
    inv128 = jnp.asarray(np.tile(inv, LANES // (HEAD_DIM // 2))[None, :], jnp.float32)
    return pl.pallas_call(
        _rope_kernel,
        out_shape=(jax.ShapeDtypeStruct((T, LANES), jnp.float32),) * 2,
        grid=(pos_f32.shape[0] // rows,),
        in_specs=[pl.BlockSpec((rows, LANES), lambda i: (i, 0)),
                  pl.BlockSpec((1, LANES), lambda i: (0, 0))],
        out_specs=(pl.BlockSpec((rows * LANES, LANES), lambda i: (i, 0)),) * 2,
        compiler_params=_params("parallel"),
        name="rope_tables",
    )(pos_f32, inv128)


def _inproj_kernel(apply_ln, x_ref, g_ref, b_ref, cos_ref, sin_ref, win_ref, wq_ref, wqp_ref,
                   wi_ref, wip_ref, *out_refs):
    if apply_ln:
        xn_ref, qt_ref, iqt_ref, iwt_ref, kr_ref, ikr_ref, vat_ref, zb_ref, zc_ref = out_refs
        x = _ln(x_ref[...], g_ref[...], b_ref[...])
        xn_ref[...] = x
    else:
        qt_ref, iqt_ref, iwt_ref, kr_ref, ikr_ref, vat_ref, zb_ref, zc_ref = out_refs
        x = x_ref[...]
    z = _dot(_bf(x), win_ref[...])
    cos = cos_ref[...]
    sin = sin_ref[...]
    cq = _bf(z[:, COL_CQ:COL_CQ + Q_RANK])
    cos3 = jnp.concatenate([cos] * 3, axis=1)
    sin3 = jnp.concatenate([sin] * 3, axis=1)
    qt_ref[...] = _bf(((_dot(cq, wq_ref[...]) * cos3 + _dot(cq, wqp_ref[...]) * sin3) * LOG2E).T)
    cos4 = jnp.concatenate([cos] * 4, axis=1)
    sin4 = jnp.concatenate([sin] * 4, axis=1)
    iqt_ref[...] = _bf((_dot(cq, wi_ref[...]) * cos4 + _dot(cq, wip_ref[...]) * sin4).T)
    kik = z[:, COL_KIK:COL_KIK + LANES] * cos + z[:, COL_KIK_P:COL_KIK_P + LANES] * sin
    kr_ref[...] = _bf(kik[:, :HEAD_DIM])
    ikr_ref[...] = _bf(kik[:, HEAD_DIM:])
    viw_t = z[:, COL_VIW:COL_VIW + LANES].T
    vat_ref[...] = _bf(viw_t[:HEAD_DIM])
    iwt_ref[...] = viw_t[HEAD_DIM:HEAD_DIM + IDX_HEADS] * (IDX_HEADS ** -0.5)
    zb_ref[...] = z[:, COL_B:COL_B + B_COLS]
    zc_ref[...] = z[:, COL_C:COL_C + POOL_WIDTH]


def _inproj(x, g, b, cos, sin, win, wq, wqp, wi, wip, apply_ln):
    T = x.shape[0]
    tm = min(T, TM_PROJ)
    row = lambda i: (i, 0)
    fix = lambda i: (0, 0)
    bf16, f32 = jnp.bfloat16, jnp.float32
    col = lambda i: (0, i)
    tok = lambda n, dt: (jax.ShapeDtypeStruct((T, n), dt), pl.BlockSpec((tm, n), row))
    feat = lambda n, dt: (jax.ShapeDtypeStruct((n, T), dt), pl.BlockSpec((n, tm), col))
    outs = [feat(ATT_WIDTH, bf16), feat(IDX_HEADS * IDX_DIM, bf16), feat(IDX_HEADS, f32), tok(HEAD_DIM, bf16),
            tok(HEAD_DIM, bf16), feat(HEAD_DIM, bf16), tok(B_COLS, f32), tok(POOL_WIDTH, f32)]
    shapes = [o[0] for o in outs]
    specs = [o[1] for o in outs]
    if apply_ln:
        shapes = [jax.ShapeDtypeStruct((T, D_MODEL), f32)] + shapes
        specs = [pl.BlockSpec((tm, D_MODEL), row)] + specs
    return pl.pallas_call(
        functools.partial(_inproj_kernel, apply_ln),
        out_shape=tuple(shapes),
        grid=(T // tm,),
        in_specs=[pl.BlockSpec((tm, D_MODEL), row),
                  pl.BlockSpec((1, D_MODEL), fix), pl.BlockSpec((1, D_MODEL), fix),
                  pl.BlockSpec((tm, LANES), row), pl.BlockSpec((tm, LANES), row),
                  pl.BlockSpec((D_MODEL, N_IN_P), fix),
                  pl.BlockSpec((Q_RANK, ATT_WIDTH), fix), pl.BlockSpec((Q_RANK, ATT_WIDTH), fix),
                  pl.BlockSpec((Q_RANK, IDX_HEADS * IDX_DIM), fix),
                  pl.BlockSpec((Q_RANK, IDX_HEADS * IDX_DIM), fix)],
        out_specs=tuple(specs),
        compiler_params=_params("parallel"),
        name="inproj",
    )(x, g, b, cos, sin, win, wq, wqp, wi, wip)


def _attn_block(kend, search, topk, qt_ref, iqt_ref, iwt_ref, kr_ref, ikr_ref, vat_ref, g_ref, o_ref,
                key_ref, bias_ref, ot_ref, hi_ref, lo_ref):
    tq = qt_ref.shape[1]
    q_idx = pl.program_id(1) * tq + lax.broadcasted_iota(jnp.int32, (1, tq), 1)
    k_idx = lax.broadcasted_iota(jnp.int32, (kend, 1), 0)
    causal = k_idx <= q_idx

    if not search:
        bias_ref[:kend, :] = jnp.where(causal, 0.0, -jnp.inf)
    else:
        bias_ref[:kend, :] = jnp.zeros((kend, tq), jnp.float32)

        def idx_pair(j, carry):
            ik = ikr_ref[0:kend, :]
            acc = bias_ref[:kend, :]
            heads = [2 * j, 2 * j + 1]
            lgs = [_dot(ik, iqt_ref[pl.ds(pl.multiple_of(h * IDX_DIM, IDX_DIM), IDX_DIM), :]) for h in heads]
            for h, lg in zip(heads, lgs):
                acc = acc + jnp.maximum(lg, 0.0) * iwt_ref[pl.ds(h, 1), :]
            bias_ref[:kend, :] = acc
            return carry

        lax.fori_loop(0, IDX_HEADS // 2, idx_pair, 0)
        score = bias_ref[:kend, :] + 0.0
        bits = pltpu.bitcast(score, jnp.int32)
        key = bits ^ ((bits >> 31) & 0x7FFFFFFF)
        key_ref[:kend, :] = jnp.where(causal, key, INT_MIN)

        i16 = jnp.int16
        pack = 16
        one_b = jnp.ones((pack, tq), jnp.bfloat16)
        zero_b = jnp.zeros((pack, tq), jnp.bfloat16)

        def count16(ref, cand):
            c16 = jnp.broadcast_to(cand, (pack, tq)).astype(i16)
            acc = zero_b
            for i in range(0, kend, pack):
                acc = acc + jnp.where(ref[i:i + pack, :] >= c16, one_b, zero_b)
            return jnp.sum(acc.astype(jnp.float32), axis=0, keepdims=True)

        def kth_largest16(ref, k):
            t = jnp.where(count16(ref, jnp.zeros((1, tq), jnp.int32)) >= k, 0, -32768).astype(jnp.int32)

            def bit_step(i, t):
                cand = t | (jnp.int32(1) << (14 - i))
                return jnp.where(count16(ref, cand) >= k, cand, t)

            return lax.fori_loop(0, 15, bit_step, t)

        keyv = key_ref[:kend, :]
        hi_ref[:kend, :] = (keyv >> 16).astype(i16)
        tau_hi = kth_largest16(hi_ref, jnp.full((1, tq), float(topk), jnp.float32))
        above = jnp.where(tau_hi < 32767, count16(hi_ref, jnp.minimum(tau_hi + 1, 32767)), 0.0)
        lo = ((keyv & 0xFFFF) - 32768).astype(i16)
        same_hi = hi_ref[:kend, :] == jnp.broadcast_to(tau_hi, (kend, tq)).astype(i16)
        lo_ref[:kend, :] = jnp.where(same_hi, lo, -32768)
        tau_lo = kth_largest16(lo_ref, topk - above)
        tau = (tau_hi << 16) | (tau_lo + 32768)
        tau = jnp.maximum(tau, INT_MIN + 1)
        ge = key_ref[:kend, :] >= tau
        n_ge = jnp.sum(jnp.where(ge, 1.0, 0.0), axis=0, keepdims=True)
        bias_ref[:kend, :] = jnp.where(ge, 0.0, -jnp.inf)

        @pl.when(jnp.max(n_ge) > topk)
        def _():
            need = topk - jnp.sum(jnp.where(key_ref[:kend, :] > tau, 1.0, 0.0), axis=0, keepdims=True)
            ri = lax.broadcasted_iota(jnp.int32, (LANES, LANES), 0)
            ci = lax.broadcasted_iota(jnp.int32, (LANES, LANES), 1)
            lower = _bf(jnp.where(ri >= ci, 1.0, 0.0))
            before = jnp.zeros((1, tq), jnp.float32)
            for c in range(0, kend, LANES):
                keyc = key_ref[c:c + LANES, :]
                eq = keyc == tau
                rank = _dot(lower, _bf(jnp.where(eq, 1.0, 0.0))) + before
                keep = (keyc > tau) | (eq & (rank <= need))
                bias_ref[c:c + LANES, :] = jnp.where(keep, 0.0, -jnp.inf)
                before = rank[LANES - 1:LANES, :]

    kc = min(kend, ATT_KEY_CHUNK)
    chunks = range(0, kend, kc)

    def att_pair(j, carry):
        heads = [2 * j, 2 * j + 1]
        qs = [qt_ref[pl.ds(pl.multiple_of(h * HEAD_DIM, HEAD_DIM), HEAD_DIM), :] for h in heads]
        s_h = [[_dot(kr_ref[c:c + kc, :], q_) + bias_ref[c:c + kc, :] for c in chunks] for q_ in qs]
        for h, s_c in zip(heads, s_h):
            m = jnp.max(functools.reduce(jnp.maximum, s_c), axis=0, keepdims=True)
            p_c = [jnp.exp2(s - m) for s in s_c]
            l = jnp.sum(sum(p_c), axis=0, keepdims=True)
            acc = sum(_dot(vat_ref[:, c:c + kc], _bf(p)) for p, c in zip(p_c, chunks))
            ot_ref[pl.ds(pl.multiple_of(h * HEAD_DIM, HEAD_DIM), HEAD_DIM), :] = acc / l
        return carry

    lax.fori_loop(0, ATT_HEADS // 2, att_pair, 0)
    ot = ot_ref[...]
    ot = ot * lax.rsqrt(jnp.mean(ot * ot, axis=0, keepdims=True) + RMS_EPS) * g_ref[...]
    o_ref[...] = _bf(ot.T)


def _key_extents(S, tq, topk):
    out = []
    for v in range(S // tq):
        need = (v + 1) * tq
        if need <= topk:
            out.append((need, False))
        else:
            out.append((min(S, -(-need // (2 * tq)) * 2 * tq), True))
    return out


def _attn_kernel(S, *refs):
    tq = refs[0].shape[1]
    topk = min(TOPK_MAX, S // 4)
    qi = pl.program_id(1)
    extents = _key_extents(S, tq, topk)
    for ext in sorted(set(extents)):
        blocks = [v for v, e in enumerate(extents) if e == ext]
        pl.when((qi >= blocks[0]) & (qi <= blocks[-1]))(functools.partial(_attn_block, *ext, topk, *refs))


def _attention(B, S, qt, iqt, iwt, kr, ikr, vat, g):
    tq = min(S, TQ)
    nq = S // tq
    qcol = lambda b, i: (0, b * nq + i)
    seq = lambda b, i: (b, 0)
    return pl.pallas_call(
        functools.partial(_attn_kernel, S),
        out_shape=jax.ShapeDtypeStruct((B * S, ATT_WIDTH), jnp.bfloat16),
        grid=(B, nq),
        in_specs=[pl.BlockSpec((ATT_WIDTH, tq), qcol),
                  pl.BlockSpec((IDX_HEADS * IDX_DIM, tq), qcol),
                  pl.BlockSpec((IDX_HEADS, tq), qcol),
                  pl.BlockSpec((S, HEAD_DIM), seq), pl.BlockSpec((S, HEAD_DIM), seq),
                  pl.BlockSpec((HEAD_DIM, S), lambda b, i: (0, b)),
                  pl.BlockSpec((ATT_WIDTH, 1), lambda b, i: (0, 0))],
        out_specs=pl.BlockSpec((tq, ATT_WIDTH), lambda b, i: (b * nq + i, 0)),
        scratch_shapes=[pltpu.VMEM((S, tq), jnp.int32), pltpu.VMEM((S, tq), jnp.float32),
                        pltpu.VMEM((ATT_WIDTH, tq), jnp.float32),
                        pltpu.VMEM((S, tq), jnp.int16), pltpu.VMEM((S, tq), jnp.int16)],
        compiler_params=_params("parallel", "parallel"),
        name="dsa_attention",
    )(qt, iqt, iwt, kr, ikr, vat, g)


def _split3(x):
    hi = _bf(x)
    r1 = x - hi.astype(jnp.float32)
    mid = _bf(r1)
    lo = _bf(r1 - mid.astype(jnp.float32))
    return hi, mid, lo


def _rwkv_kernel(zb_ref, mu_ref, w0_ref, w2_ref, a0_ref, a2_ref, g2_ref, kk_ref, ka_ref, rk_ref,
                 lg_ref, lb_ref, o_ref, prev_ref, state_ref):
    G, C, _ = zb_ref.shape
    W = RWKV_WIDTH
    R = G * C
    H = RWKV_HEADS

    @pl.when(pl.program_id(1) == 0)
    def _():
        prev_ref[...] = jnp.zeros_like(prev_ref)
        state_ref[...] = jnp.zeros_like(state_ref)

    x = zb_ref[...].reshape(R, B_COLS)
    rowi = lax.broadcasted_iota(jnp.int32, (R, 1), 0)
    shifted = pltpu.roll(x, 1, 0)
    for gi in range(G):
        shifted = jnp.where(rowi == gi * C, prev_ref[8 * gi:8 * gi + 1, :], shifted)
        prev_ref[8 * gi:8 * gi + 1, :] = x[gi * C + C - 1:gi * C + C, :]
    xm = x + (shifted - x) * mu_ref[...]
    r = xm[:, 0:W]
    k = xm[:, W:2 * W]
    v = xm[:, 2 * W:3 * W]
    wl = xm[:, 3 * W:3 * W + DECAY_RANK]
    al = xm[:, 3 * W + DECAY_RANK:3 * W + DECAY_RANK + ICLR_RANK]
    gl = xm[:, 3 * W + DECAY_RANK + ICLR_RANK:]

    wpre = w0_ref[...] + _dot(_bf(jnp.tanh(wl)), w2_ref[...])
    nw = -wpre
    w = -(jnp.maximum(nw, 0.0) + jnp.log(1.0 + jnp.exp(-jnp.abs(nw)))) - 0.5
    logd = -jnp.exp(w)
    a = jax.nn.sigmoid(a0_ref[...] + _dot(_bf(al), a2_ref[...]))
    g = _dot(_bf(jax.nn.sigmoid(gl)), g2_ref[...])

    rr = lax.broadcasted_iota(jnp.int32, (R, R), 0)
    cc = lax.broadcasted_iota(jnp.int32, (R, R), 1)
    tri = _bf(jnp.where((rr >= cc) & (rr // C == cc // C), 1.0, 0.0))
    hi, mid, lo = _split3(logd)
    cw = _dot(tri, hi) + _dot(tri, mid) + _dot(tri, lo)
    e_in = jnp.exp(cw)
    e_ex = jnp.exp(cw - logd)
    e_inv = jnp.exp(-cw)

    kk = k * kk_ref[...]
    k2 = k * (1.0 + (a - 1.0) * ka_ref[...])
    rk2 = r * k2 * rk_ref[...]
    ri = lax.broadcasted_iota(jnp.int32, (C, C), 0)
    ci = lax.broadcasted_iota(jnp.int32, (C, C), 1)
    strict = ri > ci
    ri2 = lax.broadcasted_iota(jnp.int32, (C, 2 * C), 0)
    ci2 = lax.broadcasted_iota(jnp.int32, (C, 2 * C), 1)
    incl2 = ri2 >= jnp.where(ci2 >= C, ci2 - C, ci2)
    eye = jnp.where(ri == ci, 1.0, 0.0)

    units = [(gi, h) for gi in range(G) for h in range(H)]

    def cut(arr, u):
        gi, h = u
        return arr[gi * C:(gi + 1) * C, h * HEAD_DIM:(h + 1) * HEAD_DIM]

    def unit_norm(u):
        kkh = cut(kk, u)
        return kkh / jnp.maximum(jnp.sqrt(jnp.sum(kkh * kkh, axis=-1, keepdims=True)), 1e-12)

    kkn = [unit_norm(u) for u in units]
    v_u = [cut(v, u) for u in units]
    at = [_bf(-kn * cut(e_ex, u)) for kn, u in zip(kkn, units)]
    rt = [_bf(cut(r, u) * cut(e_in, u)) for u in units]
    bt = [kn * cut(a, u) * cut(e_inv, u) for kn, u in zip(kkn, units)]
    kt = [cut(k2, u) * cut(e_inv, u) for u in units]
    bk16 = [_bf(jnp.concatenate([b_, k_], axis=0)) for b_, k_ in zip(bt, kt)]
    v16 = [_bf(v_) for v_ in v_u]
    g_a = [_dot_nt(a_, m_) for a_, m_ in zip(at, bk16)]
    l_ab = [jnp.where(strict, g_[:, :C], 0.0) for g_ in g_a]
    a_ak = [_bf(jnp.where(strict, g_[:, C:], 0.0)) for g_ in g_a]
    tinv = [eye + l_ for l_ in l_ab]
    pw = l_ab
    n = 2
    while n < C:
        pw = [_dot(_bf(p_), _bf(p_)) for p_ in pw]
        tinv = [t_ + _dot(_bf(t_), _bf(p_)) for t_, p_ in zip(tinv, pw)]
        n *= 2
    s0 = [state_ref[i] for i in range(len(units))]
    s016 = [_bf(s_) for s_ in s0]
    rhs = [_dot_nt(a_, s_) + _dot(m_, v_) for a_, s_, m_, v_ in zip(at, s016, a_ak, v16)]
    uu = [_dot(_bf(t_), _bf(x_)) for t_, x_ in zip(tinv, rhs)]
    uv16 = [_bf(jnp.concatenate([u_, v_], axis=0)) for u_, v_ in zip(uu, v_u)]
    g_r = [_bf(jnp.where(incl2, _dot_nt(r_, m_), 0.0)) for r_, m_ in zip(rt, bk16)]
    y = [_dot(m_, x_) + _dot_nt(r_, s_) for m_, x_, r_, s_ in zip(g_r, uv16, rt, s016)]
    for i, u in enumerate(units):
        state_ref[i] = (s0[i] + _dot_tn(uv16[i], bk16[i])) * cut(e_in, u)[C - 1:C, :]
    for gi in range(G):
        outs = []
        for h in range(H):
            i = gi * H + h
            u = units[i]
            sl = slice(h * HEAD_DIM, (h + 1) * HEAD_DIM)
            ym = jnp.mean(y[i], axis=-1, keepdims=True)
            yc = y[i] - ym
            yv = jnp.mean(yc * yc, axis=-1, keepdims=True)
            yn = yc * lax.rsqrt(yv + GN_EPS) * lg_ref[:, sl] + lb_ref[:, sl]
            bonus = jnp.sum(cut(rk2, u), axis=-1, keepdims=True) * v_u[i]
            outs.append((yn + bonus) * cut(g, u))
        o_ref[gi] = _bf(jnp.concatenate(outs, axis=1))


def _rwkv(B, S, zb, mu, w0, w2, a0, a2, g2, k_k, k_a, r_k, lnx_g, lnx_b):
    C = min(S, CHUNK)
    G = min(B, RWKV_ROWS)
    blk = lambda b, c: (b, c, 0)
    fix = lambda b, c: (0, 0)
    vec = lambda n: pl.BlockSpec((1, n), fix)
    return pl.pallas_call(
        _rwkv_kernel,
        out_shape=jax.ShapeDtypeStruct((B, S, RWKV_WIDTH), jnp.bfloat16),
        grid=(B // G, S // C),
        in_specs=[pl.BlockSpec((G, C, B_COLS), blk), vec(B_COLS), vec(RWKV_WIDTH),
                  pl.BlockSpec((DECAY_RANK, RWKV_WIDTH), fix), vec(RWKV_WIDTH),
                  pl.BlockSpec((ICLR_RANK, RWKV_WIDTH), fix),
                  pl.BlockSpec((GATE_RANK, RWKV_WIDTH), fix),
                  vec(RWKV_WIDTH), vec(RWKV_WIDTH), vec(RWKV_WIDTH), vec(RWKV_WIDTH),
                  vec(RWKV_WIDTH)],
        out_specs=pl.BlockSpec((G, C, RWKV_WIDTH), blk),
        scratch_shapes=[pltpu.VMEM((8 * G, B_COLS), jnp.float32),
                        pltpu.VMEM((G * RWKV_HEADS, HEAD_DIM, HEAD_DIM), jnp.float32)],
        compiler_params=_params("parallel", "arbitrary"),
        name="rwkv7",
    )(zb.reshape(B, S, B_COLS), mu, w0, w2, a0, a2, g2, k_k, k_a, r_k, lnx_g, lnx_b
      ).reshape(B * S, RWKV_WIDTH)


def _pool_kernel(zc_ref, w_ref, sc_ref, o_ref):
    S = zc_ref.shape[0]
    x = zc_ref[...]
    row = lax.broadcasted_iota(jnp.int32, (S, 1), 0)
    lane_grp = lax.broadcasted_iota(jnp.int32, (1, POOL_WIDTH), 1) // POOL_GROUP

    def lag(y, n):
        return jnp.where(row >= n, pltpu.roll(y, n, 0), 0.0)

    w2 = x + lag(x, 1)
    w4 = w2 + lag(w2, 2)
    w8 = w4 + lag(w4, 4)
    w16 = w8 + lag(w8, 8)
    wsum = jnp.where(lane_grp == 0, w2, jnp.where(lane_grp == 1, w4, jnp.where(lane_grp == 2, w8, w16)))
    win = jnp.where(lane_grp == 0, 2, jnp.where(lane_grp == 1, 4, jnp.where(lane_grp == 2, 8, 16)))
    count = jnp.minimum(row + 1, win).astype(jnp.float32)
    pooled = wsum / count - x
    o_ref[...] = _bf(_dot(_bf(pooled), w_ref[...]) * sc_ref[...])


def _pool(B, S, zc, w_bd, scale):
    return pl.pallas_call(
        _pool_kernel,
        out_shape=jax.ShapeDtypeStruct((B * S, POOL_WIDTH), jnp.bfloat16),
        grid=(B,),
        in_specs=[pl.BlockSpec((S, POOL_WIDTH), lambda b: (b, 0)),
                  pl.BlockSpec((POOL_WIDTH, POOL_WIDTH), lambda b: (0, 0)),
                  pl.BlockSpec((1, POOL_WIDTH), lambda b: (0, 0))],
        out_specs=pl.BlockSpec((S, POOL_WIDTH), lambda b: (b, 0)),
        compiler_params=_params("parallel"),
        name="pool",
    )(zc, w_bd, scale)


def _outmlp_kernel(alpha, oa_ref, ob_ref, oc_ref, x_ref, p_ref, wa_ref, wb_ref, wc_ref, g1_ref, b1_ref,
                   w1_ref, w2_ref, pg_ref, pgb_ref, pp_ref, g2_ref, b2_ref, o_ref):
    tm = x_ref.shape[0]
    halves = [slice(0, tm // 2), slice(tm // 2, tm)]
    mix = [_dot(oa_ref[h, :], wa_ref[...]) + _dot(ob_ref[h, :], wb_ref[...]) + _dot(oc_ref[h, :], wc_ref[...])
           for h in halves]
    x1 = [_ln(alpha * x_ref[h, :] + m, g1_ref[...], b1_ref[...]) for h, m in zip(halves, mix)]
    xb = [_bf(v) for v in x1]
    gate = [jax.nn.sigmoid(_dot(b_, pg_ref[...]) + pgb_ref[...]) for b_ in xb]
    pe = [_dot(_bf(p_ref[0, h, :]), pp_ref[...]) for h in halves]
    acc = [alpha * v + g_ * e_ for v, g_, e_ in zip(x1, gate, pe)]
    xb_all = jnp.concatenate(xb, axis=0)
    nk = D_FF // TF_MLP
    for kf in range(nk):
        cols = slice(kf * TF_MLP, (kf + 1) * TF_MLP)
        hk = jnp.maximum(_dot(xb_all, w1_ref[:, cols]), 0.0)
        h2 = _bf(hk * hk)
        if kf < nk - 1:
            part = _dot(h2, w2_ref[cols, :])
            acc = [a + part[h, :] for a, h in zip(acc, halves)]
        else:
            acc = [a + _dot(h2[h, :], w2_ref[cols, :]) for a, h in zip(acc, halves)]
    for h, a in zip(halves, acc):
        o_ref[h, :] = _ln(a, g2_ref[...], b2_ref[...])


def _outmlp(alpha, layer, oa, ob, oc, x, p, wa, wb, wc, g1, b1, w1, w2, pg, pgb, pp, g2, b2):
    T = x.shape[0]
    tm = min(T, TM_MLP)
    row = lambda i: (i, 0)
    fix = lambda i: (0, 0)
    once = pl.Buffered(1)
    wspec = lambda r, c: pl.BlockSpec((r, c), fix, pipeline_mode=once)
    vec = pl.BlockSpec((1, D_MODEL), fix)
    return pl.pallas_call(
        functools.partial(_outmlp_kernel, alpha),
        out_shape=jax.ShapeDtypeStruct((T, D_MODEL), jnp.float32),
        grid=(T // tm,),
        in_specs=[pl.BlockSpec((tm, ATT_WIDTH), row), pl.BlockSpec((tm, RWKV_WIDTH), row),
                  pl.BlockSpec((tm, POOL_WIDTH), row), pl.BlockSpec((tm, D_MODEL), row),
                  pl.BlockSpec((1, tm, PE_DIM), lambda i: (layer, i, 0)),
                  wspec(ATT_WIDTH, D_MODEL), wspec(RWKV_WIDTH, D_MODEL), wspec(POOL_WIDTH, D_MODEL), vec, vec,
                  wspec(D_MODEL, D_FF), wspec(D_FF, D_MODEL), wspec(D_MODEL, D_MODEL), vec,
                  wspec(PE_DIM, D_MODEL), vec, vec],
        out_specs=pl.BlockSpec((tm, D_MODEL), row),
        compiler_params=_params("parallel"),
        name="outproj_mlp",
    )(oa, ob, oc, x, p, wa, wb, wc, g1, b1, w1, w2, pg, pgb, pp, g2, b2)


def _rot_partner(w):
    half = HEAD_DIM // 2
    return jnp.concatenate([-w[..., half:], w[..., :half]], axis=-1)


def _per_head_partner(w, heads):
    r = w.reshape(w.shape[0], heads, HEAD_DIM)
    return _rot_partner(r).reshape(w.shape)


def _arrange_w_in(w):
    cq = w[:, 0:256]
    ka = w[:, 256:320]
    va = w[:, 320:384]
    ik = w[:, 384:448]
    iw = w[:, 448:456]
    rest = w[:, 456:]
    pad = jnp.zeros((w.shape[0], LANES - HEAD_DIM - IDX_HEADS), w.dtype)
    return jnp.concatenate([cq, ka, ik, _rot_partner(ka), _rot_partner(ik), va, iw, pad, rest], axis=1)


def kernel(x, p, positions, ln_emb_g, ln_emb_b, w_in, w_uq, w_uqi, attn_norm_g, rwkv_mu, rwkv_w0, rwkv_w2, rwkv_a0, rwkv_a2, rwkv_g2, rwkv_k_k, rwkv_k_a, rwkv_r_k, rwkv_lnx_g, rwkv_lnx_b, pool_w, pool_scale, w_out, ln1_g, ln1_b, mlp_w1, mlp_w2, pe_proj, pe_gate, pe_gate_b, ln2_g, ln2_b):
    B, S, _ = x.shape
    T = B * S
    depth = w_in.shape[0]
    alpha = (2 * depth) ** 0.25
    bf = lambda a: a.astype(jnp.bfloat16)
    rowv = lambda a: a.reshape(1, -1)

    cos, sin = _rope_tables(positions.astype(jnp.float32).reshape(T // LANES, LANES))
    xs = x.reshape(T, D_MODEL)
    for i in range(depth):
        win = bf(_arrange_w_in(w_in[i]))
        wq = w_uq[i] * (HEAD_DIM ** -0.5)
        wi = w_uqi[i] * (IDX_DIM ** -0.5)
        res = _inproj(xs, rowv(ln_emb_g), rowv(ln_emb_b), cos, sin, win,
                      bf(wq), bf(_per_head_partner(wq, ATT_HEADS)),
                      bf(wi), bf(_per_head_partner(wi, IDX_HEADS)), apply_ln=(i == 0))
        if i == 0:
            xs, res = res[0], res[1:]
        qt, iqt, iwt, kr, ikr, vat, zb, zc = res
        o_a = _attention(B, S, qt, iqt, iwt, kr, ikr, vat, attn_norm_g[i].reshape(-1, 1))
        o_b = _rwkv(B, S, zb, rowv(rwkv_mu[i]), rowv(rwkv_w0[i]), bf(rwkv_w2[i]), rowv(rwkv_a0[i]),
                    bf(rwkv_a2[i]), bf(rwkv_g2[i]), rowv(rwkv_k_k[i]), rowv(rwkv_k_a[i]),
                    rowv(rwkv_r_k[i]), rowv(rwkv_lnx_g[i]), rowv(rwkv_lnx_b[i]))
        w_bd = jax.scipy.linalg.block_diag(*[pool_w[i, gi] for gi in range(len(POOL_WINDOWS))])
        o_c = _pool(B, S, zc, bf(w_bd), rowv(pool_scale[i]))
        wo = bf(w_out[i])
        xs = _outmlp(alpha, i, o_a, o_b, o_c, xs, p.reshape(depth, T, PE_DIM), wo[:ATT_WIDTH],
                     wo[ATT_WIDTH:ATT_WIDTH + RWKV_WIDTH], wo[ATT_WIDTH + RWKV_WIDTH:], rowv(ln1_g[i]),
                     rowv(ln1_b[i]), bf(mlp_w1[i]), bf(mlp_w2[i]), bf(pe_gate[i]), rowv(pe_gate_b[i]),
                     bf(pe_proj[i]), rowv(ln2_g[i]), rowv(ln2_b[i]))
    return xs.reshape(B, S, D_MODEL)
```

```python
import functools

import numpy as np
import jax
import jax.numpy as jnp
from jax import lax
from jax.experimental import pallas as pl
from jax.experimental.pallas import tpu as pltpu

D_MODEL = 1024
PE_DIM = 256
HEAD_DIM = 64
ROPE_THETA = 10000.0
ATT_WIDTH = 384
ATT_HEADS = 6
Q_RANK = 256
IDX_HEADS = 8
IDX_DIM = 64
TOPK_MAX = 256
RWKV_WIDTH = 384
RWKV_HEADS = 6
DECAY_RANK = 64
ICLR_RANK = 64
GATE_RANK = 128
GN_EPS = 64e-5
POOL_WINDOWS = (2, 4, 8, 16)
POOL_WIDTH = 256
POOL_GROUP = 64
B_COLS = 3 * RWKV_WIDTH + DECAY_RANK + ICLR_RANK + GATE_RANK
D_FF = 4 * D_MODEL
LN_EPS = 1e-5
RMS_EPS = 1e-6

COL_CQ = 0
COL_KIK = 256
COL_KIK_P = 384
COL_VIW = 512
COL_B = 640
COL_C = COL_B + B_COLS
N_IN_P = COL_C + POOL_WIDTH

LANES = 128
INT_MIN = -2 ** 31
LOG2E = 1.4426950408889634
VMEM_LIMIT = 56 * 1024 * 1024

TM_PROJ = 512
TQ = 256
ATT_KEY_CHUNK = 512
CHUNK = 64
RWKV_ROWS = 4
TM_MLP = 512
TF_MLP = 1024

_NT = (((1,), (1,)), ((), ()))
_TN = (((0,), (0,)), ((), ()))


def _params(*sem):
    return pltpu.CompilerParams(dimension_semantics=sem, vmem_limit_bytes=VMEM_LIMIT)


def _ln(x, g, b):
    mu = jnp.mean(x, axis=-1, keepdims=True)
    xc = x - mu
    var = jnp.mean(xc * xc, axis=-1, keepdims=True)
    return xc * lax.rsqrt(var + LN_EPS) * g + b


def _bf(x):
    return x.astype(jnp.bfloat16)


def _dot(a, b):
    return jnp.dot(a, b, preferred_element_type=jnp.float32)


def _dot_nt(a, b):
    return lax.dot_general(a, b, _NT, preferred_element_type=jnp.float32)


def _dot_tn(a, b):
    return lax.dot_general(a, b, _TN, preferred_element_type=jnp.float32)


def _rope_kernel(pos_ref, inv_ref, cos_ref, sin_ref):
    for r in range(pos_ref.shape[0]):
        col = jnp.broadcast_to(pos_ref[r:r + 1, :], (LANES, LANES)).T
        ang = col * inv_ref[...]
        cos_ref[r * LANES:(r + 1) * LANES, :] = jnp.cos(ang)
        sin_ref[r * LANES:(r + 1) * LANES, :] = jnp.sin(ang)


def _rope_tables(pos_f32):
    T = pos_f32.shape[0] * LANES
    rows = min(pos_f32.shape[0], 16)
    inv = ROPE_THETA ** (-np.arange(0, HEAD_DIM, 2, dtype=np.float32) / HEAD_DIM)
    inv128 = jnp.asarray(np.tile(inv, LANES // (HEAD_DIM // 2))[None, :], jnp.float32)
    return pl.pallas_call(
        _rope_kernel,
        out_shape=(jax.ShapeDtypeStruct((T, LANES), jnp.float32),) * 2,
        grid=(pos_f32.shape[0] // rows,),
        in_specs=[pl.BlockSpec((rows, LANES), lambda i: (i, 0)),
                  pl.BlockSpec((1, LANES), lambda i: (0, 0))],
        out_specs=(pl.BlockSpec((rows * LANES, LANES), lambda i: (i, 0)),) * 2,
        compiler_params=_params("parallel"),
        name="rope_tables",
    )(pos_f32, inv128)


def _inproj_kernel(apply_ln, x_ref, g_ref, b_ref, cos_ref, sin_ref, win_ref, wq_ref, wqp_ref,
                   wi_ref, wip_ref, *out_refs):
    if apply_ln:
        xn_ref, qt_ref, iqt_ref, iwt_ref, kr_ref, ikr_ref, vat_ref, zb_ref, zc_ref = out_refs
        x = _ln(x_ref[...], g_ref[...], b_ref[...])
        xn_ref[...] = x
    else:
        qt_ref, iqt_ref, iwt_ref, kr_ref, ikr_ref, vat_ref, zb_ref, zc_ref = out_refs
        x = x_ref[...]
    z = _dot(_bf(x), win_ref[...])
    cos = cos_ref[...]
    sin = sin_ref[...]
    cq = _bf(z[:, COL_CQ:COL_CQ + Q_RANK])
    cos3 = jnp.concatenate([cos] * 3, axis=1)
    sin3 = jnp.concatenate([sin] * 3, axis=1)
    qt_ref[...] = _bf(((_dot(cq, wq_ref[...]) * cos3 + _dot(cq, wqp_ref[...]) * sin3) * LOG2E).T)
    cos4 = jnp.concatenate([cos] * 4, axis=1)
    sin4 = jnp.concatenate([sin] * 4, axis=1)
    iqt_ref[...] = _bf((_dot(cq, wi_ref[...]) * cos4 + _dot(cq, wip_ref[...]) * sin4).T)
    kik = z[:, COL_KIK:COL_KIK + LANES] * cos + z[:, COL_KIK_P:COL_KIK_P + LANES] * sin
    kr_ref[...] = _bf(kik[:, :HEAD_DIM])
    ikr_ref[...] = _bf(kik[:, HEAD_DIM:])
    viw_t = z[:, COL_VIW:COL_VIW + LANES].T
    vat_ref[...] = _bf(viw_t[:HEAD_DIM])
    iwt_ref[...] = viw_t[HEAD_DIM:HEAD_DIM + IDX_HEADS] * (IDX_HEADS ** -0.5)
    zb_ref[...] = z[:, COL_B:COL_B + B_COLS]
    zc_ref[...] = z[:, COL_C:COL_C + POOL_WIDTH]


def _inproj(x, g, b, cos, sin, win, wq, wqp, wi, wip, apply_ln):
    T = x.shape[0]
    tm = min(T, TM_PROJ)
    row = lambda i: (i, 0)
    fix = lambda i: (0, 0)
    bf16, f32 = jnp.bfloat16, jnp.float32
    col = lambda i: (0, i)
    tok = lambda n, dt: (jax.ShapeDtypeStruct((T, n), dt), pl.BlockSpec((tm, n), row))
    feat = lambda n, dt: (jax.ShapeDtypeStruct((n, T), dt), pl.BlockSpec((n, tm), col))
    outs = [feat(ATT_WIDTH, bf16), feat(IDX_HEADS * IDX_DIM, bf16), feat(IDX_HEADS, f32), tok(HEAD_DIM, bf16),
            tok(HEAD_DIM, bf16), feat(HEAD_DIM, bf16), tok(B_COLS, f32), tok(POOL_WIDTH, f32)]
    shapes = [o[0] for o in outs]
    specs = [o[1] for o in outs]
    if apply_ln:
        shapes = [jax.ShapeDtypeStruct((T, D_MODEL), f32)] + shapes
        specs = [pl.BlockSpec((tm, D_MODEL), row)] + specs
    return pl.pallas_call(
        functools.partial(_inproj_kernel, apply_ln),
        out_shape=tuple(shapes),
        grid=(T // tm,),
        in_specs=[pl.BlockSpec((tm, D_MODEL), row),
                  pl.BlockSpec((1, D_MODEL), fix), pl.BlockSpec((1, D_MODEL), fix),
                  pl.BlockSpec((tm, LANES), row), pl.BlockSpec((tm, LANES), row),
                  pl.BlockSpec((D_MODEL, N_IN_P), fix),
                  pl.BlockSpec((Q_RANK, ATT_WIDTH), fix), pl.BlockSpec((Q_RANK, ATT_WIDTH), fix),
                  pl.BlockSpec((Q_RANK, IDX_HEADS * IDX_DIM), fix),
                  pl.BlockSpec((Q_RANK, IDX_HEADS * IDX_DIM), fix)],
        out_specs=tuple(specs),
        compiler_params=_params("parallel"),
        name="inproj",
    )(x, g, b, cos, sin, win, wq, wqp, wi, wip)


def _attn_block(kend, search, topk, qt_ref, iqt_ref, iwt_ref, kr_ref, ikr_ref, vat_ref, g_ref, o_ref,
                key_ref, bias_ref, ot_ref, hi_ref, lo_ref):
    tq = qt_ref.shape[1]
    q_idx = pl.program_id(1) * tq + lax.broadcasted_iota(jnp.int32, (1, tq), 1)
    k_idx = lax.broadcasted_iota(jnp.int32, (kend, 1), 0)
    causal = k_idx <= q_idx

    if not search:
        bias_ref[:kend, :] = jnp.where(causal, 0.0, -jnp.inf)
    else:
        bias_ref[:kend, :] = jnp.zeros((kend, tq), jnp.float32)

        def idx_pair(j, carry):
            ik = ikr_ref[0:kend, :]
            acc = bias_ref[:kend, :]
            heads = [2 * j, 2 * j + 1]
            lgs = [_dot(ik, iqt_ref[pl.ds(pl.multiple_of(h * IDX_DIM, IDX_DIM), IDX_DIM), :]) for h in heads]
            for h, lg in zip(heads, lgs):
                acc = acc + jnp.maximum(lg, 0.0) * iwt_ref[pl.ds(h, 1), :]
            bias_ref[:kend, :] = acc
            return carry

        lax.fori_loop(0, IDX_HEADS // 2, idx_pair, 0)
        score = bias_ref[:kend, :] + 0.0
        bits = pltpu.bitcast(score, jnp.int32)
        key = bits ^ ((bits >> 31) & 0x7FFFFFFF)
        key_ref[:kend, :] = jnp.where(causal, key, INT_MIN)

        i16 = jnp.int16
        pack = 16
        one_b = jnp.ones((pack, tq), jnp.bfloat16)
        zero_b = jnp.zeros((pack, tq), jnp.bfloat16)

        def count16(ref, cand):
            c16 = jnp.broadcast_to(cand, (pack, tq)).astype(i16)
            accs = [zero_b] * 4
            for n, i in enumerate(range(0, kend, pack)):
                accs[n % 4] = accs[n % 4] + jnp.where(ref[i:i + pack, :] >= c16, one_b, zero_b)
            acc = (accs[0] + accs[1]) + (accs[2] + accs[3])
            return jnp.sum(acc.astype(jnp.float32), axis=0, keepdims=True)

        def kth_largest16(ref, k):
            t = jnp.where(count16(ref, jnp.zeros((1, tq), jnp.int32)) >= k, 0, -32768).astype(jnp.int32)

            def bit_step(i, t):
                cand = t | (jnp.int32(1) << (14 - i))
                return jnp.where(count16(ref, cand) >= k, cand, t)

            return lax.fori_loop(0, 15, bit_step, t)

        keyv = key_ref[:kend, :]
        hi_ref[:kend, :] = (keyv >> 16).astype(i16)
        tau_hi = kth_largest16(hi_ref, jnp.full((1, tq), float(topk), jnp.float32))
        above = jnp.where(tau_hi < 32767, count16(hi_ref, jnp.minimum(tau_hi + 1, 32767)), 0.0)
        lo = ((keyv & 0xFFFF) - 32768).astype(i16)
        same_hi = hi_ref[:kend, :] == jnp.broadcast_to(tau_hi, (kend, tq)).astype(i16)
        lo_ref[:kend, :] = jnp.where(same_hi, lo, -32768)
        tau_lo = kth_largest16(lo_ref, topk - above)
        tau = (tau_hi << 16) | (tau_lo + 32768)
        tau = jnp.maximum(tau, INT_MIN + 1)
        ge = key_ref[:kend, :] >= tau
        n_ge = jnp.sum(jnp.where(ge, 1.0, 0.0), axis=0, keepdims=True)
        bias_ref[:kend, :] = jnp.where(ge, 0.0, -jnp.inf)

        @pl.when(jnp.max(n_ge) > topk)
        def _():
            need = topk - jnp.sum(jnp.where(key_ref[:kend, :] > tau, 1.0, 0.0), axis=0, keepdims=True)
            ri = lax.broadcasted_iota(jnp.int32, (LANES, LANES), 0)
            ci = lax.broadcasted_iota(jnp.int32, (LANES, LANES), 1)
            lower = _bf(jnp.where(ri >= ci, 1.0, 0.0))
            before = jnp.zeros((1, tq), jnp.float32)
            for c in range(0, kend, LANES):
                keyc = key_ref[c:c + LANES, :]
                eq = keyc == tau
                rank = _dot(lower, _bf(jnp.where(eq, 1.0, 0.0))) + before
                keep = (keyc > tau) | (eq & (rank <= need))
                bias_ref[c:c + LANES, :] = jnp.where(keep, 0.0, -jnp.inf)
                before = rank[LANES - 1:LANES, :]

    kc = min(kend, ATT_KEY_CHUNK)
    chunks = range(0, kend, kc)

    def att_pair(j, carry):
        heads = [2 * j, 2 * j + 1]
        qs = [qt_ref[pl.ds(pl.multiple_of(h * HEAD_DIM, HEAD_DIM), HEAD_DIM), :] for h in heads]
        s_h = [[_dot(kr_ref[c:c + kc, :], q_) + bias_ref[c:c + kc, :] for c in chunks] for q_ in qs]
        for h, s_c in zip(heads, s_h):
            m = jnp.max(functools.reduce(jnp.maximum, s_c), axis=0, keepdims=True)
            p_c = [jnp.exp2(s - m) for s in s_c]
            l = jnp.sum(sum(p_c), axis=0, keepdims=True)
            acc = sum(_dot(vat_ref[:, c:c + kc], _bf(p)) for p, c in zip(p_c, chunks))
            ot_ref[pl.ds(pl.multiple_of(h * HEAD_DIM, HEAD_DIM), HEAD_DIM), :] = acc / l
        return carry

    lax.fori_loop(0, ATT_HEADS // 2, att_pair, 0)
    ot = ot_ref[...]
    ot = ot * lax.rsqrt(jnp.mean(ot * ot, axis=0, keepdims=True) + RMS_EPS) * g_ref[...]
    o_ref[...] = _bf(ot.T)


def _key_extents(S, tq, topk):
    out = []
    for v in range(S // tq):
        need = (v + 1) * tq
        if need <= topk:
            out.append((need, False))
        else:
            out.append((min(S, -(-need // (2 * tq)) * 2 * tq), True))
    return out


def _attn_kernel(S, *refs):
    tq = refs[0].shape[1]
    topk = min(TOPK_MAX, S // 4)
    qi = pl.program_id(1)
    extents = _key_extents(S, tq, topk)
    for ext in sorted(set(extents)):
        blocks = [v for v, e in enumerate(extents) if e == ext]
        pl.when((qi >= blocks[0]) & (qi <= blocks[-1]))(functools.partial(_attn_block, *ext, topk, *refs))


def _attention(B, S, qt, iqt, iwt, kr, ikr, vat, g):
    tq = min(S, TQ)
    nq = S // tq
    qcol = lambda b, i: (0, b * nq + i)
    seq = lambda b, i: (b, 0)
    return pl.pallas_call(
        functools.partial(_attn_kernel, S),
        out_shape=jax.ShapeDtypeStruct((B * S, ATT_WIDTH), jnp.bfloat16),
        grid=(B, nq),
        in_specs=[pl.BlockSpec((ATT_WIDTH, tq), qcol),
                  pl.BlockSpec((IDX_HEADS * IDX_DIM, tq), qcol),
                  pl.BlockSpec((IDX_HEADS, tq), qcol),
                  pl.BlockSpec((S, HEAD_DIM), seq), pl.BlockSpec((S, HEAD_DIM), seq),
                  pl.BlockSpec((HEAD_DIM, S), lambda b, i: (0, b)),
                  pl.BlockSpec((ATT_WIDTH, 1), lambda b, i: (0, 0))],
        out_specs=pl.BlockSpec((tq, ATT_WIDTH), lambda b, i: (b * nq + i, 0)),
        scratch_shapes=[pltpu.VMEM((S, tq), jnp.int32), pltpu.VMEM((S, tq), jnp.float32),
                        pltpu.VMEM((ATT_WIDTH, tq), jnp.float32),
                        pltpu.VMEM((S, tq), jnp.int16), pltpu.VMEM((S, tq), jnp.int16)],
        compiler_params=_params("parallel", "parallel"),
        name="dsa_attention",
    )(qt, iqt, iwt, kr, ikr, vat, g)


def _split3(x):
    hi = _bf(x)
    r1 = x - hi.astype(jnp.float32)
    mid = _bf(r1)
    lo = _bf(r1 - mid.astype(jnp.float32))
    return hi, mid, lo


def _rwkv_kernel(zb_ref, mu_ref, w0_ref, w2_ref, a0_ref, a2_ref, g2_ref, kk_ref, ka_ref, rk_ref,
                 lg_ref, lb_ref, o_ref, prev_ref, state_ref):
    G, C, _ = zb_ref.shape
    W = RWKV_WIDTH
    R = G * C
    H = RWKV_HEADS

    @pl.when(pl.program_id(1) == 0)
    def _():
        prev_ref[...] = jnp.zeros_like(prev_ref)
        state_ref[...] = jnp.zeros_like(state_ref)

    x = zb_ref[...].reshape(R, B_COLS)
    rowi = lax.broadcasted_iota(jnp.int32, (R, 1), 0)
    shifted = pltpu.roll(x, 1, 0)
    for gi in range(G):
        shifted = jnp.where(rowi == gi * C, prev_ref[8 * gi:8 * gi + 1, :], shifted)
        prev_ref[8 * gi:8 * gi + 1, :] = x[gi * C + C - 1:gi * C + C, :]
    xm = x + (shifted - x) * mu_ref[...]
    r = xm[:, 0:W]
    k = xm[:, W:2 * W]
    v = xm[:, 2 * W:3 * W]
    wl = xm[:, 3 * W:3 * W + DECAY_RANK]
    al = xm[:, 3 * W + DECAY_RANK:3 * W + DECAY_RANK + ICLR_RANK]
    gl = xm[:, 3 * W + DECAY_RANK + ICLR_RANK:]

    wpre = w0_ref[...] + _dot(_bf(jnp.tanh(wl)), w2_ref[...])
    nw = -wpre
    w = -(jnp.maximum(nw, 0.0) + jnp.log(1.0 + jnp.exp(-jnp.abs(nw)))) - 0.5
    logd = -jnp.exp(w)
    a = jax.nn.sigmoid(a0_ref[...] + _dot(_bf(al), a2_ref[...]))
    g = _dot(_bf(jax.nn.sigmoid(gl)), g2_ref[...])

    rr = lax.broadcasted_iota(jnp.int32, (R, R), 0)
    cc = lax.broadcasted_iota(jnp.int32, (R, R), 1)
    tri = _bf(jnp.where((rr >= cc) & (rr // C == cc // C), 1.0, 0.0))
    hi, mid, lo = _split3(logd)
    cw = _dot(tri, hi) + _dot(tri, mid) + _dot(tri, lo)
    e_in = jnp.exp(cw)
    e_ex = jnp.exp(cw - logd)
    e_inv = jnp.exp(-cw)

    kk = k * kk_ref[...]
    k2 = k * (1.0 + (a - 1.0) * ka_ref[...])
    rk2 = r * k2 * rk_ref[...]
    ri = lax.broadcasted_iota(jnp.int32, (C, C), 0)
    ci = lax.broadcasted_iota(jnp.int32, (C, C), 1)
    strict = ri > ci
    ri2 = lax.broadcasted_iota(jnp.int32, (C, 2 * C), 0)
    ci2 = lax.broadcasted_iota(jnp.int32, (C, 2 * C), 1)
    incl2 = ri2 >= jnp.where(ci2 >= C, ci2 - C, ci2)
    eye = jnp.where(ri == ci, 1.0, 0.0)

    units = [(gi, h) for gi in range(G) for h in range(H)]

    def cut(arr, u):
        gi, h = u
        return arr[gi * C:(gi + 1) * C, h * HEAD_DIM:(h + 1) * HEAD_DIM]

    def unit_norm(u):
        kkh = cut(kk, u)
        return kkh / jnp.maximum(jnp.sqrt(jnp.sum(kkh * kkh, axis=-1, keepdims=True)), 1e-12)

    kkn = [unit_norm(u) for u in units]
    v_u = [cut(v, u) for u in units]
    at = [_bf(-kn * cut(e_ex, u)) for kn, u in zip(kkn, units)]
    rt = [_bf(cut(r, u) * cut(e_in, u)) for u in units]
    bt = [kn * cut(a, u) * cut(e_inv, u) for kn, u in zip(kkn, units)]
    kt = [cut(k2, u) * cut(e_inv, u) for u in units]
    bk16 = [_bf(jnp.concatenate([b_, k_], axis=0)) for b_, k_ in zip(bt, kt)]
    v16 = [_bf(v_) for v_ in v_u]
    g_a = [_dot_nt(a_, m_) for a_, m_ in zip(at, bk16)]
    l_ab = [jnp.where(strict, g_[:, :C], 0.0) for g_ in g_a]
    a_ak = [_bf(jnp.where(strict, g_[:, C:], 0.0)) for g_ in g_a]
    tinv = [eye + l_ for l_ in l_ab]
    pw = l_ab
    n = 2
    while n < C:
        pw = [_dot(_bf(p_), _bf(p_)) for p_ in pw]
        tinv = [t_ + _dot(_bf(t_), _bf(p_)) for t_, p_ in zip(tinv, pw)]
        n *= 2
    s0 = [state_ref[i] for i in range(len(units))]
    s016 = [_bf(s_) for s_ in s0]
    rhs = [_dot_nt(a_, s_) + _dot(m_, v_) for a_, s_, m_, v_ in zip(at, s016, a_ak, v16)]
    uu = [_dot(_bf(t_), _bf(x_)) for t_, x_ in zip(tinv, rhs)]
    uv16 = [_bf(jnp.concatenate([u_, v_], axis=0)) for u_, v_ in zip(uu, v_u)]
    g_r = [_bf(jnp.where(incl2, _dot_nt(r_, m_), 0.0)) for r_, m_ in zip(rt, bk16)]
    y = [_dot(m_, x_) + _dot_nt(r_, s_) for m_, x_, r_, s_ in zip(g_r, uv16, rt, s016)]
    for i, u in enumerate(units):
        state_ref[i] = (s0[i] + _dot_tn(uv16[i], bk16[i])) * cut(e_in, u)[C - 1:C, :]
    for gi in range(G):
        outs = []
        for h in range(H):
            i = gi * H + h
            u = units[i]
            sl = slice(h * HEAD_DIM, (h + 1) * HEAD_DIM)
            ym = jnp.mean(y[i], axis=-1, keepdims=True)
            yc = y[i] - ym
            yv = jnp.mean(yc * yc, axis=-1, keepdims=True)
            yn = yc * lax.rsqrt(yv + GN_EPS) * lg_ref[:, sl] + lb_ref[:, sl]
            bonus = jnp.sum(cut(rk2, u), axis=-1, keepdims=True) * v_u[i]
            outs.append((yn + bonus) * cut(g, u))
        o_ref[gi] = _bf(jnp.concatenate(outs, axis=1))


def _rwkv(B, S, zb, mu, w0, w2, a0, a2, g2, k_k, k_a, r_k, lnx_g, lnx_b):
    C = min(S, CHUNK)
    G = min(B, RWKV_ROWS)
    blk = lambda b, c: (b, c, 0)
    fix = lambda b, c: (0, 0)
    vec = lambda n: pl.BlockSpec((1, n), fix)
    return pl.pallas_call(
        _rwkv_kernel,
        out_shape=jax.ShapeDtypeStruct((B, S, RWKV_WIDTH), jnp.bfloat16),
        grid=(B // G, S // C),
        in_specs=[pl.BlockSpec((G, C, B_COLS), blk), vec(B_COLS), vec(RWKV_WIDTH),
                  pl.BlockSpec((DECAY_RANK, RWKV_WIDTH), fix), vec(RWKV_WIDTH),
                  pl.BlockSpec((ICLR_RANK, RWKV_WIDTH), fix),
                  pl.BlockSpec((GATE_RANK, RWKV_WIDTH), fix),
                  vec(RWKV_WIDTH), vec(RWKV_WIDTH), vec(RWKV_WIDTH), vec(RWKV_WIDTH),
                  vec(RWKV_WIDTH)],
        out_specs=pl.BlockSpec((G, C, RWKV_WIDTH), blk),
        scratch_shapes=[pltpu.VMEM((8 * G, B_COLS), jnp.float32),
                        pltpu.VMEM((G * RWKV_HEADS, HEAD_DIM, HEAD_DIM), jnp.float32)],
        compiler_params=_params("parallel", "arbitrary"),
        name="rwkv7",
    )(zb.reshape(B, S, B_COLS), mu, w0, w2, a0, a2, g2, k_k, k_a, r_k, lnx_g, lnx_b
      ).reshape(B * S, RWKV_WIDTH)


def _pool_kernel(zc_ref, w_ref, sc_ref, o_ref):
    S = zc_ref.shape[0]
    x = zc_ref[...]
    row = lax.broadcasted_iota(jnp.int32, (S, 1), 0)
    lane_grp = lax.broadcasted_iota(jnp.int32, (1, POOL_WIDTH), 1) // POOL_GROUP

    def lag(y, n):
        return jnp.where(row >= n, pltpu.roll(y, n, 0), 0.0)

    w2 = x + lag(x, 1)
    w4 = w2 + lag(w2, 2)
    w8 = w4 + lag(w4, 4)
    w16 = w8 + lag(w8, 8)
    wsum = jnp.where(lane_grp == 0, w2, jnp.where(lane_grp == 1, w4, jnp.where(lane_grp == 2, w8, w16)))
    win = jnp.where(lane_grp == 0, 2, jnp.where(lane_grp == 1, 4, jnp.where(lane_grp == 2, 8, 16)))
    count = jnp.minimum(row + 1, win).astype(jnp.float32)
    pooled = wsum / count - x
    o_ref[...] = _bf(_dot(_bf(pooled), w_ref[...]) * sc_ref[...])


def _pool(B, S, zc, w_bd, scale):
    return pl.pallas_call(
        _pool_kernel,
        out_shape=jax.ShapeDtypeStruct((B * S, POOL_WIDTH), jnp.bfloat16),
        grid=(B,),
        in_specs=[pl.BlockSpec((S, POOL_WIDTH), lambda b: (b, 0)),
                  pl.BlockSpec((POOL_WIDTH, POOL_WIDTH), lambda b: (0, 0)),
                  pl.BlockSpec((1, POOL_WIDTH), lambda b: (0, 0))],
        out_specs=pl.BlockSpec((S, POOL_WIDTH), lambda b: (b, 0)),
        compiler_params=_params("parallel"),
        name="pool",
    )(zc, w_bd, scale)


def _outmlp_kernel(alpha, oa_ref, ob_ref, oc_ref, x_ref, p_ref, wa_ref, wb_ref, wc_ref, g1_ref, b1_ref,
                   w1_ref, w2_ref, pg_ref, pgb_ref, pp_ref, g2_ref, b2_ref, o_ref):
    tm = x_ref.shape[0]
    halves = [slice(0, tm // 2), slice(tm // 2, tm)]
    mix = [_dot(oa_ref[h, :], wa_ref[...]) + _dot(ob_ref[h, :], wb_ref[...]) + _dot(oc_ref[h, :], wc_ref[...])
           for h in halves]
    x1 = [_ln(alpha * x_ref[h, :] + m, g1_ref[...], b1_ref[...]) for h, m in zip(halves, mix)]
    xb = [_bf(v) for v in x1]
    gate = [jax.nn.sigmoid(_dot(b_, pg_ref[...]) + pgb_ref[...]) for b_ in xb]
    pe = [_dot(_bf(p_ref[0, h, :]), pp_ref[...]) for h in halves]
    acc = [alpha * v + g_ * e_ for v, g_, e_ in zip(x1, gate, pe)]
    xb_all = jnp.concatenate(xb, axis=0)
    nk = D_FF // TF_MLP
    for kf in range(nk):
        cols = slice(kf * TF_MLP, (kf + 1) * TF_MLP)
        hk = jnp.maximum(_dot(xb_all, w1_ref[:, cols]), 0.0)
        h2 = _bf(hk * hk)
        if kf < nk - 1:
            part = _dot(h2, w2_ref[cols, :])
            acc = [a + part[h, :] for a, h in zip(acc, halves)]
        else:
            acc = [a + _dot(h2[h, :], w2_ref[cols, :]) for a, h in zip(acc, halves)]
    for h, a in zip(halves, acc):
        o_ref[h, :] = _ln(a, g2_ref[...], b2_ref[...])


def _outmlp(alpha, layer, oa, ob, oc, x, p, wa, wb, wc, g1, b1, w1, w2, pg, pgb, pp, g2, b2):
    T = x.shape[0]
    tm = min(T, TM_MLP)
    row = lambda i: (i, 0)
    fix = lambda i: (0, 0)
    once = pl.Buffered(1)
    wspec = lambda r, c: pl.BlockSpec((r, c), fix, pipeline_mode=once)
    vec = pl.BlockSpec((1, D_MODEL), fix)
    return pl.pallas_call(
        functools.partial(_outmlp_kernel, alpha),
        out_shape=jax.ShapeDtypeStruct((T, D_MODEL), jnp.float32),
        grid=(T // tm,),
        in_specs=[pl.BlockSpec((tm, ATT_WIDTH), row), pl.BlockSpec((tm, RWKV_WIDTH), row),
                  pl.BlockSpec((tm, POOL_WIDTH), row), pl.BlockSpec((tm, D_MODEL), row),
                  pl.BlockSpec((1, tm, PE_DIM), lambda i: (layer, i, 0)),
                  wspec(ATT_WIDTH, D_MODEL), wspec(RWKV_WIDTH, D_MODEL), wspec(POOL_WIDTH, D_MODEL), vec, vec,
                  wspec(D_MODEL, D_FF), wspec(D_FF, D_MODEL), wspec(D_MODEL, D_MODEL), vec,
                  wspec(PE_DIM, D_MODEL), vec, vec],
        out_specs=pl.BlockSpec((tm, D_MODEL), row),
        compiler_params=_params("parallel"),
        name="outproj_mlp",
    )(oa, ob, oc, x, p, wa, wb, wc, g1, b1, w1, w2, pg, pgb, pp, g2, b2)


def _rot_partner(w):
    half = HEAD_DIM // 2
    return jnp.concatenate([-w[..., half:], w[..., :half]], axis=-1)


def _per_head_partner(w, heads):
    r = w.reshape(w.shape[0], heads, HEAD_DIM)
    return _rot_partner(r).reshape(w.shape)


def _arrange_w_in(w):
    cq = w[:, 0:256]
    ka = w[:, 256:320]
    va = w[:, 320:384]
    ik = w[:, 384:448]
    iw = w[:, 448:456]
    rest = w[:, 456:]
    pad = jnp.zeros((w.shape[0], LANES - HEAD_DIM - IDX_HEADS), w.dtype)
    return jnp.concatenate([cq, ka, ik, _rot_partner(ka), _rot_partner(ik), va, iw, pad, rest], axis=1)


def kernel(x, p, positions, ln_emb_g, ln_emb_b, w_in, w_uq, w_uqi, attn_norm_g, rwkv_mu, rwkv_w0, rwkv_w2, rwkv_a0, rwkv_a2, rwkv_g2, rwkv_k_k, rwkv_k_a, rwkv_r_k, rwkv_lnx_g, rwkv_lnx_b, pool_w, pool_scale, w_out, ln1_g, ln1_b, mlp_w1, mlp_w2, pe_proj, pe_gate, pe_gate_b, ln2_g, ln2_b):
    B, S, _ = x.shape
    T = B * S
    depth = w_in.shape[0]
    alpha = (2 * depth) ** 0.25
    bf = lambda a: a.astype(jnp.bfloat16)
    rowv = lambda a: a.reshape(1, -1)

    cos, sin = _rope_tables(positions.astype(jnp.float32).reshape(T // LANES, LANES))
    xs = x.reshape(T, D_MODEL)
    for i in range(depth):
        win = bf(_arrange_w_in(w_in[i]))
        wq = w_uq[i] * (HEAD_DIM ** -0.5)
        wi = w_uqi[i] * (IDX_DIM ** -0.5)
        res = _inproj(xs, rowv(ln_emb_g), rowv(ln_emb_b), cos, sin, win,
                      bf(wq), bf(_per_head_partner(wq, ATT_HEADS)),
                      bf(wi), bf(_per_head_partner(wi, IDX_HEADS)), apply_ln=(i == 0))
        if i == 0:
            xs, res = res[0], res[1:]
        qt, iqt, iwt, kr, ikr, vat, zb, zc = res
        o_a = _attention(B, S, qt, iqt, iwt, kr, ikr, vat, attn_norm_g[i].reshape(-1, 1))
        o_b = _rwkv(B, S, zb, rowv(rwkv_mu[i]), rowv(rwkv_w0[i]), bf(rwkv_w2[i]), rowv(rwkv_a0[i]),
                    bf(rwkv_a2[i]), bf(rwkv_g2[i]), rowv(rwkv_k_k[i]), rowv(rwkv_k_a[i]),
                    rowv(rwkv_r_k[i]), rowv(rwkv_lnx_g[i]), rowv(rwkv_lnx_b[i]))
        w_bd = jax.scipy.linalg.block_diag(*[pool_w[i, gi] for gi in range(len(POOL_WINDOWS))])
        o_c = _pool(B, S, zc, bf(w_bd), rowv(pool_scale[i]))
        wo = bf(w_out[i])
        xs = _outmlp(alpha, i, o_a, o_b, o_c, xs, p.reshape(depth, T, PE_DIM), wo[:ATT_WIDTH],
                     wo[ATT_WIDTH:ATT_WIDTH + RWKV_WIDTH], wo[ATT_WIDTH + RWKV_WIDTH:], rowv(ln1_g[i]),
                     rowv(ln1_b[i]), bf(mlp_w1[i]), bf(mlp_w2[i]), bf(pe_gate[i]), rowv(pe_gate_b[i]),
                     bf(pe_proj[i]), rowv(ln2_g[i]), rowv(ln2_b[i]))
    return xs.reshape(B, S, D_MODEL)
```

```python
import functools

import numpy as np
import jax
import jax.numpy as jnp
from jax import lax
from jax.experimental import pallas as pl
from jax.experimental.pallas import tpu as pltpu

D_MODEL = 1024
PE_DIM = 256
HEAD_DIM = 64
ROPE_THETA = 10000.0
ATT_WIDTH = 384
ATT_HEADS = 6
Q_RANK = 256
IDX_HEADS = 8
IDX_DIM = 64
TOPK_MAX = 256
RWKV_WIDTH = 384
RWKV_HEADS = 6
DECAY_RANK = 64
ICLR_RANK = 64
GATE_RANK = 128
GN_EPS = 64e-5
POOL_WINDOWS = (2, 4, 8, 16)
POOL_WIDTH = 256
POOL_GROUP = 64
B_COLS = 3 * RWKV_WIDTH + DECAY_RANK + ICLR_RANK + GATE_RANK
D_FF = 4 * D_MODEL
LN_EPS = 1e-5
RMS_EPS = 1e-6

COL_CQ = 0
COL_KIK = 256
COL_KIK_P = 384
COL_VIW = 512
COL_B = 640
COL_C = COL_B + B_COLS
N_IN_P = COL_C + POOL_WIDTH

LANES = 128
INT_MIN = -2 ** 31
LOG2E = 1.4426950408889634
VMEM_LIMIT = 56 * 1024 * 1024

TM_PROJ = 512
TQ = 256
ATT_KEY_CHUNK = 512
ATT_GROUP = 3
CHUNK = 64
RWKV_ROWS = 4
TM_MLP = 512
TF_MLP = 1024

_NT = (((1,), (1,)), ((), ()))
_TN = (((0,), (0,)), ((), ()))


def _params(*sem):
    return pltpu.CompilerParams(dimension_semantics=sem, vmem_limit_bytes=VMEM_LIMIT)


def _ln(x, g, b):
    mu = jnp.mean(x, axis=-1, keepdims=True)
    xc = x - mu
    var = jnp.mean(xc * xc, axis=-1, keepdims=True)
    return xc * lax.rsqrt(var + LN_EPS) * g + b


def _bf(x):
    return x.astype(jnp.bfloat16)


def _dot(a, b):
    return jnp.dot(a, b, preferred_element_type=jnp.float32)


def _dot_nt(a, b):
    return lax.dot_general(a, b, _NT, preferred_element_type=jnp.float32)


def _dot_tn(a, b):
    return lax.dot_general(a, b, _TN, preferred_element_type=jnp.float32)


def _rope_kernel(pos_ref, inv_ref, cos_ref, sin_ref):
    for r in range(pos_ref.shape[0]):
        col = jnp.broadcast_to(pos_ref[r:r + 1, :], (LANES, LANES)).T
        ang = col * inv_ref[...]
        cos_ref[r * LANES:(r + 1) * LANES, :] = jnp.cos(ang)
        sin_ref[r * LANES:(r + 1) * LANES, :] = jnp.sin(ang)


def _rope_tables(pos_f32):
    T = pos_f32.shape[0] * LANES
    rows = min(pos_f32.shape[0], 16)
    inv = ROPE_THETA ** (-np.arange(0, HEAD_DIM, 2, dtype=np.float32) / HEAD_DIM)
    inv128 = jnp.asarray(np.tile(inv, LANES // (HEAD_DIM // 2))[None, :], jnp.float32)
    return pl.pallas_call(
        _rope_kernel,
        out_shape=(jax.ShapeDtypeStruct((T, LANES), jnp.float32),) * 2,
        grid=(pos_f32.shape[0] // rows,),
        in_specs=[pl.BlockSpec((rows, LANES), lambda i: (i, 0)),
                  pl.BlockSpec((1, LANES), lambda i: (0, 0))],
        out_specs=(pl.BlockSpec((rows * LANES, LANES), lambda i: (i, 0)),) * 2,
        compiler_params=_params("parallel"),
        name="rope_tables",
    )(pos_f32, inv128)


def _inproj_kernel(apply_ln, x_ref, g_ref, b_ref, cos_ref, sin_ref, win_ref, wq_ref, wqp_ref,
                   wi_ref, wip_ref, *out_refs):
    if apply_ln:
        xn_ref, qt_ref, iqt_ref, iwt_ref, kr_ref, ikr_ref, vat_ref, zb_ref, zc_ref = out_refs
        x = _ln(x_ref[...], g_ref[...], b_ref[...])
        xn_ref[...] = x
    else:
        qt_ref, iqt_ref, iwt_ref, kr_ref, ikr_ref, vat_ref, zb_ref, zc_ref = out_refs
        x = x_ref[...]
    z = _dot(_bf(x), win_ref[...])
    cos = cos_ref[...]
    sin = sin_ref[...]
    cq = _bf(z[:, COL_CQ:COL_CQ + Q_RANK])
    cos3 = jnp.concatenate([cos] * 3, axis=1)
    sin3 = jnp.concatenate([sin] * 3, axis=1)
    qt_ref[...] = _bf(((_dot(cq, wq_ref[...]) * cos3 + _dot(cq, wqp_ref[...]) * sin3) * LOG2E).T)
    cos4 = jnp.concatenate([cos] * 4, axis=1)
    sin4 = jnp.concatenate([sin] * 4, axis=1)
    iqt_ref[...] = _bf((_dot(cq, wi_ref[...]) * cos4 + _dot(cq, wip_ref[...]) * sin4).T)
    kik = z[:, COL_KIK:COL_KIK + LANES] * cos + z[:, COL_KIK_P:COL_KIK_P + LANES] * sin
    kr_ref[...] = _bf(kik[:, :HEAD_DIM])
    ikr_ref[...] = _bf(kik[:, HEAD_DIM:])
    viw_t = z[:, COL_VIW:COL_VIW + LANES].T
    vat_ref[...] = _bf(viw_t[:HEAD_DIM])
    iwt_ref[...] = viw_t[HEAD_DIM:HEAD_DIM + IDX_HEADS] * (IDX_HEADS ** -0.5)
    zb_ref[...] = z[:, COL_B:COL_B + B_COLS]
    zc_ref[...] = z[:, COL_C:COL_C + POOL_WIDTH]


def _inproj(x, g, b, cos, sin, win, wq, wqp, wi, wip, apply_ln):
    T = x.shape[0]
    tm = min(T, TM_PROJ)
    row = lambda i: (i, 0)
    fix = lambda i: (0, 0)
    bf16, f32 = jnp.bfloat16, jnp.float32
    col = lambda i: (0, i)
    tok = lambda n, dt: (jax.ShapeDtypeStruct((T, n), dt), pl.BlockSpec((tm, n), row))
    feat = lambda n, dt: (jax.ShapeDtypeStruct((n, T), dt), pl.BlockSpec((n, tm), col))
    outs = [feat(ATT_WIDTH, bf16), feat(IDX_HEADS * IDX_DIM, bf16), feat(IDX_HEADS, f32), tok(HEAD_DIM, bf16),
            tok(HEAD_DIM, bf16), feat(HEAD_DIM, bf16), tok(B_COLS, f32), tok(POOL_WIDTH, f32)]
    shapes = [o[0] for o in outs]
    specs = [o[1] for o in outs]
    if apply_ln:
        shapes = [jax.ShapeDtypeStruct((T, D_MODEL), f32)] + shapes
        specs = [pl.BlockSpec((tm, D_MODEL), row)] + specs
    return pl.pallas_call(
        functools.partial(_inproj_kernel, apply_ln),
        out_shape=tuple(shapes),
        grid=(T // tm,),
        in_specs=[pl.BlockSpec((tm, D_MODEL), row),
                  pl.BlockSpec((1, D_MODEL), fix), pl.BlockSpec((1, D_MODEL), fix),
                  pl.BlockSpec((tm, LANES), row), pl.BlockSpec((tm, LANES), row),
                  pl.BlockSpec((D_MODEL, N_IN_P), fix),
                  pl.BlockSpec((Q_RANK, ATT_WIDTH), fix), pl.BlockSpec((Q_RANK, ATT_WIDTH), fix),
                  pl.BlockSpec((Q_RANK, IDX_HEADS * IDX_DIM), fix),
                  pl.BlockSpec((Q_RANK, IDX_HEADS * IDX_DIM), fix)],
        out_specs=tuple(specs),
        compiler_params=_params("parallel"),
        name="inproj",
    )(x, g, b, cos, sin, win, wq, wqp, wi, wip)


def _attn_block(kend, search, topk, qt_ref, iqt_ref, iwt_ref, kr_ref, ikr_ref, vat_ref, g_ref, o_ref,
                key_ref, bias_ref, ot_ref, hi_ref, lo_ref):
    tq = qt_ref.shape[1]
    q_idx = pl.program_id(1) * tq + lax.broadcasted_iota(jnp.int32, (1, tq), 1)
    k_idx = lax.broadcasted_iota(jnp.int32, (kend, 1), 0)
    causal = k_idx <= q_idx

    if not search:
        bias_ref[:kend, :] = jnp.where(causal, 0.0, -jnp.inf)
    else:
        bias_ref[:kend, :] = jnp.zeros((kend, tq), jnp.float32)

        def idx_pair(j, carry):
            ik = ikr_ref[0:kend, :]
            acc = bias_ref[:kend, :]
            heads = [2 * j, 2 * j + 1]
            lgs = [_dot(ik, iqt_ref[pl.ds(pl.multiple_of(h * IDX_DIM, IDX_DIM), IDX_DIM), :]) for h in heads]
            for h, lg in zip(heads, lgs):
                acc = acc + jnp.maximum(lg, 0.0) * iwt_ref[pl.ds(h, 1), :]
            bias_ref[:kend, :] = acc
            return carry

        lax.fori_loop(0, IDX_HEADS // 2, idx_pair, 0)
        score = bias_ref[:kend, :] + 0.0
        bits = pltpu.bitcast(score, jnp.int32)
        key = bits ^ ((bits >> 31) & 0x7FFFFFFF)
        key_ref[:kend, :] = jnp.where(causal, key, INT_MIN)

        i16 = jnp.int16
        pack = 16
        one_b = jnp.ones((pack, tq), jnp.bfloat16)
        zero_b = jnp.zeros((pack, tq), jnp.bfloat16)

        def count16(ref, cand):
            c16 = jnp.broadcast_to(cand, (pack, tq)).astype(i16)
            accs = [zero_b] * 4
            for n, i in enumerate(range(0, kend, pack)):
                accs[n % 4] = accs[n % 4] + jnp.where(ref[i:i + pack, :] >= c16, one_b, zero_b)
            acc = (accs[0] + accs[1]) + (accs[2] + accs[3])
            return jnp.sum(acc.astype(jnp.float32), axis=0, keepdims=True)

        def kth_largest16(ref, k):
            t = jnp.where(count16(ref, jnp.zeros((1, tq), jnp.int32)) >= k, 0, -32768).astype(jnp.int32)

            def bit_step(i, t):
                cand = t | (jnp.int32(1) << (14 - i))
                return jnp.where(count16(ref, cand) >= k, cand, t)

            return lax.fori_loop(0, 15, bit_step, t)

        keyv = key_ref[:kend, :]
        hi_ref[:kend, :] = (keyv >> 16).astype(i16)
        tau_hi = kth_largest16(hi_ref, jnp.full((1, tq), float(topk), jnp.float32))
        above = jnp.where(tau_hi < 32767, count16(hi_ref, jnp.minimum(tau_hi + 1, 32767)), 0.0)
        lo = ((keyv & 0xFFFF) - 32768).astype(i16)
        same_hi = hi_ref[:kend, :] == jnp.broadcast_to(tau_hi, (kend, tq)).astype(i16)
        lo_ref[:kend, :] = jnp.where(same_hi, lo, -32768)
        tau_lo = kth_largest16(lo_ref, topk - above)
        tau = (tau_hi << 16) | (tau_lo + 32768)
        tau = jnp.maximum(tau, INT_MIN + 1)
        ge = key_ref[:kend, :] >= tau
        n_ge = jnp.sum(jnp.where(ge, 1.0, 0.0), axis=0, keepdims=True)
        bias_ref[:kend, :] = jnp.where(ge, 0.0, -jnp.inf)

        @pl.when(jnp.max(n_ge) > topk)
        def _():
            need = topk - jnp.sum(jnp.where(key_ref[:kend, :] > tau, 1.0, 0.0), axis=0, keepdims=True)
            ri = lax.broadcasted_iota(jnp.int32, (LANES, LANES), 0)
            ci = lax.broadcasted_iota(jnp.int32, (LANES, LANES), 1)
            lower = _bf(jnp.where(ri >= ci, 1.0, 0.0))
            before = jnp.zeros((1, tq), jnp.float32)
            for c in range(0, kend, LANES):
                keyc = key_ref[c:c + LANES, :]
                eq = keyc == tau
                rank = _dot(lower, _bf(jnp.where(eq, 1.0, 0.0))) + before
                keep = (keyc > tau) | (eq & (rank <= need))
                bias_ref[c:c + LANES, :] = jnp.where(keep, 0.0, -jnp.inf)
                before = rank[LANES - 1:LANES, :]

    kc = min(kend, ATT_KEY_CHUNK)
    chunks = range(0, kend, kc)

    def att_pair(j, carry):
        heads = [ATT_GROUP * j + i for i in range(ATT_GROUP)]
        qs = [qt_ref[pl.ds(pl.multiple_of(h * HEAD_DIM, HEAD_DIM), HEAD_DIM), :] for h in heads]
        s_h = [[_dot(kr_ref[c:c + kc, :], q_) + bias_ref[c:c + kc, :] for c in chunks] for q_ in qs]
        for h, s_c in zip(heads, s_h):
            m = jnp.max(functools.reduce(jnp.maximum, s_c), axis=0, keepdims=True)
            p_c = [jnp.exp2(s - m) for s in s_c]
            l = jnp.sum(sum(p_c), axis=0, keepdims=True)
            acc = sum(_dot(vat_ref[:, c:c + kc], _bf(p)) for p, c in zip(p_c, chunks))
            ot_ref[pl.ds(pl.multiple_of(h * HEAD_DIM, HEAD_DIM), HEAD_DIM), :] = acc / l
        return carry

    lax.fori_loop(0, ATT_HEADS // ATT_GROUP, att_pair, 0)
    ot = ot_ref[...]
    ot = ot * lax.rsqrt(jnp.mean(ot * ot, axis=0, keepdims=True) + RMS_EPS) * g_ref[...]
    o_ref[...] = _bf(ot.T)


def _key_extents(S, tq, topk):
    out = []
    for v in range(S // tq):
        need = (v + 1) * tq
        if need <= topk:
            out.append((need, False))
        else:
            out.append((min(S, -(-need // (2 * tq)) * 2 * tq), True))
    return out


def _attn_kernel(S, *refs):
    tq = refs[0].shape[1]
    topk = min(TOPK_MAX, S // 4)
    qi = pl.program_id(1)
    extents = _key_extents(S, tq, topk)
    for ext in sorted(set(extents)):
        blocks = [v for v, e in enumerate(extents) if e == ext]
        pl.when((qi >= blocks[0]) & (qi <= blocks[-1]))(functools.partial(_attn_block, *ext, topk, *refs))


def _attention(B, S, qt, iqt, iwt, kr, ikr, vat, g):
    tq = min(S, TQ)
    nq = S // tq
    qcol = lambda b, i: (0, b * nq + i)
    seq = lambda b, i: (b, 0)
    return pl.pallas_call(
        functools.partial(_attn_kernel, S),
        out_shape=jax.ShapeDtypeStruct((B * S, ATT_WIDTH), jnp.bfloat16),
        grid=(B, nq),
        in_specs=[pl.BlockSpec((ATT_WIDTH, tq), qcol),
                  pl.BlockSpec((IDX_HEADS * IDX_DIM, tq), qcol),
                  pl.BlockSpec((IDX_HEADS, tq), qcol),
                  pl.BlockSpec((S, HEAD_DIM), seq), pl.BlockSpec((S, HEAD_DIM), seq),
                  pl.BlockSpec((HEAD_DIM, S), lambda b, i: (0, b)),
                  pl.BlockSpec((ATT_WIDTH, 1), lambda b, i: (0, 0))],
        out_specs=pl.BlockSpec((tq, ATT_WIDTH), lambda b, i: (b * nq + i, 0)),
        scratch_shapes=[pltpu.VMEM((S, tq), jnp.int32), pltpu.VMEM((S, tq), jnp.float32),
                        pltpu.VMEM((ATT_WIDTH, tq), jnp.float32),
                        pltpu.VMEM((S, tq), jnp.int16), pltpu.VMEM((S, tq), jnp.int16)],
        compiler_params=_params("parallel", "parallel"),
        name="dsa_attention",
    )(qt, iqt, iwt, kr, ikr, vat, g)


def _split3(x):
    hi = _bf(x)
    r1 = x - hi.astype(jnp.float32)
    mid = _bf(r1)
    lo = _bf(r1 - mid.astype(jnp.float32))
    return hi, mid, lo


def _rwkv_kernel(zb_ref, mu_ref, w0_ref, w2_ref, a0_ref, a2_ref, g2_ref, kk_ref, ka_ref, rk_ref,
                 lg_ref, lb_ref, o_ref, prev_ref, state_ref):
    G, C, _ = zb_ref.shape
    W = RWKV_WIDTH
    R = G * C
    H = RWKV_HEADS

    @pl.when(pl.program_id(1) == 0)
    def _():
        prev_ref[...] = jnp.zeros_like(prev_ref)
        state_ref[...] = jnp.zeros_like(state_ref)

    x = zb_ref[...].reshape(R, B_COLS)
    rowi = lax.broadcasted_iota(jnp.int32, (R, 1), 0)
    shifted = pltpu.roll(x, 1, 0)
    for gi in range(G):
        shifted = jnp.where(rowi == gi * C, prev_ref[8 * gi:8 * gi + 1, :], shifted)
        prev_ref[8 * gi:8 * gi + 1, :] = x[gi * C + C - 1:gi * C + C, :]
    xm = x + (shifted - x) * mu_ref[...]
    r = xm[:, 0:W]
    k = xm[:, W:2 * W]
    v = xm[:, 2 * W:3 * W]
    wl = xm[:, 3 * W:3 * W + DECAY_RANK]
    al = xm[:, 3 * W + DECAY_RANK:3 * W + DECAY_RANK + ICLR_RANK]
    gl = xm[:, 3 * W + DECAY_RANK + ICLR_RANK:]

    wpre = w0_ref[...] + _dot(_bf(jnp.tanh(wl)), w2_ref[...])
    nw = -wpre
    w = -(jnp.maximum(nw, 0.0) + jnp.log(1.0 + jnp.exp(-jnp.abs(nw)))) - 0.5
    logd = -jnp.exp(w)
    a = jax.nn.sigmoid(a0_ref[...] + _dot(_bf(al), a2_ref[...]))
    g = _dot(_bf(jax.nn.sigmoid(gl)), g2_ref[...])

    rr = lax.broadcasted_iota(jnp.int32, (R, R), 0)
    cc = lax.broadcasted_iota(jnp.int32, (R, R), 1)
    tri = _bf(jnp.where((rr >= cc) & (rr // C == cc // C), 1.0, 0.0))
    hi, mid, lo = _split3(logd)
    cw = _dot(tri, hi) + _dot(tri, mid) + _dot(tri, lo)
    e_in = jnp.exp(cw)
    e_ex = jnp.exp(cw - logd)
    e_inv = jnp.exp(-cw)

    kk = k * kk_ref[...]
    k2 = k * (1.0 + (a - 1.0) * ka_ref[...])
    rk2 = r * k2 * rk_ref[...]
    ri = lax.broadcasted_iota(jnp.int32, (C, C), 0)
    ci = lax.broadcasted_iota(jnp.int32, (C, C), 1)
    strict = ri > ci
    ri2 = lax.broadcasted_iota(jnp.int32, (C, 2 * C), 0)
    ci2 = lax.broadcasted_iota(jnp.int32, (C, 2 * C), 1)
    incl2 = ri2 >= jnp.where(ci2 >= C, ci2 - C, ci2)
    eye = jnp.where(ri == ci, 1.0, 0.0)

    units = [(gi, h) for gi in range(G) for h in range(H)]

    def cut(arr, u):
        gi, h = u
        return arr[gi * C:(gi + 1) * C, h * HEAD_DIM:(h + 1) * HEAD_DIM]

    def unit_norm(u):
        kkh = cut(kk, u)
        return kkh / jnp.maximum(jnp.sqrt(jnp.sum(kkh * kkh, axis=-1, keepdims=True)), 1e-12)

    kkn = [unit_norm(u) for u in units]
    v_u = [cut(v, u) for u in units]
    at = [_bf(-kn * cut(e_ex, u)) for kn, u in zip(kkn, units)]
    rt = [_bf(cut(r, u) * cut(e_in, u)) for u in units]
    bt = [kn * cut(a, u) * cut(e_inv, u) for kn, u in zip(kkn, units)]
    kt = [cut(k2, u) * cut(e_inv, u) for u in units]
    bk16 = [_bf(jnp.concatenate([b_, k_], axis=0)) for b_, k_ in zip(bt, kt)]
    v16 = [_bf(v_) for v_ in v_u]
    g_a = [_dot_nt(a_, m_) for a_, m_ in zip(at, bk16)]
    l_ab = [jnp.where(strict, g_[:, :C], 0.0) for g_ in g_a]
    a_ak = [_bf(jnp.where(strict, g_[:, C:], 0.0)) for g_ in g_a]
    tinv = [eye + l_ for l_ in l_ab]
    pw = l_ab
    n = 2
    while n < C:
        pw = [_dot(_bf(p_), _bf(p_)) for p_ in pw]
        tinv = [t_ + _dot(_bf(t_), _bf(p_)) for t_, p_ in zip(tinv, pw)]
        n *= 2
    s0 = [state_ref[i] for i in range(len(units))]
    s016 = [_bf(s_) for s_ in s0]
    rhs = [_dot_nt(a_, s_) + _dot(m_, v_) for a_, s_, m_, v_ in zip(at, s016, a_ak, v16)]
    uu = [_dot(_bf(t_), _bf(x_)) for t_, x_ in zip(tinv, rhs)]
    uv16 = [_bf(jnp.concatenate([u_, v_], axis=0)) for u_, v_ in zip(uu, v_u)]
    g_r = [_bf(jnp.where(incl2, _dot_nt(r_, m_), 0.0)) for r_, m_ in zip(rt, bk16)]
    y = [_dot(m_, x_) + _dot_nt(r_, s_) for m_, x_, r_, s_ in zip(g_r, uv16, rt, s016)]
    for i, u in enumerate(units):
        state_ref[i] = (s0[i] + _dot_tn(uv16[i], bk16[i])) * cut(e_in, u)[C - 1:C, :]
    for gi in range(G):
        outs = []
        for h in range(H):
            i = gi * H + h
            u = units[i]
            sl = slice(h * HEAD_DIM, (h + 1) * HEAD_DIM)
            ym = jnp.mean(y[i], axis=-1, keepdims=True)
            yc = y[i] - ym
            yv = jnp.mean(yc * yc, axis=-1, keepdims=True)
            yn = yc * lax.rsqrt(yv + GN_EPS) * lg_ref[:, sl] + lb_ref[:, sl]
            bonus = jnp.sum(cut(rk2, u), axis=-1, keepdims=True) * v_u[i]
            outs.append((yn + bonus) * cut(g, u))
        o_ref[gi] = _bf(jnp.concatenate(outs, axis=1))


def _rwkv(B, S, zb, mu, w0, w2, a0, a2, g2, k_k, k_a, r_k, lnx_g, lnx_b):
    C = min(S, CHUNK)
    G = min(B, RWKV_ROWS)
    blk = lambda b, c: (b, c, 0)
    fix = lambda b, c: (0, 0)
    vec = lambda n: pl.BlockSpec((1, n), fix)
    return pl.pallas_call(
        _rwkv_kernel,
        out_shape=jax.ShapeDtypeStruct((B, S, RWKV_WIDTH), jnp.bfloat16),
        grid=(B // G, S // C),
        in_specs=[pl.BlockSpec((G, C, B_COLS), blk), vec(B_COLS), vec(RWKV_WIDTH),
                  pl.BlockSpec((DECAY_RANK, RWKV_WIDTH), fix), vec(RWKV_WIDTH),
                  pl.BlockSpec((ICLR_RANK, RWKV_WIDTH), fix),
                  pl.BlockSpec((GATE_RANK, RWKV_WIDTH), fix),
                  vec(RWKV_WIDTH), vec(RWKV_WIDTH), vec(RWKV_WIDTH), vec(RWKV_WIDTH),
                  vec(RWKV_WIDTH)],
        out_specs=pl.BlockSpec((G, C, RWKV_WIDTH), blk),
        scratch_shapes=[pltpu.VMEM((8 * G, B_COLS), jnp.float32),
                        pltpu.VMEM((G * RWKV_HEADS, HEAD_DIM, HEAD_DIM), jnp.float32)],
        compiler_params=_params("parallel", "arbitrary"),
        name="rwkv7",
    )(zb.reshape(B, S, B_COLS), mu, w0, w2, a0, a2, g2, k_k, k_a, r_k, lnx_g, lnx_b
      ).reshape(B * S, RWKV_WIDTH)


def _pool_kernel(zc_ref, w_ref, sc_ref, o_ref):
    S = zc_ref.shape[0]
    x = zc_ref[...]
    row = lax.broadcasted_iota(jnp.int32, (S, 1), 0)
    lane_grp = lax.broadcasted_iota(jnp.int32, (1, POOL_WIDTH), 1) // POOL_GROUP

    def lag(y, n):
        return jnp.where(row >= n, pltpu.roll(y, n, 0), 0.0)

    w2 = x + lag(x, 1)
    w4 = w2 + lag(w2, 2)
    w8 = w4 + lag(w4, 4)
    w16 = w8 + lag(w8, 8)
    wsum = jnp.where(lane_grp == 0, w2, jnp.where(lane_grp == 1, w4, jnp.where(lane_grp == 2, w8, w16)))
    win = jnp.where(lane_grp == 0, 2, jnp.where(lane_grp == 1, 4, jnp.where(lane_grp == 2, 8, 16)))
    count = jnp.minimum(row + 1, win).astype(jnp.float32)
    pooled = wsum / count - x
    o_ref[...] = _bf(_dot(_bf(pooled), w_ref[...]) * sc_ref[...])


def _pool(B, S, zc, w_bd, scale):
    return pl.pallas_call(
        _pool_kernel,
        out_shape=jax.ShapeDtypeStruct((B * S, POOL_WIDTH), jnp.bfloat16),
        grid=(B,),
        in_specs=[pl.BlockSpec((S, POOL_WIDTH), lambda b: (b, 0)),
                  pl.BlockSpec((POOL_WIDTH, POOL_WIDTH), lambda b: (0, 0)),
                  pl.BlockSpec((1, POOL_WIDTH), lambda b: (0, 0))],
        out_specs=pl.BlockSpec((S, POOL_WIDTH), lambda b: (b, 0)),
        compiler_params=_params("parallel"),
        name="pool",
    )(zc, w_bd, scale)


def _outmlp_kernel(alpha, oa_ref, ob_ref, oc_ref, x_ref, p_ref, wa_ref, wb_ref, wc_ref, g1_ref, b1_ref,
                   w1_ref, w2_ref, pg_ref, pgb_ref, pp_ref, g2_ref, b2_ref, o_ref):
    tm = x_ref.shape[0]
    halves = [slice(0, tm // 2), slice(tm // 2, tm)]
    mix = [_dot(oa_ref[h, :], wa_ref[...]) + _dot(ob_ref[h, :], wb_ref[...]) + _dot(oc_ref[h, :], wc_ref[...])
           for h in halves]
    x1 = [_ln(alpha * x_ref[h, :] + m, g1_ref[...], b1_ref[...]) for h, m in zip(halves, mix)]
    xb = [_bf(v) for v in x1]
    gate = [jax.nn.sigmoid(_dot(b_, pg_ref[...]) + pgb_ref[...]) for b_ in xb]
    pe = [_dot(_bf(p_ref[0, h, :]), pp_ref[...]) for h in halves]
    acc = [alpha * v + g_ * e_ for v, g_, e_ in zip(x1, gate, pe)]
    xb_all = jnp.concatenate(xb, axis=0)
    nk = D_FF // TF_MLP
    for kf in range(nk):
        cols = slice(kf * TF_MLP, (kf + 1) * TF_MLP)
        hk = jnp.maximum(_dot(xb_all, w1_ref[:, cols]), 0.0)
        h2 = _bf(hk * hk)
        if kf < nk - 1:
            part = _dot(h2, w2_ref[cols, :])
            acc = [a + part[h, :] for a, h in zip(acc, halves)]
        else:
            acc = [a + _dot(h2[h, :], w2_ref[cols, :]) for a, h in zip(acc, halves)]
    for h, a in zip(halves, acc):
        o_ref[h, :] = _ln(a, g2_ref[...], b2_ref[...])


def _outmlp(alpha, layer, oa, ob, oc, x, p, wa, wb, wc, g1, b1, w1, w2, pg, pgb, pp, g2, b2):
    T = x.shape[0]
    tm = min(T, TM_MLP)
    row = lambda i: (i, 0)
    fix = lambda i: (0, 0)
    once = pl.Buffered(1)
    wspec = lambda r, c: pl.BlockSpec((r, c), fix, pipeline_mode=once)
    vec = pl.BlockSpec((1, D_MODEL), fix)
    return pl.pallas_call(
        functools.partial(_outmlp_kernel, alpha),
        out_shape=jax.ShapeDtypeStruct((T, D_MODEL), jnp.float32),
        grid=(T // tm,),
        in_specs=[pl.BlockSpec((tm, ATT_WIDTH), row), pl.BlockSpec((tm, RWKV_WIDTH), row),
                  pl.BlockSpec((tm, POOL_WIDTH), row), pl.BlockSpec((tm, D_MODEL), row),
                  pl.BlockSpec((1, tm, PE_DIM), lambda i: (layer, i, 0)),
                  wspec(ATT_WIDTH, D_MODEL), wspec(RWKV_WIDTH, D_MODEL), wspec(POOL_WIDTH, D_MODEL), vec, vec,
                  wspec(D_MODEL, D_FF), wspec(D_FF, D_MODEL), wspec(D_MODEL, D_MODEL), vec,
                  wspec(PE_DIM, D_MODEL), vec, vec],
        out_specs=pl.BlockSpec((tm, D_MODEL), row),
        compiler_params=_params("parallel"),
        name="outproj_mlp",
    )(oa, ob, oc, x, p, wa, wb, wc, g1, b1, w1, w2, pg, pgb, pp, g2, b2)


def _rot_partner(w):
    half = HEAD_DIM // 2
    return jnp.concatenate([-w[..., half:], w[..., :half]], axis=-1)


def _per_head_partner(w, heads):
    r = w.reshape(w.shape[0], heads, HEAD_DIM)
    return _rot_partner(r).reshape(w.shape)


def _arrange_w_in(w):
    cq = w[:, 0:256]
    ka = w[:, 256:320]
    va = w[:, 320:384]
    ik = w[:, 384:448]
    iw = w[:, 448:456]
    rest = w[:, 456:]
    pad = jnp.zeros((w.shape[0], LANES - HEAD_DIM - IDX_HEADS), w.dtype)
    return jnp.concatenate([cq, ka, ik, _rot_partner(ka), _rot_partner(ik), va, iw, pad, rest], axis=1)


def kernel(x, p, positions, ln_emb_g, ln_emb_b, w_in, w_uq, w_uqi, attn_norm_g, rwkv_mu, rwkv_w0, rwkv_w2, rwkv_a0, rwkv_a2, rwkv_g2, rwkv_k_k, rwkv_k_a, rwkv_r_k, rwkv_lnx_g, rwkv_lnx_b, pool_w, pool_scale, w_out, ln1_g, ln1_b, mlp_w1, mlp_w2, pe_proj, pe_gate, pe_gate_b, ln2_g, ln2_b):
    B, S, _ = x.shape
    T = B * S
    depth = w_in.shape[0]
    alpha = (2 * depth) ** 0.25
    bf = lambda a: a.astype(jnp.bfloat16)
    rowv = lambda a: a.reshape(1, -1)

    cos, sin = _rope_tables(positions.astype(jnp.float32).reshape(T // LANES, LANES))
    xs = x.reshape(T, D_MODEL)
    for i in range(depth):
        win = bf(_arrange_w_in(w_in[i]))
        wq = w_uq[i] * (HEAD_DIM ** -0.5)
        wi = w_uqi[i] * (IDX_DIM ** -0.5)
        res = _inproj(xs, rowv(ln_emb_g), rowv(ln_emb_b), cos, sin, win,
                      bf(wq), bf(_per_head_partner(wq, ATT_HEADS)),
                      bf(wi), bf(_per_head_partner(wi, IDX_HEADS)), apply_ln=(i == 0))
        if i == 0:
            xs, res = res[0], res[1:]
        qt, iqt, iwt, kr, ikr, vat, zb, zc = res
        o_a = _attention(B, S, qt, iqt, iwt, kr, ikr, vat, attn_norm_g[i].reshape(-1, 1))
        o_b = _rwkv(B, S, zb, rowv(rwkv_mu[i]), rowv(rwkv_w0[i]), bf(rwkv_w2[i]), rowv(rwkv_a0[i]),
                    bf(rwkv_a2[i]), bf(rwkv_g2[i]), rowv(rwkv_k_k[i]), rowv(rwkv_k_a[i]),
                    rowv(rwkv_r_k[i]), rowv(rwkv_lnx_g[i]), rowv(rwkv_lnx_b[i]))
        w_bd = jax.scipy.linalg.block_diag(*[pool_w[i, gi] for gi in range(len(POOL_WINDOWS))])
        o_c = _pool(B, S, zc, bf(w_bd), rowv(pool_scale[i]))
        wo = bf(w_out[i])
        xs = _outmlp(alpha, i, o_a, o_b, o_c, xs, p.reshape(depth, T, PE_DIM), wo[:ATT_WIDTH],
                     wo[ATT_WIDTH:ATT_WIDTH + RWKV_WIDTH], wo[ATT_WIDTH + RWKV_WIDTH:], rowv(ln1_g[i]),
                     rowv(ln1_b[i]), bf(mlp_w1[i]), bf(mlp_w2[i]), bf(pe_gate[i]), rowv(pe_gate_b[i]),
                     bf(pe_proj[i]), rowv(ln2_g[i]), rowv(ln2_b[i]))
    return xs.reshape(B, S, D_MODEL)
```

```python
import functools

import numpy as np
import jax
import jax.numpy as jnp
from jax import lax
from jax.experimental import pallas as pl
from jax.experimental.pallas import tpu as pltpu

D_MODEL = 1024
PE_DIM = 256
HEAD_DIM = 64
ROPE_THETA = 10000.0
ATT_WIDTH = 384
ATT_HEADS = 6
Q_RANK = 256
IDX_HEADS = 8
IDX_DIM = 64
TOPK_MAX = 256
RWKV_WIDTH = 384
RWKV_HEADS = 6
DECAY_RANK = 64
ICLR_RANK = 64
GATE_RANK = 128
GN_EPS = 64e-5
POOL_WINDOWS = (2, 4, 8, 16)
POOL_WIDTH = 256
POOL_GROUP = 64
B_COLS = 3 * RWKV_WIDTH + DECAY_RANK + ICLR_RANK + GATE_RANK
D_FF = 4 * D_MODEL
LN_EPS = 1e-5
RMS_EPS = 1e-6

COL_CQ = 0
COL_KIK = 256
COL_KIK_P = 384
COL_VIW = 512
COL_B = 640
COL_C = COL_B + B_COLS
N_IN_P = COL_C + POOL_WIDTH

LANES = 128
INT_MIN = -2 ** 31
LOG2E = 1.4426950408889634
VMEM_LIMIT = 56 * 1024 * 1024

TM_PROJ = 512
TQ = 256
ATT_KEY_CHUNK = 512
ATT_GROUP = 3
CHUNK = 64
RWKV_ROWS = 4
TM_MLP = 512
TF_MLP = 1024

_NT = (((1,), (1,)), ((), ()))
_TN = (((0,), (0,)), ((), ()))


def _params(*sem):
    return pltpu.CompilerParams(dimension_semantics=sem, vmem_limit_bytes=VMEM_LIMIT)


def _ln(x, g, b):
    mu = jnp.mean(x, axis=-1, keepdims=True)
    xc = x - mu
    var = jnp.mean(xc * xc, axis=-1, keepdims=True)
    return xc * lax.rsqrt(var + LN_EPS) * g + b


def _bf(x):
    return x.astype(jnp.bfloat16)


def _dot(a, b):
    return jnp.dot(a, b, preferred_element_type=jnp.float32)


def _dot_nt(a, b):
    return lax.dot_general(a, b, _NT, preferred_element_type=jnp.float32)


def _dot_tn(a, b):
    return lax.dot_general(a, b, _TN, preferred_element_type=jnp.float32)


def _rope_kernel(pos_ref, inv_ref, cos_ref, sin_ref, col_ref):
    quarter = LANES // 4
    grp = lax.broadcasted_iota(jnp.int32, (1, LANES), 1) // quarter
    for r in range(pos_ref.shape[0]):
        col_ref[...] = jnp.broadcast_to(pos_ref[r:r + 1, :], (LANES, LANES)).T
        p4 = col_ref[pl.ds(0, quarter, stride=4), :]
        for j in range(1, 4):
            p4 = jnp.where(grp == j, col_ref[pl.ds(j, quarter, stride=4), :], p4)
        ang4 = p4 * inv_ref[...]
        for tab_ref, fn in ((cos_ref, jnp.cos), (sin_ref, jnp.sin)):
            t4 = fn(ang4)
            rolls = [t4] + [pltpu.roll(t4, quarter * k, 1) for k in range(1, 4)]
            for j in range(4):
                y = rolls[(-j) % 4]
                for g in range(1, 4):
                    y = jnp.where(grp == g, rolls[(g - j) % 4], y)
                tab_ref[pl.ds(r * LANES + j, quarter, stride=4), :] = y


def _rope_tables(pos_f32):
    T = pos_f32.shape[0] * LANES
    rows = min(pos_f32.shape[0], 16)
    inv = ROPE_THETA ** (-np.arange(0, HEAD_DIM, 2, dtype=np.float32) / HEAD_DIM)
    inv128 = jnp.asarray(np.tile(inv, LANES // (HEAD_DIM // 2))[None, :], jnp.float32)
    return pl.pallas_call(
        _rope_kernel,
        out_shape=(jax.ShapeDtypeStruct((T, LANES), jnp.float32),) * 2,
        grid=(pos_f32.shape[0] // rows,),
        in_specs=[pl.BlockSpec((rows, LANES), lambda i: (i, 0)),
                  pl.BlockSpec((1, LANES), lambda i: (0, 0))],
        out_specs=(pl.BlockSpec((rows * LANES, LANES), lambda i: (i, 0)),) * 2,
        scratch_shapes=[pltpu.VMEM((LANES, LANES), jnp.float32)],
        compiler_params=_params("parallel"),
        name="rope_tables",
    )(pos_f32, inv128)


def _inproj_kernel(apply_ln, x_ref, g_ref, b_ref, cos_ref, sin_ref, win_ref, wq_ref, wqp_ref,
                   wi_ref, wip_ref, *out_refs):
    if apply_ln:
        xn_ref, qt_ref, iqt_ref, iwt_ref, kr_ref, ikr_ref, vat_ref, zb_ref, zc_ref = out_refs
        x = _ln(x_ref[...], g_ref[...], b_ref[...])
        xn_ref[...] = x
    else:
        qt_ref, iqt_ref, iwt_ref, kr_ref, ikr_ref, vat_ref, zb_ref, zc_ref = out_refs
        x = x_ref[...]
    z = _dot(_bf(x), win_ref[...])
    cos = cos_ref[...]
    sin = sin_ref[...]
    cq = _bf(z[:, COL_CQ:COL_CQ + Q_RANK])
    cos3 = jnp.concatenate([cos] * 3, axis=1)
    sin3 = jnp.concatenate([sin] * 3, axis=1)
    qt_ref[...] = _bf(((_dot(cq, wq_ref[...]) * cos3 + _dot(cq, wqp_ref[...]) * sin3) * LOG2E).T)
    cos4 = jnp.concatenate([cos] * 4, axis=1)
    sin4 = jnp.concatenate([sin] * 4, axis=1)
    iqt_ref[...] = _bf((_dot(cq, wi_ref[...]) * cos4 + _dot(cq, wip_ref[...]) * sin4).T)
    kik = z[:, COL_KIK:COL_KIK + LANES] * cos + z[:, COL_KIK_P:COL_KIK_P + LANES] * sin
    kr_ref[...] = _bf(kik[:, :HEAD_DIM])
    ikr_ref[...] = _bf(kik[:, HEAD_DIM:])
    viw_t = z[:, COL_VIW:COL_VIW + LANES].T
    vat_ref[...] = _bf(viw_t[:HEAD_DIM])
    iwt_ref[...] = viw_t[HEAD_DIM:HEAD_DIM + IDX_HEADS] * (IDX_HEADS ** -0.5)
    zb_ref[...] = z[:, COL_B:COL_B + B_COLS]
    zc_ref[...] = z[:, COL_C:COL_C + POOL_WIDTH]


def _inproj(x, g, b, cos, sin, win, wq, wqp, wi, wip, apply_ln):
    T = x.shape[0]
    tm = min(T, TM_PROJ)
    row = lambda i: (i, 0)
    fix = lambda i: (0, 0)
    bf16, f32 = jnp.bfloat16, jnp.float32
    col = lambda i: (0, i)
    tok = lambda n, dt: (jax.ShapeDtypeStruct((T, n), dt), pl.BlockSpec((tm, n), row))
    feat = lambda n, dt: (jax.ShapeDtypeStruct((n, T), dt), pl.BlockSpec((n, tm), col))
    outs = [feat(ATT_WIDTH, bf16), feat(IDX_HEADS * IDX_DIM, bf16), feat(IDX_HEADS, f32), tok(HEAD_DIM, bf16),
            tok(HEAD_DIM, bf16), feat(HEAD_DIM, bf16), tok(B_COLS, f32), tok(POOL_WIDTH, f32)]
    shapes = [o[0] for o in outs]
    specs = [o[1] for o in outs]
    if apply_ln:
        shapes = [jax.ShapeDtypeStruct((T, D_MODEL), f32)] + shapes
        specs = [pl.BlockSpec((tm, D_MODEL), row)] + specs
    return pl.pallas_call(
        functools.partial(_inproj_kernel, apply_ln),
        out_shape=tuple(shapes),
        grid=(T // tm,),
        in_specs=[pl.BlockSpec((tm, D_MODEL), row),
                  pl.BlockSpec((1, D_MODEL), fix), pl.BlockSpec((1, D_MODEL), fix),
                  pl.BlockSpec((tm, LANES), row), pl.BlockSpec((tm, LANES), row),
                  pl.BlockSpec((D_MODEL, N_IN_P), fix),
                  pl.BlockSpec((Q_RANK, ATT_WIDTH), fix), pl.BlockSpec((Q_RANK, ATT_WIDTH), fix),
                  pl.BlockSpec((Q_RANK, IDX_HEADS * IDX_DIM), fix),
                  pl.BlockSpec((Q_RANK, IDX_HEADS * IDX_DIM), fix)],
        out_specs=tuple(specs),
        compiler_params=_params("parallel"),
        name="inproj",
    )(x, g, b, cos, sin, win, wq, wqp, wi, wip)


def _attn_block(kend, search, topk, qt_ref, iqt_ref, iwt_ref, kr_ref, ikr_ref, vat_ref, g_ref, o_ref,
                key_ref, bias_ref, ot_ref, hi_ref, lo_ref):
    tq = qt_ref.shape[1]
    q_idx = pl.program_id(1) * tq + lax.broadcasted_iota(jnp.int32, (1, tq), 1)
    k_idx = lax.broadcasted_iota(jnp.int32, (kend, 1), 0)
    causal = k_idx <= q_idx

    if not search:
        bias_ref[:kend, :] = jnp.where(causal, 0.0, -jnp.inf)
    else:
        def idx_pair(j, first):
            ik = ikr_ref[0:kend, :]
            heads = [2 * j, 2 * j + 1]
            rows = [h * IDX_DIM if first else pl.multiple_of(h * IDX_DIM, IDX_DIM) for h in heads]
            lgs = [_dot(ik, iqt_ref[pl.ds(r, IDX_DIM), :]) for r in rows]
            acc = None if first else bias_ref[:kend, :]
            for h, lg in zip(heads, lgs):
                term = jnp.maximum(lg, 0.0) * iwt_ref[pl.ds(h, 1), :]
                acc = term if acc is None else acc + term
            bias_ref[:kend, :] = acc

        idx_pair(0, True)
        lax.fori_loop(1, IDX_HEADS // 2, lambda j, c: (idx_pair(j, False), c)[1], 0)
        score = bias_ref[:kend, :] + 0.0
        bits = pltpu.bitcast(score, jnp.int32)
        key = bits ^ ((bits >> 31) & 0x7FFFFFFF)
        key_ref[:kend, :] = jnp.where(causal, key, INT_MIN)

        i16 = jnp.int16
        pack = 16
        one_b = jnp.ones((pack, tq), jnp.bfloat16)
        zero_b = jnp.zeros((pack, tq), jnp.bfloat16)

        def count16(ref, cand):
            c16 = jnp.broadcast_to(cand, (pack, tq)).astype(i16)
            accs = [zero_b] * 4
            for n, i in enumerate(range(0, kend, pack)):
                accs[n % 4] = accs[n % 4] + jnp.where(ref[i:i + pack, :] >= c16, one_b, zero_b)
            acc = (accs[0] + accs[1]) + (accs[2] + accs[3])
            return jnp.sum(acc.astype(jnp.float32), axis=0, keepdims=True)

        def kth_largest16(ref, k):
            t = jnp.where(count16(ref, jnp.zeros((1, tq), jnp.int32)) >= k, 0, -32768).astype(jnp.int32)

            def bit_step(i, t):
                cand = t | (jnp.int32(1) << (14 - i))
                return jnp.where(count16(ref, cand) >= k, cand, t)

            return lax.fori_loop(0, 15, bit_step, t)

        keyv = key_ref[:kend, :]
        hi_ref[:kend, :] = (keyv >> 16).astype(i16)
        tau_hi = kth_largest16(hi_ref, jnp.full((1, tq), float(topk), jnp.float32))
        above = jnp.where(tau_hi < 32767, count16(hi_ref, jnp.minimum(tau_hi + 1, 32767)), 0.0)
        lo = ((keyv & 0xFFFF) - 32768).astype(i16)
        same_hi = hi_ref[:kend, :] == jnp.broadcast_to(tau_hi, (kend, tq)).astype(i16)
        lo_ref[:kend, :] = jnp.where(same_hi, lo, -32768)
        tau_lo = kth_largest16(lo_ref, topk - above)
        tau = (tau_hi << 16) | (tau_lo + 32768)
        tau = jnp.maximum(tau, INT_MIN + 1)
        ge = key_ref[:kend, :] >= tau
        n_ge = jnp.sum(jnp.where(ge, 1.0, 0.0), axis=0, keepdims=True)
        bias_ref[:kend, :] = jnp.where(ge, 0.0, -jnp.inf)

        @pl.when(jnp.max(n_ge) > topk)
        def _():
            need = topk - jnp.sum(jnp.where(key_ref[:kend, :] > tau, 1.0, 0.0), axis=0, keepdims=True)
            ri = lax.broadcasted_iota(jnp.int32, (LANES, LANES), 0)
            ci = lax.broadcasted_iota(jnp.int32, (LANES, LANES), 1)
            lower = _bf(jnp.where(ri >= ci, 1.0, 0.0))
            before = jnp.zeros((1, tq), jnp.float32)
            for c in range(0, kend, LANES):
                keyc = key_ref[c:c + LANES, :]
                eq = keyc == tau
                rank = _dot(lower, _bf(jnp.where(eq, 1.0, 0.0))) + before
                keep = (keyc > tau) | (eq & (rank <= need))
                bias_ref[c:c + LANES, :] = jnp.where(keep, 0.0, -jnp.inf)
                before = rank[LANES - 1:LANES, :]

    kc = min(kend, ATT_KEY_CHUNK)
    chunks = range(0, kend, kc)

    def att_pair(j, carry):
        heads = [ATT_GROUP * j + i for i in range(ATT_GROUP)]
        qs = [qt_ref[pl.ds(pl.multiple_of(h * HEAD_DIM, HEAD_DIM), HEAD_DIM), :] for h in heads]
        s_h = [[_dot(kr_ref[c:c + kc, :], q_) + bias_ref[c:c + kc, :] for c in chunks] for q_ in qs]
        for h, s_c in zip(heads, s_h):
            m = jnp.max(functools.reduce(jnp.maximum, s_c), axis=0, keepdims=True)
            p_c = [jnp.exp2(s - m) for s in s_c]
            l = jnp.sum(sum(p_c), axis=0, keepdims=True)
            acc = sum(_dot(vat_ref[:, c:c + kc], _bf(p)) for p, c in zip(p_c, chunks))
            ot_ref[pl.ds(pl.multiple_of(h * HEAD_DIM, HEAD_DIM), HEAD_DIM), :] = acc / l
        return carry

    lax.fori_loop(0, ATT_HEADS // ATT_GROUP, att_pair, 0)
    ot = ot_ref[...]
    ot = ot * lax.rsqrt(jnp.mean(ot * ot, axis=0, keepdims=True) + RMS_EPS) * g_ref[...]
    o_ref[...] = _bf(ot.T)


def _key_extents(S, tq, topk):
    out = []
    for v in range(S // tq):
        need = (v + 1) * tq
        if need <= topk:
            out.append((need, False))
        else:
            out.append((min(S, -(-need // (2 * tq)) * 2 * tq), True))
    return out


def _attn_kernel(S, *refs):
    tq = refs[0].shape[1]
    topk = min(TOPK_MAX, S // 4)
    qi = pl.program_id(1)
    extents = _key_extents(S, tq, topk)
    for ext in sorted(set(extents)):
        blocks = [v for v, e in enumerate(extents) if e == ext]
        pl.when((qi >= blocks[0]) & (qi <= blocks[-1]))(functools.partial(_attn_block, *ext, topk, *refs))


def _attention(B, S, qt, iqt, iwt, kr, ikr, vat, g):
    tq = min(S, TQ)
    nq = S // tq
    qcol = lambda b, i: (0, b * nq + i)
    seq = lambda b, i: (b, 0)
    return pl.pallas_call(
        functools.partial(_attn_kernel, S),
        out_shape=jax.ShapeDtypeStruct((B * S, ATT_WIDTH), jnp.bfloat16),
        grid=(B, nq),
        in_specs=[pl.BlockSpec((ATT_WIDTH, tq), qcol),
                  pl.BlockSpec((IDX_HEADS * IDX_DIM, tq), qcol),
                  pl.BlockSpec((IDX_HEADS, tq), qcol),
                  pl.BlockSpec((S, HEAD_DIM), seq), pl.BlockSpec((S, HEAD_DIM), seq),
                  pl.BlockSpec((HEAD_DIM, S), lambda b, i: (0, b)),
                  pl.BlockSpec((ATT_WIDTH, 1), lambda b, i: (0, 0))],
        out_specs=pl.BlockSpec((tq, ATT_WIDTH), lambda b, i: (b * nq + i, 0)),
        scratch_shapes=[pltpu.VMEM((S, tq), jnp.int32), pltpu.VMEM((S, tq), jnp.float32),
                        pltpu.VMEM((ATT_WIDTH, tq), jnp.float32),
                        pltpu.VMEM((S, tq), jnp.int16), pltpu.VMEM((S, tq), jnp.int16)],
        compiler_params=_params("parallel", "parallel"),
        name="dsa_attention",
    )(qt, iqt, iwt, kr, ikr, vat, g)


def _split3(x):
    hi = _bf(x)
    r1 = x - hi.astype(jnp.float32)
    mid = _bf(r1)
    lo = _bf(r1 - mid.astype(jnp.float32))
    return hi, mid, lo


def _rwkv_kernel(zb_ref, mu_ref, w0_ref, w2_ref, a0_ref, a2_ref, g2_ref, kk_ref, ka_ref, rk_ref,
                 lg_ref, lb_ref, o_ref, prev_ref, state_ref):
    G, C, _ = zb_ref.shape
    W = RWKV_WIDTH
    R = G * C
    H = RWKV_HEADS

    @pl.when(pl.program_id(1) == 0)
    def _():
        prev_ref[...] = jnp.zeros_like(prev_ref)
        state_ref[...] = jnp.zeros_like(state_ref)

    x = zb_ref[...].reshape(R, B_COLS)
    rowi = lax.broadcasted_iota(jnp.int32, (R, 1), 0)
    shifted = pltpu.roll(x, 1, 0)
    for gi in range(G):
        shifted = jnp.where(rowi == gi * C, prev_ref[8 * gi:8 * gi + 1, :], shifted)
        prev_ref[8 * gi:8 * gi + 1, :] = x[gi * C + C - 1:gi * C + C, :]
    xm = x + (shifted - x) * mu_ref[...]
    r = xm[:, 0:W]
    k = xm[:, W:2 * W]
    v = xm[:, 2 * W:3 * W]
    wl = xm[:, 3 * W:3 * W + DECAY_RANK]
    al = xm[:, 3 * W + DECAY_RANK:3 * W + DECAY_RANK + ICLR_RANK]
    gl = xm[:, 3 * W + DECAY_RANK + ICLR_RANK:]

    wpre = w0_ref[...] + _dot(_bf(jnp.tanh(wl)), w2_ref[...])
    nw = -wpre
    w = -(jnp.maximum(nw, 0.0) + jnp.log(1.0 + jnp.exp(-jnp.abs(nw)))) - 0.5
    logd = -jnp.exp(w)
    a = jax.nn.sigmoid(a0_ref[...] + _dot(_bf(al), a2_ref[...]))
    g = _dot(_bf(jax.nn.sigmoid(gl)), g2_ref[...])

    rr = lax.broadcasted_iota(jnp.int32, (R, R), 0)
    cc = lax.broadcasted_iota(jnp.int32, (R, R), 1)
    tri = _bf(jnp.where((rr >= cc) & (rr // C == cc // C), 1.0, 0.0))
    hi, mid, lo = _split3(logd)
    cw = _dot(tri, hi) + _dot(tri, mid) + _dot(tri, lo)
    e_in = jnp.exp(cw)
    e_ex = jnp.exp(cw - logd)
    e_inv = jnp.exp(-cw)

    kk = k * kk_ref[...]
    k2 = k * (1.0 + (a - 1.0) * ka_ref[...])
    rk2 = r * k2 * rk_ref[...]
    ri = lax.broadcasted_iota(jnp.int32, (C, C), 0)
    ci = lax.broadcasted_iota(jnp.int32, (C, C), 1)
    strict = ri > ci
    ri2 = lax.broadcasted_iota(jnp.int32, (C, 2 * C), 0)
    ci2 = lax.broadcasted_iota(jnp.int32, (C, 2 * C), 1)
    incl2 = ri2 >= jnp.where(ci2 >= C, ci2 - C, ci2)
    eye = jnp.where(ri == ci, 1.0, 0.0)

    units = [(gi, h) for gi in range(G) for h in range(H)]

    def cut(arr, u):
        gi, h = u
        return arr[gi * C:(gi + 1) * C, h * HEAD_DIM:(h + 1) * HEAD_DIM]

    def unit_norm(u):
        kkh = cut(kk, u)
        return kkh / jnp.maximum(jnp.sqrt(jnp.sum(kkh * kkh, axis=-1, keepdims=True)), 1e-12)

    kkn = [unit_norm(u) for u in units]
    v_u = [cut(v, u) for u in units]
    at = [_bf(-kn * cut(e_ex, u)) for kn, u in zip(kkn, units)]
    rt = [_bf(cut(r, u) * cut(e_in, u)) for u in units]
    bt = [kn * cut(a, u) * cut(e_inv, u) for kn, u in zip(kkn, units)]
    kt = [cut(k2, u) * cut(e_inv, u) for u in units]
    bk16 = [_bf(jnp.concatenate([b_, k_], axis=0)) for b_, k_ in zip(bt, kt)]
    v16 = [_bf(v_) for v_ in v_u]
    g_a = [_dot_nt(a_, m_) for a_, m_ in zip(at, bk16)]
    l_ab = [jnp.where(strict, g_[:, :C], 0.0) for g_ in g_a]
    a_ak = [_bf(jnp.where(strict, g_[:, C:], 0.0)) for g_ in g_a]
    tinv = [eye + l_ for l_ in l_ab]
    pw = l_ab
    n = 2
    while n < C:
        pw = [_dot(_bf(p_), _bf(p_)) for p_ in pw]
        tinv = [t_ + _dot(_bf(t_), _bf(p_)) for t_, p_ in zip(tinv, pw)]
        n *= 2
    s0 = [state_ref[i] for i in range(len(units))]
    s016 = [_bf(s_) for s_ in s0]
    rhs = [_dot_nt(a_, s_) + _dot(m_, v_) for a_, s_, m_, v_ in zip(at, s016, a_ak, v16)]
    uu = [_dot(_bf(t_), _bf(x_)) for t_, x_ in zip(tinv, rhs)]
    uv16 = [_bf(jnp.concatenate([u_, v_], axis=0)) for u_, v_ in zip(uu, v_u)]
    g_r = [_bf(jnp.where(incl2, _dot_nt(r_, m_), 0.0)) for r_, m_ in zip(rt, bk16)]
    y = [_dot(m_, x_) + _dot_nt(r_, s_) for m_, x_, r_, s_ in zip(g_r, uv16, rt, s016)]
    for i, u in enumerate(units):
        state_ref[i] = (s0[i] + _dot_tn(uv16[i], bk16[i])) * cut(e_in, u)[C - 1:C, :]
    for gi in range(G):
        outs = []
        for h in range(H):
            i = gi * H + h
            u = units[i]
            sl = slice(h * HEAD_DIM, (h + 1) * HEAD_DIM)
            ym = jnp.mean(y[i], axis=-1, keepdims=True)
            yc = y[i] - ym
            yv = jnp.mean(yc * yc, axis=-1, keepdims=True)
            yn = yc * lax.rsqrt(yv + GN_EPS) * lg_ref[:, sl] + lb_ref[:, sl]
            bonus = jnp.sum(cut(rk2, u), axis=-1, keepdims=True) * v_u[i]
            outs.append((yn + bonus) * cut(g, u))
        o_ref[gi] = _bf(jnp.concatenate(outs, axis=1))


def _rwkv(B, S, zb, mu, w0, w2, a0, a2, g2, k_k, k_a, r_k, lnx_g, lnx_b):
    C = min(S, CHUNK)
    G = min(B, RWKV_ROWS)
    blk = lambda b, c: (b, c, 0)
    fix = lambda b, c: (0, 0)
    vec = lambda n: pl.BlockSpec((1, n), fix)
    return pl.pallas_call(
        _rwkv_kernel,
        out_shape=jax.ShapeDtypeStruct((B, S, RWKV_WIDTH), jnp.bfloat16),
        grid=(B // G, S // C),
        in_specs=[pl.BlockSpec((G, C, B_COLS), blk), vec(B_COLS), vec(RWKV_WIDTH),
                  pl.BlockSpec((DECAY_RANK, RWKV_WIDTH), fix), vec(RWKV_WIDTH),
                  pl.BlockSpec((ICLR_RANK, RWKV_WIDTH), fix),
                  pl.BlockSpec((GATE_RANK, RWKV_WIDTH), fix),
                  vec(RWKV_WIDTH), vec(RWKV_WIDTH), vec(RWKV_WIDTH), vec(RWKV_WIDTH),
                  vec(RWKV_WIDTH)],
        out_specs=pl.BlockSpec((G, C, RWKV_WIDTH), blk),
        scratch_shapes=[pltpu.VMEM((8 * G, B_COLS), jnp.float32),
                        pltpu.VMEM((G * RWKV_HEADS, HEAD_DIM, HEAD_DIM), jnp.float32)],
        compiler_params=_params("parallel", "arbitrary"),
        name="rwkv7",
    )(zb.reshape(B, S, B_COLS), mu, w0, w2, a0, a2, g2, k_k, k_a, r_k, lnx_g, lnx_b
      ).reshape(B * S, RWKV_WIDTH)


def _pool_kernel(zc_ref, w_ref, sc_ref, o_ref):
    S = zc_ref.shape[0]
    x = zc_ref[...]
    row = lax.broadcasted_iota(jnp.int32, (S, 1), 0)
    lane_grp = lax.broadcasted_iota(jnp.int32, (1, POOL_WIDTH), 1) // POOL_GROUP

    def lag(y, n):
        return jnp.where(row >= n, pltpu.roll(y, n, 0), 0.0)

    w2 = x + lag(x, 1)
    w4 = w2 + lag(w2, 2)
    w8 = w4 + lag(w4, 4)
    w16 = w8 + lag(w8, 8)
    wsum = jnp.where(lane_grp == 0, w2, jnp.where(lane_grp == 1, w4, jnp.where(lane_grp == 2, w8, w16)))
    win = jnp.where(lane_grp == 0, 2, jnp.where(lane_grp == 1, 4, jnp.where(lane_grp == 2, 8, 16)))
    count = jnp.minimum(row + 1, win).astype(jnp.float32)
    pooled = wsum / count - x
    o_ref[...] = _bf(_dot(_bf(pooled), w_ref[...]) * sc_ref[...])


def _pool(B, S, zc, w_bd, scale):
    return pl.pallas_call(
        _pool_kernel,
        out_shape=jax.ShapeDtypeStruct((B * S, POOL_WIDTH), jnp.bfloat16),
        grid=(B,),
        in_specs=[pl.BlockSpec((S, POOL_WIDTH), lambda b: (b, 0)),
                  pl.BlockSpec((POOL_WIDTH, POOL_WIDTH), lambda b: (0, 0)),
                  pl.BlockSpec((1, POOL_WIDTH), lambda b: (0, 0))],
        out_specs=pl.BlockSpec((S, POOL_WIDTH), lambda b: (b, 0)),
        compiler_params=_params("parallel"),
        name="pool",
    )(zc, w_bd, scale)


def _outmlp_kernel(alpha, oa_ref, ob_ref, oc_ref, x_ref, p_ref, wa_ref, wb_ref, wc_ref, g1_ref, b1_ref,
                   w1_ref, w2_ref, pg_ref, pgb_ref, pp_ref, g2_ref, b2_ref, o_ref):
    tm = x_ref.shape[0]
    halves = [slice(0, tm // 2), slice(tm // 2, tm)]
    mix = [_dot(oa_ref[h, :], wa_ref[...]) + _dot(ob_ref[h, :], wb_ref[...]) + _dot(oc_ref[h, :], wc_ref[...])
           for h in halves]
    x1 = [_ln(alpha * x_ref[h, :] + m, g1_ref[...], b1_ref[...]) for h, m in zip(halves, mix)]
    xb = [_bf(v) for v in x1]
    gate = [jax.nn.sigmoid(_dot(b_, pg_ref[...]) + pgb_ref[...]) for b_ in xb]
    pe = [_dot(_bf(p_ref[0, h, :]), pp_ref[...]) for h in halves]
    acc = [alpha * v + g_ * e_ for v, g_, e_ in zip(x1, gate, pe)]
    xb_all = jnp.concatenate(xb, axis=0)
    nk = D_FF // TF_MLP
    for kf in range(nk):
        cols = slice(kf * TF_MLP, (kf + 1) * TF_MLP)
        hk = jnp.maximum(_dot(xb_all, w1_ref[:, cols]), 0.0)
        h2 = _bf(hk * hk)
        if kf < nk - 1:
            part = _dot(h2, w2_ref[cols, :])
            acc = [a + part[h, :] for a, h in zip(acc, halves)]
        else:
            acc = [a + _dot(h2[h, :], w2_ref[cols, :]) for a, h in zip(acc, halves)]
    for h, a in zip(halves, acc):
        o_ref[h, :] = _ln(a, g2_ref[...], b2_ref[...])


def _outmlp(alpha, layer, oa, ob, oc, x, p, wa, wb, wc, g1, b1, w1, w2, pg, pgb, pp, g2, b2):
    T = x.shape[0]
    tm = min(T, TM_MLP)
    row = lambda i: (i, 0)
    fix = lambda i: (0, 0)
    once = pl.Buffered(1)
    wspec = lambda r, c: pl.BlockSpec((r, c), fix, pipeline_mode=once)
    vec = pl.BlockSpec((1, D_MODEL), fix)
    return pl.pallas_call(
        functools.partial(_outmlp_kernel, alpha),
        out_shape=jax.ShapeDtypeStruct((T, D_MODEL), jnp.float32),
        grid=(T // tm,),
        in_specs=[pl.BlockSpec((tm, ATT_WIDTH), row), pl.BlockSpec((tm, RWKV_WIDTH), row),
                  pl.BlockSpec((tm, POOL_WIDTH), row), pl.BlockSpec((tm, D_MODEL), row),
                  pl.BlockSpec((1, tm, PE_DIM), lambda i: (layer, i, 0)),
                  wspec(ATT_WIDTH, D_MODEL), wspec(RWKV_WIDTH, D_MODEL), wspec(POOL_WIDTH, D_MODEL), vec, vec,
                  wspec(D_MODEL, D_FF), wspec(D_FF, D_MODEL), wspec(D_MODEL, D_MODEL), vec,
                  wspec(PE_DIM, D_MODEL), vec, vec],
        out_specs=pl.BlockSpec((tm, D_MODEL), row),
        compiler_params=_params("parallel"),
        name="outproj_mlp",
    )(oa, ob, oc, x, p, wa, wb, wc, g1, b1, w1, w2, pg, pgb, pp, g2, b2)


def _rot_partner(w):
    half = HEAD_DIM // 2
    return jnp.concatenate([-w[..., half:], w[..., :half]], axis=-1)


def _per_head_partner(w, heads):
    r = w.reshape(w.shape[0], heads, HEAD_DIM)
    return _rot_partner(r).reshape(w.shape)


def _arrange_w_in(w):
    cq = w[:, 0:256]
    ka = w[:, 256:320]
    va = w[:, 320:384]
    ik = w[:, 384:448]
    iw = w[:, 448:456]
    rest = w[:, 456:]
    pad = jnp.zeros((w.shape[0], LANES - HEAD_DIM - IDX_HEADS), w.dtype)
    return jnp.concatenate([cq, ka, ik, _rot_partner(ka), _rot_partner(ik), va, iw, pad, rest], axis=1)


def kernel(x, p, positions, ln_emb_g, ln_emb_b, w_in, w_uq, w_uqi, attn_norm_g, rwkv_mu, rwkv_w0, rwkv_w2, rwkv_a0, rwkv_a2, rwkv_g2, rwkv_k_k, rwkv_k_a, rwkv_r_k, rwkv_lnx_g, rwkv_lnx_b, pool_w, pool_scale, w_out, ln1_g, ln1_b, mlp_w1, mlp_w2, pe_proj, pe_gate, pe_gate_b, ln2_g, ln2_b):
    B, S, _ = x.shape
    T = B * S
    depth = w_in.shape[0]
    alpha = (2 * depth) ** 0.25
    bf = lambda a: a.astype(jnp.bfloat16)
    rowv = lambda a: a.reshape(1, -1)

    cos, sin = _rope_tables(positions.astype(jnp.float32).reshape(T // LANES, LANES))
    xs = x.reshape(T, D_MODEL)
    for i in range(depth):
        win = bf(_arrange_w_in(w_in[i]))
        wq = w_uq[i] * (HEAD_DIM ** -0.5)
        wi = w_uqi[i] * (IDX_DIM ** -0.5)
        res = _inproj(xs, rowv(ln_emb_g), rowv(ln_emb_b), cos, sin, win,
                      bf(wq), bf(_per_head_partner(wq, ATT_HEADS)),
                      bf(wi), bf(_per_head_partner(wi, IDX_HEADS)), apply_ln=(i == 0))
        if i == 0:
            xs, res = res[0], res[1:]
        qt, iqt, iwt, kr, ikr, vat, zb, zc = res
        o_a = _attention(B, S, qt, iqt, iwt, kr, ikr, vat, attn_norm_g[i].reshape(-1, 1))
        o_b = _rwkv(B, S, zb, rowv(rwkv_mu[i]), rowv(rwkv_w0[i]), bf(rwkv_w2[i]), rowv(rwkv_a0[i]),
                    bf(rwkv_a2[i]), bf(rwkv_g2[i]), rowv(rwkv_k_k[i]), rowv(rwkv_k_a[i]),
                    rowv(rwkv_r_k[i]), rowv(rwkv_lnx_g[i]), rowv(rwkv_lnx_b[i]))
        w_bd = jax.scipy.linalg.block_diag(*[pool_w[i, gi] for gi in range(len(POOL_WINDOWS))])
        o_c = _pool(B, S, zc, bf(w_bd), rowv(pool_scale[i]))
        wo = bf(w_out[i])
        xs = _outmlp(alpha, i, o_a, o_b, o_c, xs, p.reshape(depth, T, PE_DIM), wo[:ATT_WIDTH],
                     wo[ATT_WIDTH:ATT_WIDTH + RWKV_WIDTH], wo[ATT_WIDTH + RWKV_WIDTH:], rowv(ln1_g[i]),
                     rowv(ln1_b[i]), bf(mlp_w1[i]), bf(mlp_w2[i]), bf(pe_gate[i]), rowv(pe_gate_b[i]),
                     bf(pe_proj[i]), rowv(ln2_g[i]), rowv(ln2_b[i]))
    return xs.reshape(B, S, D_MODEL)
```

```python
import functools

import numpy as np
import jax
import jax.numpy as jnp
from jax import lax
from jax.experimental import pallas as pl
from jax.experimental.pallas import tpu as pltpu

D_MODEL = 1024
PE_DIM = 256
HEAD_DIM = 64
ROPE_THETA = 10000.0
ATT_WIDTH = 384
ATT_HEADS = 6
Q_RANK = 256
IDX_HEADS = 8
IDX_DIM = 64
TOPK_MAX = 256
RWKV_WIDTH = 384
RWKV_HEADS = 6
DECAY_RANK = 64
ICLR_RANK = 64
GATE_RANK = 128
GN_EPS = 64e-5
POOL_WINDOWS = (2, 4, 8, 16)
POOL_WIDTH = 256
POOL_GROUP = 64
B_COLS = 3 * RWKV_WIDTH + DECAY_RANK + ICLR_RANK + GATE_RANK
D_FF = 4 * D_MODEL
LN_EPS = 1e-5
RMS_EPS = 1e-6

COL_CQ = 0
COL_KIK = 256
COL_KIK_P = 384
COL_VIW = 512
COL_B = 640
COL_C = COL_B + B_COLS
N_IN_P = COL_C + POOL_WIDTH

LANES = 128
INT_MIN = -2 ** 31
LOG2E = 1.4426950408889634
VMEM_LIMIT = 56 * 1024 * 1024

TM_PROJ = 512
TQ = 256
ATT_KEY_CHUNK = 512
ATT_GROUP = 3
CHUNK = 64
RWKV_ROWS = 4
TM_MLP = 512
TF_MLP = 1024

_NT = (((1,), (1,)), ((), ()))
_TN = (((0,), (0,)), ((), ()))


def _params(*sem):
    return pltpu.CompilerParams(dimension_semantics=sem, vmem_limit_bytes=VMEM_LIMIT)


def _ln(x, g, b):
    mu = jnp.mean(x, axis=-1, keepdims=True)
    xc = x - mu
    var = jnp.mean(xc * xc, axis=-1, keepdims=True)
    return xc * lax.rsqrt(var + LN_EPS) * g + b


def _bf(x):
    return x.astype(jnp.bfloat16)


def _dot(a, b):
    return jnp.dot(a, b, preferred_element_type=jnp.float32)


def _dot_nt(a, b):
    return lax.dot_general(a, b, _NT, preferred_element_type=jnp.float32)


def _dot_tn(a, b):
    return lax.dot_general(a, b, _TN, preferred_element_type=jnp.float32)


def _rope_kernel(pos_ref, inv_ref, cos_ref, sin_ref, col_ref):
    quarter = LANES // 4
    grp = lax.broadcasted_iota(jnp.int32, (1, LANES), 1) // quarter
    for r in range(pos_ref.shape[0]):
        col_ref[...] = jnp.broadcast_to(pos_ref[r:r + 1, :], (LANES, LANES)).T
        p4 = col_ref[pl.ds(0, quarter, stride=4), :]
        for j in range(1, 4):
            p4 = jnp.where(grp == j, col_ref[pl.ds(j, quarter, stride=4), :], p4)
        ang4 = p4 * inv_ref[...]
        for tab_ref, fn in ((cos_ref, jnp.cos), (sin_ref, jnp.sin)):
            t4 = fn(ang4)
            rolls = [t4] + [pltpu.roll(t4, quarter * k, 1) for k in range(1, 4)]
            for j in range(4):
                y = rolls[(-j) % 4]
                for g in range(1, 4):
                    y = jnp.where(grp == g, rolls[(g - j) % 4], y)
                tab_ref[pl.ds(r * LANES + j, quarter, stride=4), :] = y


def _rope_tables(pos_f32):
    T = pos_f32.shape[0] * LANES
    rows = min(pos_f32.shape[0], 16)
    inv = ROPE_THETA ** (-np.arange(0, HEAD_DIM, 2, dtype=np.float32) / HEAD_DIM)
    inv128 = jnp.asarray(np.tile(inv, LANES // (HEAD_DIM // 2))[None, :], jnp.float32)
    return pl.pallas_call(
        _rope_kernel,
        out_shape=(jax.ShapeDtypeStruct((T, LANES), jnp.float32),) * 2,
        grid=(pos_f32.shape[0] // rows,),
        in_specs=[pl.BlockSpec((rows, LANES), lambda i: (i, 0)),
                  pl.BlockSpec((1, LANES), lambda i: (0, 0))],
        out_specs=(pl.BlockSpec((rows * LANES, LANES), lambda i: (i, 0)),) * 2,
        scratch_shapes=[pltpu.VMEM((LANES, LANES), jnp.float32)],
        compiler_params=_params("parallel"),
        name="rope_tables",
    )(pos_f32, inv128)


def _inproj_kernel(apply_ln, x_ref, g_ref, b_ref, cos_ref, sin_ref, win_ref, wq_ref, wqp_ref,
                   wi_ref, wip_ref, *out_refs):
    if apply_ln:
        xn_ref, qt_ref, iqt_ref, iwt_ref, kr_ref, ikr_ref, vat_ref, zb_ref, zc_ref = out_refs
        x = _ln(x_ref[...], g_ref[...], b_ref[...])
        xn_ref[...] = x
    else:
        qt_ref, iqt_ref, iwt_ref, kr_ref, ikr_ref, vat_ref, zb_ref, zc_ref = out_refs
        x = x_ref[...]
    z = _dot(_bf(x), win_ref[...])
    cos = cos_ref[...]
    sin = sin_ref[...]
    cq = _bf(z[:, COL_CQ:COL_CQ + Q_RANK])
    cos3 = jnp.concatenate([cos] * 3, axis=1)
    sin3 = jnp.concatenate([sin] * 3, axis=1)
    qt_ref[...] = _bf(((_dot(cq, wq_ref[...]) * cos3 + _dot(cq, wqp_ref[...]) * sin3) * LOG2E).T)
    cos4 = jnp.concatenate([cos] * 4, axis=1)
    sin4 = jnp.concatenate([sin] * 4, axis=1)
    iqt_ref[...] = _bf((_dot(cq, wi_ref[...]) * cos4 + _dot(cq, wip_ref[...]) * sin4).T)
    kik = z[:, COL_KIK:COL_KIK + LANES] * cos + z[:, COL_KIK_P:COL_KIK_P + LANES] * sin
    kr_ref[...] = _bf(kik[:, :HEAD_DIM])
    ikr_ref[...] = _bf(kik[:, HEAD_DIM:])
    viw_t = z[:, COL_VIW:COL_VIW + LANES].T
    vat_ref[...] = _bf(viw_t[:HEAD_DIM])
    iwt_ref[...] = viw_t[HEAD_DIM:HEAD_DIM + IDX_HEADS] * (IDX_HEADS ** -0.5)
    zb_ref[...] = z[:, COL_B:COL_B + B_COLS]
    zc_ref[...] = z[:, COL_C:COL_C + POOL_WIDTH]


def _inproj(x, g, b, cos, sin, win, wq, wqp, wi, wip, apply_ln):
    T = x.shape[0]
    tm = min(T, TM_PROJ)
    row = lambda i: (i, 0)
    fix = lambda i: (0, 0)
    bf16, f32 = jnp.bfloat16, jnp.float32
    col = lambda i: (0, i)
    tok = lambda n, dt: (jax.ShapeDtypeStruct((T, n), dt), pl.BlockSpec((tm, n), row))
    feat = lambda n, dt: (jax.ShapeDtypeStruct((n, T), dt), pl.BlockSpec((n, tm), col))
    outs = [feat(ATT_WIDTH, bf16), feat(IDX_HEADS * IDX_DIM, bf16), feat(IDX_HEADS, f32), tok(HEAD_DIM, bf16),
            tok(HEAD_DIM, bf16), feat(HEAD_DIM, bf16), tok(B_COLS, f32), tok(POOL_WIDTH, f32)]
    shapes = [o[0] for o in outs]
    specs = [o[1] for o in outs]
    if apply_ln:
        shapes = [jax.ShapeDtypeStruct((T, D_MODEL), f32)] + shapes
        specs = [pl.BlockSpec((tm, D_MODEL), row)] + specs
    return pl.pallas_call(
        functools.partial(_inproj_kernel, apply_ln),
        out_shape=tuple(shapes),
        grid=(T // tm,),
        in_specs=[pl.BlockSpec((tm, D_MODEL), row),
                  pl.BlockSpec((1, D_MODEL), fix), pl.BlockSpec((1, D_MODEL), fix),
                  pl.BlockSpec((tm, LANES), row), pl.BlockSpec((tm, LANES), row),
                  pl.BlockSpec((D_MODEL, N_IN_P), fix),
                  pl.BlockSpec((Q_RANK, ATT_WIDTH), fix), pl.BlockSpec((Q_RANK, ATT_WIDTH), fix),
                  pl.BlockSpec((Q_RANK, IDX_HEADS * IDX_DIM), fix),
                  pl.BlockSpec((Q_RANK, IDX_HEADS * IDX_DIM), fix)],
        out_specs=tuple(specs),
        compiler_params=_params("parallel"),
        name="inproj",
    )(x, g, b, cos, sin, win, wq, wqp, wi, wip)


def _attn_block(kend, search, topk, qt_ref, iqt_ref, iwt_ref, kr_ref, ikr_ref, vat_ref, g_ref, o_ref,
                key_ref, bias_ref, ot_ref, hi_ref, lo_ref):
    tq = qt_ref.shape[1]
    q_idx = pl.program_id(1) * tq + lax.broadcasted_iota(jnp.int32, (1, tq), 1)
    k_idx = lax.broadcasted_iota(jnp.int32, (kend, 1), 0)
    causal = k_idx <= q_idx

    if not search:
        bias_ref[:kend, :] = jnp.where(causal, 0.0, -jnp.inf)
    else:
        def idx_pair(j, first):
            ik = ikr_ref[0:kend, :]
            heads = [2 * j, 2 * j + 1]
            rows = [h * IDX_DIM if first else pl.multiple_of(h * IDX_DIM, IDX_DIM) for h in heads]
            lgs = [_dot(ik, iqt_ref[pl.ds(r, IDX_DIM), :]) for r in rows]
            acc = None if first else bias_ref[:kend, :]
            for h, lg in zip(heads, lgs):
                term = jnp.maximum(lg, 0.0) * iwt_ref[pl.ds(h, 1), :]
                acc = term if acc is None else acc + term
            bias_ref[:kend, :] = acc

        idx_pair(0, True)
        lax.fori_loop(1, IDX_HEADS // 2, lambda j, c: (idx_pair(j, False), c)[1], 0)
        score = bias_ref[:kend, :] + 0.0
        bits = pltpu.bitcast(score, jnp.int32)
        key = bits ^ ((bits >> 31) & 0x7FFFFFFF)
        key_ref[:kend, :] = jnp.where(causal, key, INT_MIN)

        i16 = jnp.int16
        pack = 16
        one_b = jnp.ones((pack, tq), jnp.bfloat16)
        zero_b = jnp.zeros((pack, tq), jnp.bfloat16)

        def count16(ref, cand):
            c16 = jnp.broadcast_to(cand, (pack, tq)).astype(i16)
            accs = [zero_b] * 4
            for n, i in enumerate(range(0, kend, pack)):
                accs[n % 4] = accs[n % 4] + jnp.where(ref[i:i + pack, :] >= c16, one_b, zero_b)
            acc = (accs[0] + accs[1]) + (accs[2] + accs[3])
            return jnp.sum(acc.astype(jnp.float32), axis=0, keepdims=True)

        def kth_largest16(ref, k):
            t = jnp.where(count16(ref, jnp.zeros((1, tq), jnp.int32)) >= k, 0, -32768).astype(jnp.int32)

            def bit_step(i, t):
                cand = t | (jnp.int32(1) << (14 - i))
                return jnp.where(count16(ref, cand) >= k, cand, t)

            return lax.fori_loop(0, 15, bit_step, t)

        keyv = key_ref[:kend, :]
        hi_ref[:kend, :] = (keyv >> 16).astype(i16)
        tau_hi = kth_largest16(hi_ref, jnp.full((1, tq), float(topk), jnp.float32))
        above = jnp.where(tau_hi < 32767, count16(hi_ref, jnp.minimum(tau_hi + 1, 32767)), 0.0)
        lo = ((keyv & 0xFFFF) - 32768).astype(i16)
        same_hi = hi_ref[:kend, :] == jnp.broadcast_to(tau_hi, (kend, tq)).astype(i16)
        lo_ref[:kend, :] = jnp.where(same_hi, lo, -32768)
        tau_lo = kth_largest16(lo_ref, topk - above)
        tau = (tau_hi << 16) | (tau_lo + 32768)
        tau = jnp.maximum(tau, INT_MIN + 1)
        ge = key_ref[:kend, :] >= tau
        n_ge = jnp.sum(jnp.where(ge, 1.0, 0.0), axis=0, keepdims=True)
        bias_ref[:kend, :] = jnp.where(ge, 0.0, -jnp.inf)

        @pl.when(jnp.max(n_ge) > topk)
        def _():
            need = topk - jnp.sum(jnp.where(key_ref[:kend, :] > tau, 1.0, 0.0), axis=0, keepdims=True)
            ri = lax.broadcasted_iota(jnp.int32, (LANES, LANES), 0)
            ci = lax.broadcasted_iota(jnp.int32, (LANES, LANES), 1)
            lower = _bf(jnp.where(ri >= ci, 1.0, 0.0))
            before = jnp.zeros((1, tq), jnp.float32)
            for c in range(0, kend, LANES):
                keyc = key_ref[c:c + LANES, :]
                eq = keyc == tau
                rank = _dot(lower, _bf(jnp.where(eq, 1.0, 0.0))) + before
                keep = (keyc > tau) | (eq & (rank <= need))
                bias_ref[c:c + LANES, :] = jnp.where(keep, 0.0, -jnp.inf)
                before = rank[LANES - 1:LANES, :]

    kc = min(kend, ATT_KEY_CHUNK)
    chunks = range(0, kend, kc)

    def att_pair(j, carry):
        heads = [ATT_GROUP * j + i for i in range(ATT_GROUP)]
        qs = [qt_ref[pl.ds(pl.multiple_of(h * HEAD_DIM, HEAD_DIM), HEAD_DIM), :] for h in heads]
        s_h = [[_dot(kr_ref[c:c + kc, :], q_) + bias_ref[c:c + kc, :] for c in chunks] for q_ in qs]
        for h, s_c in zip(heads, s_h):
            m = jnp.max(functools.reduce(jnp.maximum, s_c), axis=0, keepdims=True)
            p_c = [jnp.exp2(s - m) for s in s_c]
            l = jnp.sum(sum(p_c), axis=0, keepdims=True)
            acc = sum(_dot(vat_ref[:, c:c + kc], _bf(p)) for p, c in zip(p_c, chunks))
            ot_ref[pl.ds(pl.multiple_of(h * HEAD_DIM, HEAD_DIM), HEAD_DIM), :] = acc / l
        return carry

    lax.fori_loop(0, ATT_HEADS // ATT_GROUP, att_pair, 0)
    ot = ot_ref[...]
    ot = ot * lax.rsqrt(jnp.mean(ot * ot, axis=0, keepdims=True) + RMS_EPS) * g_ref[...]
    o_ref[...] = _bf(ot.T)


def _key_extents(S, tq, topk):
    out = []
    for v in range(S // tq):
        need = (v + 1) * tq
        if need <= topk:
            out.append((need, False))
        else:
            out.append((min(S, -(-need // (2 * tq)) * 2 * tq), True))
    return out


def _attn_kernel(S, *refs):
    tq = refs[0].shape[1]
    topk = min(TOPK_MAX, S // 4)
    qi = pl.program_id(1)
    extents = _key_extents(S, tq, topk)
    for ext in sorted(set(extents)):
        blocks = [v for v, e in enumerate(extents) if e == ext]
        pl.when((qi >= blocks[0]) & (qi <= blocks[-1]))(functools.partial(_attn_block, *ext, topk, *refs))


def _attention(B, S, qt, iqt, iwt, kr, ikr, vat, g):
    tq = min(S, TQ)
    nq = S // tq
    qcol = lambda b, i: (0, b * nq + i)
    seq = lambda b, i: (b, 0)
    return pl.pallas_call(
        functools.partial(_attn_kernel, S),
        out_shape=jax.ShapeDtypeStruct((B * S, ATT_WIDTH), jnp.bfloat16),
        grid=(B, nq),
        in_specs=[pl.BlockSpec((ATT_WIDTH, tq), qcol),
                  pl.BlockSpec((IDX_HEADS * IDX_DIM, tq), qcol),
                  pl.BlockSpec((IDX_HEADS, tq), qcol),
                  pl.BlockSpec((S, HEAD_DIM), seq), pl.BlockSpec((S, HEAD_DIM), seq),
                  pl.BlockSpec((HEAD_DIM, S), lambda b, i: (0, b)),
                  pl.BlockSpec((ATT_WIDTH, 1), lambda b, i: (0, 0))],
        out_specs=pl.BlockSpec((tq, ATT_WIDTH), lambda b, i: (b * nq + i, 0)),
        scratch_shapes=[pltpu.VMEM((S, tq), jnp.int32), pltpu.VMEM((S, tq), jnp.float32),
                        pltpu.VMEM((ATT_WIDTH, tq), jnp.float32),
                        pltpu.VMEM((S, tq), jnp.int16), pltpu.VMEM((S, tq), jnp.int16)],
        compiler_params=_params("parallel", "parallel"),
        name="dsa_attention",
    )(qt, iqt, iwt, kr, ikr, vat, g)


def _split3(x):
    hi = _bf(x)
    r1 = x - hi.astype(jnp.float32)
    mid = _bf(r1)
    lo = _bf(r1 - mid.astype(jnp.float32))
    return hi, mid, lo


def _rwkv_kernel(zb_ref, mu_ref, w0_ref, w2_ref, a0_ref, a2_ref, g2_ref, kk_ref, ka_ref, rk_ref,
                 lg_ref, lb_ref, o_ref, prev_ref, state_ref):
    G, C, _ = zb_ref.shape
    W = RWKV_WIDTH
    R = G * C
    H = RWKV_HEADS

    @pl.when(pl.program_id(1) == 0)
    def _():
        prev_ref[...] = jnp.zeros_like(prev_ref)
        state_ref[...] = jnp.zeros_like(state_ref)

    x = zb_ref[...].reshape(R, B_COLS)
    rowi = lax.broadcasted_iota(jnp.int32, (R, 1), 0)
    shifted = pltpu.roll(x, 1, 0)
    for gi in range(G):
        shifted = jnp.where(rowi == gi * C, prev_ref[8 * gi:8 * gi + 1, :], shifted)
        prev_ref[8 * gi:8 * gi + 1, :] = x[gi * C + C - 1:gi * C + C, :]
    xm = x + (shifted - x) * mu_ref[...]
    r = xm[:, 0:W]
    k = xm[:, W:2 * W]
    v = xm[:, 2 * W:3 * W]
    wl = xm[:, 3 * W:3 * W + DECAY_RANK]
    al = xm[:, 3 * W + DECAY_RANK:3 * W + DECAY_RANK + ICLR_RANK]
    gl = xm[:, 3 * W + DECAY_RANK + ICLR_RANK:]

    wpre = w0_ref[...] + _dot(_bf(jnp.tanh(wl)), w2_ref[...])
    nw = -wpre
    w = -(jnp.maximum(nw, 0.0) + jnp.log(1.0 + jnp.exp(-jnp.abs(nw)))) - 0.5
    logd = -jnp.exp(w)
    a = jax.nn.sigmoid(a0_ref[...] + _dot(_bf(al), a2_ref[...]))
    g = _dot(_bf(jax.nn.sigmoid(gl)), g2_ref[...])

    rr = lax.broadcasted_iota(jnp.int32, (R, R), 0)
    cc = lax.broadcasted_iota(jnp.int32, (R, R), 1)
    tri = _bf(jnp.where((rr >= cc) & (rr // C == cc // C), 1.0, 0.0))
    hi, mid, lo = _split3(logd)
    cw = _dot(tri, hi) + _dot(tri, mid) + _dot(tri, lo)
    e_in = jnp.exp(cw)
    e_ex = jnp.exp(cw - logd)
    e_inv = jnp.exp(-cw)

    kk = k * kk_ref[...]
    k2 = k * (1.0 + (a - 1.0) * ka_ref[...])
    rk2 = r * k2 * rk_ref[...]
    ri = lax.broadcasted_iota(jnp.int32, (C, C), 0)
    ci = lax.broadcasted_iota(jnp.int32, (C, C), 1)
    strict = ri > ci
    ri2 = lax.broadcasted_iota(jnp.int32, (C, 2 * C), 0)
    ci2 = lax.broadcasted_iota(jnp.int32, (C, 2 * C), 1)
    incl2 = ri2 >= jnp.where(ci2 >= C, ci2 - C, ci2)
    eye = jnp.where(ri == ci, 1.0, 0.0)

    units = [(gi, h) for gi in range(G) for h in range(H)]

    def cut(arr, u):
        gi, h = u
        return arr[gi * C:(gi + 1) * C, h * HEAD_DIM:(h + 1) * HEAD_DIM]

    hd = (lax.broadcasted_iota(jnp.int32, (W, W), 0) // HEAD_DIM
          == lax.broadcasted_iota(jnp.int32, (W, W), 1) // HEAD_DIM)
    head_ones = _bf(jnp.where(hd, 1.0, 0.0))

    def head_sum(z):
        return _dot(_bf(z), head_ones)

    kkn_w = kk * lax.rsqrt(jnp.maximum(head_sum(kk * kk), 1e-24))
    at_w = _bf(-kkn_w * e_ex)
    rt_w = _bf(r * e_in)
    bt_w = _bf(kkn_w * a * e_inv)
    kt_w = _bf(k2 * e_inv)
    v_w = _bf(v)
    at = [cut(at_w, u) for u in units]
    rt = [cut(rt_w, u) for u in units]
    v16 = [cut(v_w, u) for u in units]
    bk16 = [jnp.concatenate([cut(bt_w, u), cut(kt_w, u)], axis=0) for u in units]
    g_a = [_dot_nt(a_, m_) for a_, m_ in zip(at, bk16)]
    l_ab = [jnp.where(strict, g_[:, :C], 0.0) for g_ in g_a]
    a_ak = [_bf(jnp.where(strict, g_[:, C:], 0.0)) for g_ in g_a]
    tinv = [eye + l_ for l_ in l_ab]
    pw = l_ab
    n = 2
    while n < C:
        pw = [_dot(_bf(p_), _bf(p_)) for p_ in pw]
        tinv = [t_ + _dot(_bf(t_), _bf(p_)) for t_, p_ in zip(tinv, pw)]
        n *= 2
    s0 = [state_ref[i] for i in range(len(units))]
    s016 = [_bf(s_) for s_ in s0]
    rhs = [_dot_nt(a_, s_) + _dot(m_, v_) for a_, s_, m_, v_ in zip(at, s016, a_ak, v16)]
    uu = [_dot(_bf(t_), _bf(x_)) for t_, x_ in zip(tinv, rhs)]
    uv16 = [jnp.concatenate([_bf(u_), v_], axis=0) for u_, v_ in zip(uu, v16)]
    g_r = [_bf(jnp.where(incl2, _dot_nt(r_, m_), 0.0)) for r_, m_ in zip(rt, bk16)]
    y = [_dot(m_, x_) + _dot_nt(r_, s_) for m_, x_, r_, s_ in zip(g_r, uv16, rt, s016)]
    for i, u in enumerate(units):
        state_ref[i] = (s0[i] + _dot_tn(uv16[i], bk16[i])) * cut(e_in, u)[C - 1:C, :]
    yw = jnp.concatenate([jnp.concatenate(y[gi * H:(gi + 1) * H], axis=1) for gi in range(G)], axis=0)
    inv_d = 1.0 / HEAD_DIM
    y_hi = _bf(yw)
    yc = yw - (_dot(y_hi, head_ones) + head_sum(yw - y_hi.astype(jnp.float32))) * inv_d
    yn = yc * lax.rsqrt(head_sum(yc * yc) * inv_d + GN_EPS) * lg_ref[...] + lb_ref[...]
    out = _bf((yn + head_sum(rk2) * v) * g)
    for gi in range(G):
        o_ref[gi] = out[gi * C:(gi + 1) * C, :]


def _rwkv(B, S, zb, mu, w0, w2, a0, a2, g2, k_k, k_a, r_k, lnx_g, lnx_b):
    C = min(S, CHUNK)
    G = min(B, RWKV_ROWS)
    blk = lambda b, c: (b, c, 0)
    fix = lambda b, c: (0, 0)
    vec = lambda n: pl.BlockSpec((1, n), fix)
    return pl.pallas_call(
        _rwkv_kernel,
        out_shape=jax.ShapeDtypeStruct((B, S, RWKV_WIDTH), jnp.bfloat16),
        grid=(B // G, S // C),
        in_specs=[pl.BlockSpec((G, C, B_COLS), blk), vec(B_COLS), vec(RWKV_WIDTH),
                  pl.BlockSpec((DECAY_RANK, RWKV_WIDTH), fix), vec(RWKV_WIDTH),
                  pl.BlockSpec((ICLR_RANK, RWKV_WIDTH), fix),
                  pl.BlockSpec((GATE_RANK, RWKV_WIDTH), fix),
                  vec(RWKV_WIDTH), vec(RWKV_WIDTH), vec(RWKV_WIDTH), vec(RWKV_WIDTH),
                  vec(RWKV_WIDTH)],
        out_specs=pl.BlockSpec((G, C, RWKV_WIDTH), blk),
        scratch_shapes=[pltpu.VMEM((8 * G, B_COLS), jnp.float32),
                        pltpu.VMEM((G * RWKV_HEADS, HEAD_DIM, HEAD_DIM), jnp.float32)],
        compiler_params=_params("parallel", "arbitrary"),
        name="rwkv7",
    )(zb.reshape(B, S, B_COLS), mu, w0, w2, a0, a2, g2, k_k, k_a, r_k, lnx_g, lnx_b
      ).reshape(B * S, RWKV_WIDTH)


def _pool_kernel(zc_ref, w_ref, sc_ref, o_ref):
    S = zc_ref.shape[0]
    x = zc_ref[...]
    row = lax.broadcasted_iota(jnp.int32, (S, 1), 0)
    lane_grp = lax.broadcasted_iota(jnp.int32, (1, POOL_WIDTH), 1) // POOL_GROUP

    def lag(y, n):
        return jnp.where(row >= n, pltpu.roll(y, n, 0), 0.0)

    w2 = x + lag(x, 1)
    w4 = w2 + lag(w2, 2)
    w8 = w4 + lag(w4, 4)
    w16 = w8 + lag(w8, 8)
    wsum = jnp.where(lane_grp == 0, w2, jnp.where(lane_grp == 1, w4, jnp.where(lane_grp == 2, w8, w16)))
    win = jnp.where(lane_grp == 0, 2, jnp.where(lane_grp == 1, 4, jnp.where(lane_grp == 2, 8, 16)))
    count = jnp.minimum(row + 1, win).astype(jnp.float32)
    pooled = wsum / count - x
    o_ref[...] = _bf(_dot(_bf(pooled), w_ref[...]) * sc_ref[...])


def _pool(B, S, zc, w_bd, scale):
    return pl.pallas_call(
        _pool_kernel,
        out_shape=jax.ShapeDtypeStruct((B * S, POOL_WIDTH), jnp.bfloat16),
        grid=(B,),
        in_specs=[pl.BlockSpec((S, POOL_WIDTH), lambda b: (b, 0)),
                  pl.BlockSpec((POOL_WIDTH, POOL_WIDTH), lambda b: (0, 0)),
                  pl.BlockSpec((1, POOL_WIDTH), lambda b: (0, 0))],
        out_specs=pl.BlockSpec((S, POOL_WIDTH), lambda b: (b, 0)),
        compiler_params=_params("parallel"),
        name="pool",
    )(zc, w_bd, scale)


def _outmlp_kernel(alpha, oa_ref, ob_ref, oc_ref, x_ref, p_ref, wa_ref, wb_ref, wc_ref, g1_ref, b1_ref,
                   w1_ref, w2_ref, pg_ref, pgb_ref, pp_ref, g2_ref, b2_ref, o_ref):
    tm = x_ref.shape[0]
    halves = [slice(0, tm // 2), slice(tm // 2, tm)]
    mix = [_dot(oa_ref[h, :], wa_ref[...]) + _dot(ob_ref[h, :], wb_ref[...]) + _dot(oc_ref[h, :], wc_ref[...])
           for h in halves]
    x1 = [_ln(alpha * x_ref[h, :] + m, g1_ref[...], b1_ref[...]) for h, m in zip(halves, mix)]
    xb = [_bf(v) for v in x1]
    gate = [jax.nn.sigmoid(_dot(b_, pg_ref[...]) + pgb_ref[...]) for b_ in xb]
    pe = [_dot(_bf(p_ref[0, h, :]), pp_ref[...]) for h in halves]
    acc = [alpha * v + g_ * e_ for v, g_, e_ in zip(x1, gate, pe)]
    xb_all = jnp.concatenate(xb, axis=0)
    nk = D_FF // TF_MLP
    for kf in range(nk):
        cols = slice(kf * TF_MLP, (kf + 1) * TF_MLP)
        hk = jnp.maximum(_dot(xb_all, w1_ref[:, cols]), 0.0)
        h2 = _bf(hk * hk)
        if kf < nk - 1:
            part = _dot(h2, w2_ref[cols, :])
            acc = [a + part[h, :] for a, h in zip(acc, halves)]
        else:
            acc = [a + _dot(h2[h, :], w2_ref[cols, :]) for a, h in zip(acc, halves)]
    for h, a in zip(halves, acc):
        o_ref[h, :] = _ln(a, g2_ref[...], b2_ref[...])


def _outmlp(alpha, layer, oa, ob, oc, x, p, wa, wb, wc, g1, b1, w1, w2, pg, pgb, pp, g2, b2):
    T = x.shape[0]
    tm = min(T, TM_MLP)
    row = lambda i: (i, 0)
    fix = lambda i: (0, 0)
    once = pl.Buffered(1)
    wspec = lambda r, c: pl.BlockSpec((r, c), fix, pipeline_mode=once)
    vec = pl.BlockSpec((1, D_MODEL), fix)
    return pl.pallas_call(
        functools.partial(_outmlp_kernel, alpha),
        out_shape=jax.ShapeDtypeStruct((T, D_MODEL), jnp.float32),
        grid=(T // tm,),
        in_specs=[pl.BlockSpec((tm, ATT_WIDTH), row), pl.BlockSpec((tm, RWKV_WIDTH), row),
                  pl.BlockSpec((tm, POOL_WIDTH), row), pl.BlockSpec((tm, D_MODEL), row),
                  pl.BlockSpec((1, tm, PE_DIM), lambda i: (layer, i, 0)),
                  wspec(ATT_WIDTH, D_MODEL), wspec(RWKV_WIDTH, D_MODEL), wspec(POOL_WIDTH, D_MODEL), vec, vec,
                  wspec(D_MODEL, D_FF), wspec(D_FF, D_MODEL), wspec(D_MODEL, D_MODEL), vec,
                  wspec(PE_DIM, D_MODEL), vec, vec],
        out_specs=pl.BlockSpec((tm, D_MODEL), row),
        compiler_params=_params("parallel"),
        name="outproj_mlp",
    )(oa, ob, oc, x, p, wa, wb, wc, g1, b1, w1, w2, pg, pgb, pp, g2, b2)


def _rot_partner(w):
    half = HEAD_DIM // 2
    return jnp.concatenate([-w[..., half:], w[..., :half]], axis=-1)


def _per_head_partner(w, heads):
    r = w.reshape(w.shape[0], heads, HEAD_DIM)
    return _rot_partner(r).reshape(w.shape)


def _arrange_w_in(w):
    cq = w[:, 0:256]
    ka = w[:, 256:320]
    va = w[:, 320:384]
    ik = w[:, 384:448]
    iw = w[:, 448:456]
    rest = w[:, 456:]
    pad = jnp.zeros((w.shape[0], LANES - HEAD_DIM - IDX_HEADS), w.dtype)
    return jnp.concatenate([cq, ka, ik, _rot_partner(ka), _rot_partner(ik), va, iw, pad, rest], axis=1)


def kernel(x, p, positions, ln_emb_g, ln_emb_b, w_in, w_uq, w_uqi, attn_norm_g, rwkv_mu, rwkv_w0, rwkv_w2, rwkv_a0, rwkv_a2, rwkv_g2, rwkv_k_k, rwkv_k_a, rwkv_r_k, rwkv_lnx_g, rwkv_lnx_b, pool_w, pool_scale, w_out, ln1_g, ln1_b, mlp_w1, mlp_w2, pe_proj, pe_gate, pe_gate_b, ln2_g, ln2_b):
    B, S, _ = x.shape
    T = B * S
    depth = w_in.shape[0]
    alpha = (2 * depth) ** 0.25
    bf = lambda a: a.astype(jnp.bfloat16)
    rowv = lambda a: a.reshape(1, -1)

    cos, sin = _rope_tables(positions.astype(jnp.float32).reshape(T // LANES, LANES))
    xs = x.reshape(T, D_MODEL)
    for i in range(depth):
        win = bf(_arrange_w_in(w_in[i]))
        wq = w_uq[i] * (HEAD_DIM ** -0.5)
        wi = w_uqi[i] * (IDX_DIM ** -0.5)
        res = _inproj(xs, rowv(ln_emb_g), rowv(ln_emb_b), cos, sin, win,
                      bf(wq), bf(_per_head_partner(wq, ATT_HEADS)),
                      bf(wi), bf(_per_head_partner(wi, IDX_HEADS)), apply_ln=(i == 0))
        if i == 0:
            xs, res = res[0], res[1:]
        qt, iqt, iwt, kr, ikr, vat, zb, zc = res
        o_a = _attention(B, S, qt, iqt, iwt, kr, ikr, vat, attn_norm_g[i].reshape(-1, 1))
        o_b = _rwkv(B, S, zb, rowv(rwkv_mu[i]), rowv(rwkv_w0[i]), bf(rwkv_w2[i]), rowv(rwkv_a0[i]),
                    bf(rwkv_a2[i]), bf(rwkv_g2[i]), rowv(rwkv_k_k[i]), rowv(rwkv_k_a[i]),
                    rowv(rwkv_r_k[i]), rowv(rwkv_lnx_g[i]), rowv(rwkv_lnx_b[i]))
        w_bd = jax.scipy.linalg.block_diag(*[pool_w[i, gi] for gi in range(len(POOL_WINDOWS))])
        o_c = _pool(B, S, zc, bf(w_bd), rowv(pool_scale[i]))
        wo = bf(w_out[i])
        xs = _outmlp(alpha, i, o_a, o_b, o_c, xs, p.reshape(depth, T, PE_DIM), wo[:ATT_WIDTH],
                     wo[ATT_WIDTH:ATT_WIDTH + RWKV_WIDTH], wo[ATT_WIDTH + RWKV_WIDTH:], rowv(ln1_g[i]),
                     rowv(ln1_b[i]), bf(mlp_w1[i]), bf(mlp_w2[i]), bf(pe_gate[i]), rowv(pe_gate_b[i]),
                     bf(pe_proj[i]), rowv(ln2_g[i]), rowv(ln2_b[i]))
    return xs.reshape(B, S, D_MODEL)
```

```python
import functools

import numpy as np
import jax
import jax.numpy as jnp
from jax import lax
from jax.experimental import pallas as pl
from jax.experimental.pallas import tpu as pltpu

D_MODEL = 1024
PE_DIM = 256
HEAD_DIM = 64
ROPE_THETA = 10000.0
ATT_WIDTH = 384
ATT_HEADS = 6
Q_RANK = 256
IDX_HEADS = 8
IDX_DIM = 64
TOPK_MAX = 256
RWKV_WIDTH = 384
RWKV_HEADS = 6
DECAY_RANK = 64
ICLR_RANK = 64
GATE_RANK = 128
GN_EPS = 64e-5
POOL_WINDOWS = (2, 4, 8, 16)
POOL_WIDTH = 256
POOL_GROUP = 64
B_COLS = 3 * RWKV_WIDTH + DECAY_RANK + ICLR_RANK + GATE_RANK
D_FF = 4 * D_MODEL
LN_EPS = 1e-5
RMS_EPS = 1e-6

COL_CQ = 0
COL_KIK = 256
COL_KIK_P = 384
COL_VIW = 512
COL_B = 640
COL_C = COL_B + B_COLS
N_IN_P = COL_C + POOL_WIDTH

LANES = 128
INT_MIN = -2 ** 31
LOG2E = 1.4426950408889634
VMEM_LIMIT = 56 * 1024 * 1024

TM_PROJ = 512
TQ = 256
ATT_KEY_CHUNK = 512
ATT_GROUP = 3
CHUNK = 64
RWKV_ROWS = 4
TM_MLP = 512
TF_MLP = 1024

_NT = (((1,), (1,)), ((), ()))
_TN = (((0,), (0,)), ((), ()))


def _params(*sem):
    return pltpu.CompilerParams(dimension_semantics=sem, vmem_limit_bytes=VMEM_LIMIT)


def _ln(x, g, b):
    mu = jnp.mean(x, axis=-1, keepdims=True)
    xc = x - mu
    var = jnp.mean(xc * xc, axis=-1, keepdims=True)
    return xc * lax.rsqrt(var + LN_EPS) * g + b


def _bf(x):
    return x.astype(jnp.bfloat16)


def _dot(a, b):
    return jnp.dot(a, b, preferred_element_type=jnp.float32)


def _dot_nt(a, b):
    return lax.dot_general(a, b, _NT, preferred_element_type=jnp.float32)


def _dot_tn(a, b):
    return lax.dot_general(a, b, _TN, preferred_element_type=jnp.float32)


def _rope_kernel(pos_ref, inv_ref, cos_ref, sin_ref, col_ref):
    quarter = LANES // 4
    grp = lax.broadcasted_iota(jnp.int32, (1, LANES), 1) // quarter
    for r in range(pos_ref.shape[0]):
        col_ref[...] = jnp.broadcast_to(pos_ref[r:r + 1, :], (LANES, LANES)).T
        p4 = col_ref[pl.ds(0, quarter, stride=4), :]
        for j in range(1, 4):
            p4 = jnp.where(grp == j, col_ref[pl.ds(j, quarter, stride=4), :], p4)
        ang4 = p4 * inv_ref[...]
        for tab_ref, fn in ((cos_ref, jnp.cos), (sin_ref, jnp.sin)):
            t4 = fn(ang4)
            rolls = [t4] + [pltpu.roll(t4, quarter * k, 1) for k in range(1, 4)]
            for j in range(4):
                y = rolls[(-j) % 4]
                for g in range(1, 4):
                    y = jnp.where(grp == g, rolls[(g - j) % 4], y)
                tab_ref[pl.ds(r * LANES + j, quarter, stride=4), :] = y


def _rope_tables(pos_f32):
    T = pos_f32.shape[0] * LANES
    rows = min(pos_f32.shape[0], 16)
    inv = ROPE_THETA ** (-np.arange(0, HEAD_DIM, 2, dtype=np.float32) / HEAD_DIM)
    inv128 = jnp.asarray(np.tile(inv, LANES // (HEAD_DIM // 2))[None, :], jnp.float32)
    return pl.pallas_call(
        _rope_kernel,
        out_shape=(jax.ShapeDtypeStruct((T, LANES), jnp.float32),) * 2,
        grid=(pos_f32.shape[0] // rows,),
        in_specs=[pl.BlockSpec((rows, LANES), lambda i: (i, 0)),
                  pl.BlockSpec((1, LANES), lambda i: (0, 0))],
        out_specs=(pl.BlockSpec((rows * LANES, LANES), lambda i: (i, 0)),) * 2,
        scratch_shapes=[pltpu.VMEM((LANES, LANES), jnp.float32)],
        compiler_params=_params("parallel"),
        name="rope_tables",
    )(pos_f32, inv128)


def _inproj_kernel(apply_ln, x_ref, g_ref, b_ref, cos_ref, sin_ref, win_ref, wq_ref, wqp_ref,
                   wi_ref, wip_ref, *out_refs):
    if apply_ln:
        xn_ref, qt_ref, iqt_ref, iwt_ref, kr_ref, ikr_ref, vat_ref, zb_ref, zc_ref = out_refs
        x = _ln(x_ref[...], g_ref[...], b_ref[...])
        xn_ref[...] = x
    else:
        qt_ref, iqt_ref, iwt_ref, kr_ref, ikr_ref, vat_ref, zb_ref, zc_ref = out_refs
        x = x_ref[...]
    z = _dot(_bf(x), win_ref[...])
    cos = cos_ref[...]
    sin = sin_ref[...]
    cq = _bf(z[:, COL_CQ:COL_CQ + Q_RANK])
    cos3 = jnp.concatenate([cos] * 3, axis=1)
    sin3 = jnp.concatenate([sin] * 3, axis=1)
    qt_ref[...] = _bf(((_dot(cq, wq_ref[...]) * cos3 + _dot(cq, wqp_ref[...]) * sin3) * LOG2E).T)
    cos4 = jnp.concatenate([cos] * 4, axis=1)
    sin4 = jnp.concatenate([sin] * 4, axis=1)
    iqt_ref[...] = _bf((_dot(cq, wi_ref[...]) * cos4 + _dot(cq, wip_ref[...]) * sin4).T)
    kik = z[:, COL_KIK:COL_KIK + LANES] * cos + z[:, COL_KIK_P:COL_KIK_P + LANES] * sin
    kr_ref[...] = _bf(kik[:, :HEAD_DIM])
    ikr_ref[...] = _bf(kik[:, HEAD_DIM:])
    viw_t = z[:, COL_VIW:COL_VIW + LANES].T
    vat_ref[...] = _bf(viw_t[:HEAD_DIM])
    iwt_ref[...] = viw_t[HEAD_DIM:HEAD_DIM + IDX_HEADS] * (IDX_HEADS ** -0.5)
    zb_ref[...] = z[:, COL_B:COL_B + B_COLS]
    zc_ref[...] = z[:, COL_C:COL_C + POOL_WIDTH]


def _inproj(x, g, b, cos, sin, win, wq, wqp, wi, wip, apply_ln):
    T = x.shape[0]
    tm = min(T, TM_PROJ)
    row = lambda i: (i, 0)
    fix = lambda i: (0, 0)
    bf16, f32 = jnp.bfloat16, jnp.float32
    col = lambda i: (0, i)
    tok = lambda n, dt: (jax.ShapeDtypeStruct((T, n), dt), pl.BlockSpec((tm, n), row))
    feat = lambda n, dt: (jax.ShapeDtypeStruct((n, T), dt), pl.BlockSpec((n, tm), col))
    outs = [feat(ATT_WIDTH, bf16), feat(IDX_HEADS * IDX_DIM, bf16), feat(IDX_HEADS, f32), tok(HEAD_DIM, bf16),
            tok(HEAD_DIM, bf16), feat(HEAD_DIM, bf16), tok(B_COLS, f32), tok(POOL_WIDTH, f32)]
    shapes = [o[0] for o in outs]
    specs = [o[1] for o in outs]
    if apply_ln:
        shapes = [jax.ShapeDtypeStruct((T, D_MODEL), f32)] + shapes
        specs = [pl.BlockSpec((tm, D_MODEL), row)] + specs
    return pl.pallas_call(
        functools.partial(_inproj_kernel, apply_ln),
        out_shape=tuple(shapes),
        grid=(T // tm,),
        in_specs=[pl.BlockSpec((tm, D_MODEL), row),
                  pl.BlockSpec((1, D_MODEL), fix), pl.BlockSpec((1, D_MODEL), fix),
                  pl.BlockSpec((tm, LANES), row), pl.BlockSpec((tm, LANES), row),
                  pl.BlockSpec((D_MODEL, N_IN_P), fix),
                  pl.BlockSpec((Q_RANK, ATT_WIDTH), fix), pl.BlockSpec((Q_RANK, ATT_WIDTH), fix),
                  pl.BlockSpec((Q_RANK, IDX_HEADS * IDX_DIM), fix),
                  pl.BlockSpec((Q_RANK, IDX_HEADS * IDX_DIM), fix)],
        out_specs=tuple(specs),
        compiler_params=_params("parallel"),
        name="inproj",
    )(x, g, b, cos, sin, win, wq, wqp, wi, wip)


def _attn_block(kend, search, topk, qt_ref, iqt_ref, iwt_ref, kr_ref, ikr_ref, vat_ref, g_ref, o_ref,
                key_ref, bias_ref, ot_ref, hi_ref, lo_ref):
    tq = qt_ref.shape[1]
    q_idx = pl.program_id(1) * tq + lax.broadcasted_iota(jnp.int32, (1, tq), 1)
    k_idx = lax.broadcasted_iota(jnp.int32, (kend, 1), 0)
    causal = k_idx <= q_idx

    if not search:
        bias_ref[:kend, :] = jnp.where(causal, 0.0, -jnp.inf)
    else:
        def idx_pair(j, first):
            ik = ikr_ref[0:kend, :]
            heads = [2 * j, 2 * j + 1]
            rows = [h * IDX_DIM if first else pl.multiple_of(h * IDX_DIM, IDX_DIM) for h in heads]
            lgs = [_dot(ik, iqt_ref[pl.ds(r, IDX_DIM), :]) for r in rows]
            acc = None if first else bias_ref[:kend, :]
            for h, lg in zip(heads, lgs):
                term = jnp.maximum(lg, 0.0) * iwt_ref[pl.ds(h, 1), :]
                acc = term if acc is None else acc + term
            bias_ref[:kend, :] = acc

        idx_pair(0, True)
        lax.fori_loop(1, IDX_HEADS // 2, lambda j, c: (idx_pair(j, False), c)[1], 0)
        score = bias_ref[:kend, :] + 0.0
        bits = pltpu.bitcast(score, jnp.int32)
        key = bits ^ ((bits >> 31) & 0x7FFFFFFF)
        key_ref[:kend, :] = jnp.where(causal, key, INT_MIN)

        i16 = jnp.int16
        pack = 16
        one_b = jnp.ones((pack, tq), jnp.bfloat16)
        zero_b = jnp.zeros((pack, tq), jnp.bfloat16)

        def count16(ref, cand):
            c16 = jnp.broadcast_to(cand, (pack, tq)).astype(i16)
            accs = [zero_b] * 4
            for n, i in enumerate(range(0, kend, pack)):
                accs[n % 4] = accs[n % 4] + jnp.where(ref[i:i + pack, :] >= c16, one_b, zero_b)
            acc = (accs[0] + accs[1]) + (accs[2] + accs[3])
            return jnp.sum(acc.astype(jnp.float32), axis=0, keepdims=True)

        def kth_largest16(ref, k):
            t = jnp.where(count16(ref, jnp.zeros((1, tq), jnp.int32)) >= k, 0, -32768).astype(jnp.int32)

            def bit_step(i, t):
                cand = t | (jnp.int32(1) << (14 - i))
                return jnp.where(count16(ref, cand) >= k, cand, t)

            return lax.fori_loop(0, 15, bit_step, t)

        keyv = key_ref[:kend, :]
        hi_ref[:kend, :] = (keyv >> 16).astype(i16)
        tau_hi = kth_largest16(hi_ref, jnp.full((1, tq), float(topk), jnp.float32))
        above = jnp.where(tau_hi < 32767, count16(hi_ref, jnp.minimum(tau_hi + 1, 32767)), 0.0)
        lo = ((keyv & 0xFFFF) - 32768).astype(i16)
        same_hi = hi_ref[:kend, :] == jnp.broadcast_to(tau_hi, (kend, tq)).astype(i16)
        lo_ref[:kend, :] = jnp.where(same_hi, lo, -32768)
        tau_lo = kth_largest16(lo_ref, topk - above)
        tau = (tau_hi << 16) | (tau_lo + 32768)
        tau = jnp.maximum(tau, INT_MIN + 1)
        ge = key_ref[:kend, :] >= tau
        n_ge = jnp.sum(jnp.where(ge, 1.0, 0.0), axis=0, keepdims=True)
        bias_ref[:kend, :] = jnp.where(ge, 0.0, -jnp.inf)

        @pl.when(jnp.max(n_ge) > topk)
        def _():
            need = topk - jnp.sum(jnp.where(key_ref[:kend, :] > tau, 1.0, 0.0), axis=0, keepdims=True)
            ri = lax.broadcasted_iota(jnp.int32, (LANES, LANES), 0)
            ci = lax.broadcasted_iota(jnp.int32, (LANES, LANES), 1)
            lower = _bf(jnp.where(ri >= ci, 1.0, 0.0))
            before = jnp.zeros((1, tq), jnp.float32)
            for c in range(0, kend, LANES):
                keyc = key_ref[c:c + LANES, :]
                eq = keyc == tau
                rank = _dot(lower, _bf(jnp.where(eq, 1.0, 0.0))) + before
                keep = (keyc > tau) | (eq & (rank <= need))
                bias_ref[c:c + LANES, :] = jnp.where(keep, 0.0, -jnp.inf)
                before = rank[LANES - 1:LANES, :]

    kc = min(kend, ATT_KEY_CHUNK)
    chunks = range(0, kend, kc)

    def att_pair(j, carry):
        heads = [ATT_GROUP * j + i for i in range(ATT_GROUP)]
        qs = [qt_ref[pl.ds(pl.multiple_of(h * HEAD_DIM, HEAD_DIM), HEAD_DIM), :] for h in heads]
        s_h = [[_dot(kr_ref[c:c + kc, :], q_) + bias_ref[c:c + kc, :] for c in chunks] for q_ in qs]
        for h, s_c in zip(heads, s_h):
            m = jnp.max(functools.reduce(jnp.maximum, s_c), axis=0, keepdims=True)
            p_c = [jnp.exp2(s - m) for s in s_c]
            l = jnp.sum(sum(p_c), axis=0, keepdims=True)
            acc = sum(_dot(vat_ref[:, c:c + kc], _bf(p)) for p, c in zip(p_c, chunks))
            ot_ref[pl.ds(pl.multiple_of(h * HEAD_DIM, HEAD_DIM), HEAD_DIM), :] = acc / l
        return carry

    lax.fori_loop(0, ATT_HEADS // ATT_GROUP, att_pair, 0)
    ot = ot_ref[...]
    ot = ot * lax.rsqrt(jnp.mean(ot * ot, axis=0, keepdims=True) + RMS_EPS) * g_ref[...]
    o_ref[...] = _bf(ot.T)


def _key_extents(S, tq, topk):
    out = []
    for v in range(S // tq):
        need = (v + 1) * tq
        if need <= topk:
            out.append((need, False))
        else:
            out.append((min(S, -(-need // (2 * tq)) * 2 * tq), True))
    return out


def _attn_kernel(S, *refs):
    tq = refs[0].shape[1]
    topk = min(TOPK_MAX, S // 4)
    qi = pl.program_id(1)
    extents = _key_extents(S, tq, topk)
    for ext in sorted(set(extents)):
        blocks = [v for v, e in enumerate(extents) if e == ext]
        pl.when((qi >= blocks[0]) & (qi <= blocks[-1]))(functools.partial(_attn_block, *ext, topk, *refs))


def _attention(B, S, qt, iqt, iwt, kr, ikr, vat, g):
    tq = min(S, TQ)
    nq = S // tq
    qcol = lambda b, i: (0, b * nq + i)
    seq = lambda b, i: (b, 0)
    return pl.pallas_call(
        functools.partial(_attn_kernel, S),
        out_shape=jax.ShapeDtypeStruct((B * S, ATT_WIDTH), jnp.bfloat16),
        grid=(B, nq),
        in_specs=[pl.BlockSpec((ATT_WIDTH, tq), qcol),
                  pl.BlockSpec((IDX_HEADS * IDX_DIM, tq), qcol),
                  pl.BlockSpec((IDX_HEADS, tq), qcol),
                  pl.BlockSpec((S, HEAD_DIM), seq), pl.BlockSpec((S, HEAD_DIM), seq),
                  pl.BlockSpec((HEAD_DIM, S), lambda b, i: (0, b)),
                  pl.BlockSpec((ATT_WIDTH, 1), lambda b, i: (0, 0))],
        out_specs=pl.BlockSpec((tq, ATT_WIDTH), lambda b, i: (b * nq + i, 0)),
        scratch_shapes=[pltpu.VMEM((S, tq), jnp.int32), pltpu.VMEM((S, tq), jnp.float32),
                        pltpu.VMEM((ATT_WIDTH, tq), jnp.float32),
                        pltpu.VMEM((S, tq), jnp.int16), pltpu.VMEM((S, tq), jnp.int16)],
        compiler_params=_params("parallel", "parallel"),
        name="dsa_attention",
    )(qt, iqt, iwt, kr, ikr, vat, g)


def _split3(x):
    hi = _bf(x)
    r1 = x - hi.astype(jnp.float32)
    mid = _bf(r1)
    lo = _bf(r1 - mid.astype(jnp.float32))
    return hi, mid, lo


def _rwkv_kernel(zb_ref, mu_ref, w0_ref, w2_ref, a0_ref, a2_ref, g2_ref, kk_ref, ka_ref, rk_ref,
                 lg_ref, lb_ref, o_ref, prev_ref, state_ref):
    G, C, _ = zb_ref.shape
    W = RWKV_WIDTH
    R = G * C
    H = RWKV_HEADS

    @pl.when(pl.program_id(1) == 0)
    def _():
        prev_ref[...] = jnp.zeros_like(prev_ref)
        state_ref[...] = jnp.zeros_like(state_ref)

    x = zb_ref[...].reshape(R, B_COLS)
    rowi = lax.broadcasted_iota(jnp.int32, (R, 1), 0)
    shifted = pltpu.roll(x, 1, 0)
    for gi in range(G):
        shifted = jnp.where(rowi == gi * C, prev_ref[8 * gi:8 * gi + 1, :], shifted)
        prev_ref[8 * gi:8 * gi + 1, :] = x[gi * C + C - 1:gi * C + C, :]
    xm = x + (shifted - x) * mu_ref[...]
    r = xm[:, 0:W]
    k = xm[:, W:2 * W]
    v = xm[:, 2 * W:3 * W]
    wl = xm[:, 3 * W:3 * W + DECAY_RANK]
    al = xm[:, 3 * W + DECAY_RANK:3 * W + DECAY_RANK + ICLR_RANK]
    gl = xm[:, 3 * W + DECAY_RANK + ICLR_RANK:]

    wpre = w0_ref[...] + _dot(_bf(jnp.tanh(wl)), w2_ref[...])
    nw = -wpre
    w = -(jnp.maximum(nw, 0.0) + jnp.log(1.0 + jnp.exp(-jnp.abs(nw)))) - 0.5
    logd = -jnp.exp(w)
    a = jax.nn.sigmoid(a0_ref[...] + _dot(_bf(al), a2_ref[...]))
    g = _dot(_bf(jax.nn.sigmoid(gl)), g2_ref[...])

    rr = lax.broadcasted_iota(jnp.int32, (R, R), 0)
    cc = lax.broadcasted_iota(jnp.int32, (R, R), 1)
    tri = _bf(jnp.where((rr >= cc) & (rr // C == cc // C), 1.0, 0.0))
    hi, mid, lo = _split3(logd)
    cw = _dot(tri, hi) + _dot(tri, mid) + _dot(tri, lo)
    e_in = jnp.exp(cw)
    e_ex = jnp.exp(cw - logd)
    e_inv = jnp.exp(-cw)

    kk = k * kk_ref[...]
    k2 = k * (1.0 + (a - 1.0) * ka_ref[...])
    rk2 = r * k2 * rk_ref[...]
    ri = lax.broadcasted_iota(jnp.int32, (C, C), 0)
    ci = lax.broadcasted_iota(jnp.int32, (C, C), 1)
    strict = ri > ci
    ri2 = lax.broadcasted_iota(jnp.int32, (C, 2 * C), 0)
    ci2 = lax.broadcasted_iota(jnp.int32, (C, 2 * C), 1)
    incl2 = ri2 >= jnp.where(ci2 >= C, ci2 - C, ci2)
    eye = jnp.where(ri == ci, 1.0, 0.0)

    units = [(gi, h) for gi in range(G) for h in range(H)]

    def cut(arr, u):
        gi, h = u
        return arr[gi * C:(gi + 1) * C, h * HEAD_DIM:(h + 1) * HEAD_DIM]

    hd = (lax.broadcasted_iota(jnp.int32, (LANES, LANES), 0) // HEAD_DIM
          == lax.broadcasted_iota(jnp.int32, (LANES, LANES), 1) // HEAD_DIM)
    head_ones = _bf(jnp.where(hd, 1.0, 0.0))

    def ones_dot(zb):
        return jnp.concatenate([_dot(zb[:, t:t + LANES], head_ones) for t in range(0, W, LANES)], axis=1)

    def head_sum(z):
        return ones_dot(_bf(z))

    kkn_w = kk * lax.rsqrt(jnp.maximum(head_sum(kk * kk), 1e-24))
    at_w = _bf(-kkn_w * e_ex)
    rt_w = _bf(r * e_in)
    bt_w = _bf(kkn_w * a * e_inv)
    kt_w = _bf(k2 * e_inv)
    v_w = _bf(v)
    at = [cut(at_w, u) for u in units]
    rt = [cut(rt_w, u) for u in units]
    v16 = [cut(v_w, u) for u in units]
    bk16 = [jnp.concatenate([cut(bt_w, u), cut(kt_w, u)], axis=0) for u in units]
    g_a = [_dot_nt(a_, m_) for a_, m_ in zip(at, bk16)]
    l_ab = [jnp.where(strict, g_[:, :C], 0.0) for g_ in g_a]
    a_ak = [_bf(jnp.where(strict, g_[:, C:], 0.0)) for g_ in g_a]
    tinv = [eye + l_ for l_ in l_ab]
    pw = l_ab
    n = 2
    while n < C:
        pw = [_dot(_bf(p_), _bf(p_)) for p_ in pw]
        tinv = [t_ + _dot(_bf(t_), _bf(p_)) for t_, p_ in zip(tinv, pw)]
        n *= 2
    s0 = [state_ref[i] for i in range(len(units))]
    s016 = [_bf(s_) for s_ in s0]
    rhs = [_dot_nt(a_, s_) + _dot(m_, v_) for a_, s_, m_, v_ in zip(at, s016, a_ak, v16)]
    uu = [_dot(_bf(t_), _bf(x_)) for t_, x_ in zip(tinv, rhs)]
    uv16 = [jnp.concatenate([_bf(u_), v_], axis=0) for u_, v_ in zip(uu, v16)]
    g_r = [_bf(jnp.where(incl2, _dot_nt(r_, m_), 0.0)) for r_, m_ in zip(rt, bk16)]
    y = [_dot(m_, x_) + _dot_nt(r_, s_) for m_, x_, r_, s_ in zip(g_r, uv16, rt, s016)]
    for i, u in enumerate(units):
        state_ref[i] = (s0[i] + _dot_tn(uv16[i], bk16[i])) * cut(e_in, u)[C - 1:C, :]
    yw = jnp.concatenate([jnp.concatenate(y[gi * H:(gi + 1) * H], axis=1) for gi in range(G)], axis=0)
    inv_d = 1.0 / HEAD_DIM
    y_hi = _bf(yw)
    yc = yw - (ones_dot(y_hi) + head_sum(yw - y_hi.astype(jnp.float32))) * inv_d
    yn = yc * lax.rsqrt(head_sum(yc * yc) * inv_d + GN_EPS) * lg_ref[...] + lb_ref[...]
    out = _bf((yn + head_sum(rk2) * v) * g)
    for gi in range(G):
        o_ref[gi] = out[gi * C:(gi + 1) * C, :]


def _rwkv(B, S, zb, mu, w0, w2, a0, a2, g2, k_k, k_a, r_k, lnx_g, lnx_b):
    C = min(S, CHUNK)
    G = min(B, RWKV_ROWS)
    blk = lambda b, c: (b, c, 0)
    fix = lambda b, c: (0, 0)
    vec = lambda n: pl.BlockSpec((1, n), fix)
    return pl.pallas_call(
        _rwkv_kernel,
        out_shape=jax.ShapeDtypeStruct((B, S, RWKV_WIDTH), jnp.bfloat16),
        grid=(B // G, S // C),
        in_specs=[pl.BlockSpec((G, C, B_COLS), blk), vec(B_COLS), vec(RWKV_WIDTH),
                  pl.BlockSpec((DECAY_RANK, RWKV_WIDTH), fix), vec(RWKV_WIDTH),
                  pl.BlockSpec((ICLR_RANK, RWKV_WIDTH), fix),
                  pl.BlockSpec((GATE_RANK, RWKV_WIDTH), fix),
                  vec(RWKV_WIDTH), vec(RWKV_WIDTH), vec(RWKV_WIDTH), vec(RWKV_WIDTH),
                  vec(RWKV_WIDTH)],
        out_specs=pl.BlockSpec((G, C, RWKV_WIDTH), blk),
        scratch_shapes=[pltpu.VMEM((8 * G, B_COLS), jnp.float32),
                        pltpu.VMEM((G * RWKV_HEADS, HEAD_DIM, HEAD_DIM), jnp.float32)],
        compiler_params=_params("parallel", "arbitrary"),
        name="rwkv7",
    )(zb.reshape(B, S, B_COLS), mu, w0, w2, a0, a2, g2, k_k, k_a, r_k, lnx_g, lnx_b
      ).reshape(B * S, RWKV_WIDTH)


def _pool_kernel(zc_ref, w_ref, sc_ref, o_ref):
    S = zc_ref.shape[0]
    x = zc_ref[...]
    row = lax.broadcasted_iota(jnp.int32, (S, 1), 0)
    lane_grp = lax.broadcasted_iota(jnp.int32, (1, POOL_WIDTH), 1) // POOL_GROUP

    def lag(y, n):
        return jnp.where(row >= n, pltpu.roll(y, n, 0), 0.0)

    w2 = x + lag(x, 1)
    w4 = w2 + lag(w2, 2)
    w8 = w4 + lag(w4, 4)
    w16 = w8 + lag(w8, 8)
    wsum = jnp.where(lane_grp == 0, w2, jnp.where(lane_grp == 1, w4, jnp.where(lane_grp == 2, w8, w16)))
    win = jnp.where(lane_grp == 0, 2, jnp.where(lane_grp == 1, 4, jnp.where(lane_grp == 2, 8, 16)))
    count = jnp.minimum(row + 1, win).astype(jnp.float32)
    pooled = wsum / count - x
    o_ref[...] = _bf(_dot(_bf(pooled), w_ref[...]) * sc_ref[...])


def _pool(B, S, zc, w_bd, scale):
    return pl.pallas_call(
        _pool_kernel,
        out_shape=jax.ShapeDtypeStruct((B * S, POOL_WIDTH), jnp.bfloat16),
        grid=(B,),
        in_specs=[pl.BlockSpec((S, POOL_WIDTH), lambda b: (b, 0)),
                  pl.BlockSpec((POOL_WIDTH, POOL_WIDTH), lambda b: (0, 0)),
                  pl.BlockSpec((1, POOL_WIDTH), lambda b: (0, 0))],
        out_specs=pl.BlockSpec((S, POOL_WIDTH), lambda b: (b, 0)),
        compiler_params=_params("parallel"),
        name="pool",
    )(zc, w_bd, scale)


def _outmlp_kernel(alpha, oa_ref, ob_ref, oc_ref, x_ref, p_ref, wa_ref, wb_ref, wc_ref, g1_ref, b1_ref,
                   w1_ref, w2_ref, pg_ref, pgb_ref, pp_ref, g2_ref, b2_ref, o_ref):
    tm = x_ref.shape[0]
    halves = [slice(0, tm // 2), slice(tm // 2, tm)]
    mix = [_dot(oa_ref[h, :], wa_ref[...]) + _dot(ob_ref[h, :], wb_ref[...]) + _dot(oc_ref[h, :], wc_ref[...])
           for h in halves]
    x1 = [_ln(alpha * x_ref[h, :] + m, g1_ref[...], b1_ref[...]) for h, m in zip(halves, mix)]
    xb = [_bf(v) for v in x1]
    gate = [jax.nn.sigmoid(_dot(b_, pg_ref[...]) + pgb_ref[...]) for b_ in xb]
    pe = [_dot(_bf(p_ref[0, h, :]), pp_ref[...]) for h in halves]
    acc = [alpha * v + g_ * e_ for v, g_, e_ in zip(x1, gate, pe)]
    xb_all = jnp.concatenate(xb, axis=0)
    nk = D_FF // TF_MLP
    for kf in range(nk):
        cols = slice(kf * TF_MLP, (kf + 1) * TF_MLP)
        hk = jnp.maximum(_dot(xb_all, w1_ref[:, cols]), 0.0)
        h2 = _bf(hk * hk)
        if kf < nk - 1:
            part = _dot(h2, w2_ref[cols, :])
            acc = [a + part[h, :] for a, h in zip(acc, halves)]
        else:
            acc = [a + _dot(h2[h, :], w2_ref[cols, :]) for a, h in zip(acc, halves)]
    for h, a in zip(halves, acc):
        o_ref[h, :] = _ln(a, g2_ref[...], b2_ref[...])


def _outmlp(alpha, layer, oa, ob, oc, x, p, wa, wb, wc, g1, b1, w1, w2, pg, pgb, pp, g2, b2):
    T = x.shape[0]
    tm = min(T, TM_MLP)
    row = lambda i: (i, 0)
    fix = lambda i: (0, 0)
    once = pl.Buffered(1)
    wspec = lambda r, c: pl.BlockSpec((r, c), fix, pipeline_mode=once)
    vec = pl.BlockSpec((1, D_MODEL), fix)
    return pl.pallas_call(
        functools.partial(_outmlp_kernel, alpha),
        out_shape=jax.ShapeDtypeStruct((T, D_MODEL), jnp.float32),
        grid=(T // tm,),
        in_specs=[pl.BlockSpec((tm, ATT_WIDTH), row), pl.BlockSpec((tm, RWKV_WIDTH), row),
                  pl.BlockSpec((tm, POOL_WIDTH), row), pl.BlockSpec((tm, D_MODEL), row),
                  pl.BlockSpec((1, tm, PE_DIM), lambda i: (layer, i, 0)),
                  wspec(ATT_WIDTH, D_MODEL), wspec(RWKV_WIDTH, D_MODEL), wspec(POOL_WIDTH, D_MODEL), vec, vec,
                  wspec(D_MODEL, D_FF), wspec(D_FF, D_MODEL), wspec(D_MODEL, D_MODEL), vec,
                  wspec(PE_DIM, D_MODEL), vec, vec],
        out_specs=pl.BlockSpec((tm, D_MODEL), row),
        compiler_params=_params("parallel"),
        name="outproj_mlp",
    )(oa, ob, oc, x, p, wa, wb, wc, g1, b1, w1, w2, pg, pgb, pp, g2, b2)


def _rot_partner(w):
    half = HEAD_DIM // 2
    return jnp.concatenate([-w[..., half:], w[..., :half]], axis=-1)


def _per_head_partner(w, heads):
    r = w.reshape(w.shape[0], heads, HEAD_DIM)
    return _rot_partner(r).reshape(w.shape)


def _arrange_w_in(w):
    cq = w[:, 0:256]
    ka = w[:, 256:320]
    va = w[:, 320:384]
    ik = w[:, 384:448]
    iw = w[:, 448:456]
    rest = w[:, 456:]
    pad = jnp.zeros((w.shape[0], LANES - HEAD_DIM - IDX_HEADS), w.dtype)
    return jnp.concatenate([cq, ka, ik, _rot_partner(ka), _rot_partner(ik), va, iw, pad, rest], axis=1)


def kernel(x, p, positions, ln_emb_g, ln_emb_b, w_in, w_uq, w_uqi, attn_norm_g, rwkv_mu, rwkv_w0, rwkv_w2, rwkv_a0, rwkv_a2, rwkv_g2, rwkv_k_k, rwkv_k_a, rwkv_r_k, rwkv_lnx_g, rwkv_lnx_b, pool_w, pool_scale, w_out, ln1_g, ln1_b, mlp_w1, mlp_w2, pe_proj, pe_gate, pe_gate_b, ln2_g, ln2_b):
    B, S, _ = x.shape
    T = B * S
    depth = w_in.shape[0]
    alpha = (2 * depth) ** 0.25
    bf = lambda a: a.astype(jnp.bfloat16)
    rowv = lambda a: a.reshape(1, -1)

    cos, sin = _rope_tables(positions.astype(jnp.float32).reshape(T // LANES, LANES))
    xs = x.reshape(T, D_MODEL)
    for i in range(depth):
        win = bf(_arrange_w_in(w_in[i]))
        wq = w_uq[i] * (HEAD_DIM ** -0.5)
        wi = w_uqi[i] * (IDX_DIM ** -0.5)
        res = _inproj(xs, rowv(ln_emb_g), rowv(ln_emb_b), cos, sin, win,
                      bf(wq), bf(_per_head_partner(wq, ATT_HEADS)),
                      bf(wi), bf(_per_head_partner(wi, IDX_HEADS)), apply_ln=(i == 0))
        if i == 0:
            xs, res = res[0], res[1:]
        qt, iqt, iwt, kr, ikr, vat, zb, zc = res
        o_a = _attention(B, S, qt, iqt, iwt, kr, ikr, vat, attn_norm_g[i].reshape(-1, 1))
        o_b = _rwkv(B, S, zb, rowv(rwkv_mu[i]), rowv(rwkv_w0[i]), bf(rwkv_w2[i]), rowv(rwkv_a0[i]),
                    bf(rwkv_a2[i]), bf(rwkv_g2[i]), rowv(rwkv_k_k[i]), rowv(rwkv_k_a[i]),
                    rowv(rwkv_r_k[i]), rowv(rwkv_lnx_g[i]), rowv(rwkv_lnx_b[i]))
        w_bd = jax.scipy.linalg.block_diag(*[pool_w[i, gi] for gi in range(len(POOL_WINDOWS))])
        o_c = _pool(B, S, zc, bf(w_bd), rowv(pool_scale[i]))
        wo = bf(w_out[i])
        xs = _outmlp(alpha, i, o_a, o_b, o_c, xs, p.reshape(depth, T, PE_DIM), wo[:ATT_WIDTH],
                     wo[ATT_WIDTH:ATT_WIDTH + RWKV_WIDTH], wo[ATT_WIDTH + RWKV_WIDTH:], rowv(ln1_g[i]),
                     rowv(ln1_b[i]), bf(mlp_w1[i]), bf(mlp_w2[i]), bf(pe_gate[i]), rowv(pe_gate_b[i]),
                     bf(pe_proj[i]), rowv(ln2_g[i]), rowv(ln2_b[i]))
    return xs.reshape(B, S, D_MODEL)
```

```python
import functools

import numpy as np
import jax
import jax.numpy as jnp
from jax import lax
from jax.experimental import pallas as pl
from jax.experimental.pallas import tpu as pltpu

D_MODEL = 1024
PE_DIM = 256
HEAD_DIM = 64
ROPE_THETA = 10000.0
ATT_WIDTH = 384
ATT_HEADS = 6
Q_RANK = 256
IDX_HEADS = 8
IDX_DIM = 64
TOPK_MAX = 256
RWKV_WIDTH = 384
RWKV_HEADS = 6
DECAY_RANK = 64
ICLR_RANK = 64
GATE_RANK = 128
GN_EPS = 64e-5
POOL_WINDOWS = (2, 4, 8, 16)
POOL_WIDTH = 256
POOL_GROUP = 64
B_COLS = 3 * RWKV_WIDTH + DECAY_RANK + ICLR_RANK + GATE_RANK
D_FF = 4 * D_MODEL
LN_EPS = 1e-5
RMS_EPS = 1e-6

COL_CQ = 0
COL_KIK = 256
COL_KIK_P = 384
COL_VIW = 512
COL_B = 640
COL_C = COL_B + B_COLS
N_IN_P = COL_C + POOL_WIDTH

LANES = 128
INT_MIN = -2 ** 31
LOG2E = 1.4426950408889634
VMEM_LIMIT = 56 * 1024 * 1024

TM_PROJ = 512
TQ = 256
ATT_KEY_CHUNK = 512
ATT_GROUP = 3
CHUNK = 64
RWKV_ROWS = 8
TM_MLP = 512
TF_MLP = 1024

_NT = (((1,), (1,)), ((), ()))
_TN = (((0,), (0,)), ((), ()))


def _params(*sem):
    return pltpu.CompilerParams(dimension_semantics=sem, vmem_limit_bytes=VMEM_LIMIT)


def _ln(x, g, b):
    mu = jnp.mean(x, axis=-1, keepdims=True)
    xc = x - mu
    var = jnp.mean(xc * xc, axis=-1, keepdims=True)
    return xc * lax.rsqrt(var + LN_EPS) * g + b


def _bf(x):
    return x.astype(jnp.bfloat16)


def _dot(a, b):
    return jnp.dot(a, b, preferred_element_type=jnp.float32)


def _dot_nt(a, b):
    return lax.dot_general(a, b, _NT, preferred_element_type=jnp.float32)


def _dot_tn(a, b):
    return lax.dot_general(a, b, _TN, preferred_element_type=jnp.float32)


def _rope_kernel(pos_ref, inv_ref, cos_ref, sin_ref, col_ref):
    quarter = LANES // 4
    grp = lax.broadcasted_iota(jnp.int32, (1, LANES), 1) // quarter
    for r in range(pos_ref.shape[0]):
        col_ref[...] = jnp.broadcast_to(pos_ref[r:r + 1, :], (LANES, LANES)).T
        p4 = col_ref[pl.ds(0, quarter, stride=4), :]
        for j in range(1, 4):
            p4 = jnp.where(grp == j, col_ref[pl.ds(j, quarter, stride=4), :], p4)
        ang4 = p4 * inv_ref[...]
        for tab_ref, fn in ((cos_ref, jnp.cos), (sin_ref, jnp.sin)):
            t4 = fn(ang4)
            rolls = [t4] + [pltpu.roll(t4, quarter * k, 1) for k in range(1, 4)]
            for j in range(4):
                y = rolls[(-j) % 4]
                for g in range(1, 4):
                    y = jnp.where(grp == g, rolls[(g - j) % 4], y)
                tab_ref[pl.ds(r * LANES + j, quarter, stride=4), :] = y


def _rope_tables(pos_f32):
    T = pos_f32.shape[0] * LANES
    rows = min(pos_f32.shape[0], 16)
    inv = ROPE_THETA ** (-np.arange(0, HEAD_DIM, 2, dtype=np.float32) / HEAD_DIM)
    inv128 = jnp.asarray(np.tile(inv, LANES // (HEAD_DIM // 2))[None, :], jnp.float32)
    return pl.pallas_call(
        _rope_kernel,
        out_shape=(jax.ShapeDtypeStruct((T, LANES), jnp.float32),) * 2,
        grid=(pos_f32.shape[0] // rows,),
        in_specs=[pl.BlockSpec((rows, LANES), lambda i: (i, 0)),
                  pl.BlockSpec((1, LANES), lambda i: (0, 0))],
        out_specs=(pl.BlockSpec((rows * LANES, LANES), lambda i: (i, 0)),) * 2,
        scratch_shapes=[pltpu.VMEM((LANES, LANES), jnp.float32)],
        compiler_params=_params("parallel"),
        name="rope_tables",
    )(pos_f32, inv128)


def _inproj_kernel(apply_ln, x_ref, g_ref, b_ref, cos_ref, sin_ref, win_ref, wq_ref, wqp_ref,
                   wi_ref, wip_ref, *out_refs):
    if apply_ln:
        xn_ref, qt_ref, iqt_ref, iwt_ref, kr_ref, ikr_ref, vat_ref, zb_ref, zc_ref = out_refs
        x = _ln(x_ref[...], g_ref[...], b_ref[...])
        xn_ref[...] = x
    else:
        qt_ref, iqt_ref, iwt_ref, kr_ref, ikr_ref, vat_ref, zb_ref, zc_ref = out_refs
        x = x_ref[...]
    z = _dot(_bf(x), win_ref[...])
    cos = cos_ref[...]
    sin = sin_ref[...]
    cq = _bf(z[:, COL_CQ:COL_CQ + Q_RANK])
    cos3 = jnp.concatenate([cos] * 3, axis=1)
    sin3 = jnp.concatenate([sin] * 3, axis=1)
    qt_ref[...] = _bf(((_dot(cq, wq_ref[...]) * cos3 + _dot(cq, wqp_ref[...]) * sin3) * LOG2E).T)
    cos4 = jnp.concatenate([cos] * 4, axis=1)
    sin4 = jnp.concatenate([sin] * 4, axis=1)
    iqt_ref[...] = _bf((_dot(cq, wi_ref[...]) * cos4 + _dot(cq, wip_ref[...]) * sin4).T)
    kik = z[:, COL_KIK:COL_KIK + LANES] * cos + z[:, COL_KIK_P:COL_KIK_P + LANES] * sin
    kr_ref[...] = _bf(kik[:, :HEAD_DIM])
    ikr_ref[...] = _bf(kik[:, HEAD_DIM:])
    viw_t = z[:, COL_VIW:COL_VIW + LANES].T
    vat_ref[...] = _bf(viw_t[:HEAD_DIM])
    iwt_ref[...] = viw_t[HEAD_DIM:HEAD_DIM + IDX_HEADS] * (IDX_HEADS ** -0.5)
    zb_ref[...] = z[:, COL_B:COL_B + B_COLS]
    zc_ref[...] = z[:, COL_C:COL_C + POOL_WIDTH]


def _inproj(x, g, b, cos, sin, win, wq, wqp, wi, wip, apply_ln):
    T = x.shape[0]
    tm = min(T, TM_PROJ)
    row = lambda i: (i, 0)
    fix = lambda i: (0, 0)
    bf16, f32 = jnp.bfloat16, jnp.float32
    col = lambda i: (0, i)
    tok = lambda n, dt: (jax.ShapeDtypeStruct((T, n), dt), pl.BlockSpec((tm, n), row))
    feat = lambda n, dt: (jax.ShapeDtypeStruct((n, T), dt), pl.BlockSpec((n, tm), col))
    outs = [feat(ATT_WIDTH, bf16), feat(IDX_HEADS * IDX_DIM, bf16), feat(IDX_HEADS, f32), tok(HEAD_DIM, bf16),
            tok(HEAD_DIM, bf16), feat(HEAD_DIM, bf16), tok(B_COLS, f32), tok(POOL_WIDTH, f32)]
    shapes = [o[0] for o in outs]
    specs = [o[1] for o in outs]
    if apply_ln:
        shapes = [jax.ShapeDtypeStruct((T, D_MODEL), f32)] + shapes
        specs = [pl.BlockSpec((tm, D_MODEL), row)] + specs
    return pl.pallas_call(
        functools.partial(_inproj_kernel, apply_ln),
        out_shape=tuple(shapes),
        grid=(T // tm,),
        in_specs=[pl.BlockSpec((tm, D_MODEL), row),
                  pl.BlockSpec((1, D_MODEL), fix), pl.BlockSpec((1, D_MODEL), fix),
                  pl.BlockSpec((tm, LANES), row), pl.BlockSpec((tm, LANES), row),
                  pl.BlockSpec((D_MODEL, N_IN_P), fix),
                  pl.BlockSpec((Q_RANK, ATT_WIDTH), fix), pl.BlockSpec((Q_RANK, ATT_WIDTH), fix),
                  pl.BlockSpec((Q_RANK, IDX_HEADS * IDX_DIM), fix),
                  pl.BlockSpec((Q_RANK, IDX_HEADS * IDX_DIM), fix)],
        out_specs=tuple(specs),
        compiler_params=_params("parallel"),
        name="inproj",
    )(x, g, b, cos, sin, win, wq, wqp, wi, wip)


def _attn_block(kend, search, topk, qt_ref, iqt_ref, iwt_ref, kr_ref, ikr_ref, vat_ref, g_ref, o_ref,
                key_ref, bias_ref, ot_ref, hi_ref, lo_ref):
    tq = qt_ref.shape[1]
    q_idx = pl.program_id(1) * tq + lax.broadcasted_iota(jnp.int32, (1, tq), 1)
    k_idx = lax.broadcasted_iota(jnp.int32, (kend, 1), 0)
    causal = k_idx <= q_idx

    if not search:
        bias_ref[:kend, :] = jnp.where(causal, 0.0, -jnp.inf)
    else:
        def idx_pair(j, first):
            ik = ikr_ref[0:kend, :]
            heads = [2 * j, 2 * j + 1]
            rows = [h * IDX_DIM if first else pl.multiple_of(h * IDX_DIM, IDX_DIM) for h in heads]
            lgs = [_dot(ik, iqt_ref[pl.ds(r, IDX_DIM), :]) for r in rows]
            acc = None if first else bias_ref[:kend, :]
            for h, lg in zip(heads, lgs):
                term = jnp.maximum(lg, 0.0) * iwt_ref[pl.ds(h, 1), :]
                acc = term if acc is None else acc + term
            bias_ref[:kend, :] = acc

        idx_pair(0, True)
        lax.fori_loop(1, IDX_HEADS // 2, lambda j, c: (idx_pair(j, False), c)[1], 0)
        score = bias_ref[:kend, :] + 0.0
        bits = pltpu.bitcast(score, jnp.int32)
        key = bits ^ ((bits >> 31) & 0x7FFFFFFF)
        key_ref[:kend, :] = jnp.where(causal, key, INT_MIN)

        i16 = jnp.int16
        pack = 16
        one_b = jnp.ones((pack, tq), jnp.bfloat16)
        zero_b = jnp.zeros((pack, tq), jnp.bfloat16)

        def count16(ref, cand):
            c16 = jnp.broadcast_to(cand, (pack, tq)).astype(i16)
            accs = [zero_b] * 4
            for n, i in enumerate(range(0, kend, pack)):
                accs[n % 4] = accs[n % 4] + jnp.where(ref[i:i + pack, :] >= c16, one_b, zero_b)
            acc = (accs[0] + accs[1]) + (accs[2] + accs[3])
            return jnp.sum(acc.astype(jnp.float32), axis=0, keepdims=True)

        def kth_largest16(ref, k):
            t = jnp.where(count16(ref, jnp.zeros((1, tq), jnp.int32)) >= k, 0, -32768).astype(jnp.int32)

            def bit_step(i, t):
                cand = t | (jnp.int32(1) << (14 - i))
                return jnp.where(count16(ref, cand) >= k, cand, t)

            return lax.fori_loop(0, 15, bit_step, t)

        keyv = key_ref[:kend, :]
        hi_ref[:kend, :] = (keyv >> 16).astype(i16)
        tau_hi = kth_largest16(hi_ref, jnp.full((1, tq), float(topk), jnp.float32))
        above = jnp.where(tau_hi < 32767, count16(hi_ref, jnp.minimum(tau_hi + 1, 32767)), 0.0)
        lo = ((keyv & 0xFFFF) - 32768).astype(i16)
        same_hi = hi_ref[:kend, :] == jnp.broadcast_to(tau_hi, (kend, tq)).astype(i16)
        lo_ref[:kend, :] = jnp.where(same_hi, lo, -32768)
        tau_lo = kth_largest16(lo_ref, topk - above)
        tau = (tau_hi << 16) | (tau_lo + 32768)
        tau = jnp.maximum(tau, INT_MIN + 1)
        ge = key_ref[:kend, :] >= tau
        n_ge = jnp.sum(jnp.where(ge, 1.0, 0.0), axis=0, keepdims=True)
        bias_ref[:kend, :] = jnp.where(ge, 0.0, -jnp.inf)

        @pl.when(jnp.max(n_ge) > topk)
        def _():
            need = topk - jnp.sum(jnp.where(key_ref[:kend, :] > tau, 1.0, 0.0), axis=0, keepdims=True)
            ri = lax.broadcasted_iota(jnp.int32, (LANES, LANES), 0)
            ci = lax.broadcasted_iota(jnp.int32, (LANES, LANES), 1)
            lower = _bf(jnp.where(ri >= ci, 1.0, 0.0))
            before = jnp.zeros((1, tq), jnp.float32)
            for c in range(0, kend, LANES):
                keyc = key_ref[c:c + LANES, :]
                eq = keyc == tau
                rank = _dot(lower, _bf(jnp.where(eq, 1.0, 0.0))) + before
                keep = (keyc > tau) | (eq & (rank <= need))
                bias_ref[c:c + LANES, :] = jnp.where(keep, 0.0, -jnp.inf)
                before = rank[LANES - 1:LANES, :]

    kc = min(kend, ATT_KEY_CHUNK)
    chunks = range(0, kend, kc)

    def att_pair(j, carry):
        heads = [ATT_GROUP * j + i for i in range(ATT_GROUP)]
        qs = [qt_ref[pl.ds(pl.multiple_of(h * HEAD_DIM, HEAD_DIM), HEAD_DIM), :] for h in heads]
        s_h = [[_dot(kr_ref[c:c + kc, :], q_) + bias_ref[c:c + kc, :] for c in chunks] for q_ in qs]
        for h, s_c in zip(heads, s_h):
            m = jnp.max(functools.reduce(jnp.maximum, s_c), axis=0, keepdims=True)
            p_c = [jnp.exp2(s - m) for s in s_c]
            l = jnp.sum(sum(p_c), axis=0, keepdims=True)
            acc = sum(_dot(vat_ref[:, c:c + kc], _bf(p)) for p, c in zip(p_c, chunks))
            ot_ref[pl.ds(pl.multiple_of(h * HEAD_DIM, HEAD_DIM), HEAD_DIM), :] = acc / l
        return carry

    lax.fori_loop(0, ATT_HEADS // ATT_GROUP, att_pair, 0)
    ot = ot_ref[...]
    ot = ot * lax.rsqrt(jnp.mean(ot * ot, axis=0, keepdims=True) + RMS_EPS) * g_ref[...]
    o_ref[...] = _bf(ot.T)


def _key_extents(S, tq, topk):
    out = []
    for v in range(S // tq):
        need = (v + 1) * tq
        if need <= topk:
            out.append((need, False))
        else:
            out.append((min(S, -(-need // (2 * tq)) * 2 * tq), True))
    return out


def _attn_kernel(S, *refs):
    tq = refs[0].shape[1]
    topk = min(TOPK_MAX, S // 4)
    qi = pl.program_id(1)
    extents = _key_extents(S, tq, topk)
    for ext in sorted(set(extents)):
        blocks = [v for v, e in enumerate(extents) if e == ext]
        pl.when((qi >= blocks[0]) & (qi <= blocks[-1]))(functools.partial(_attn_block, *ext, topk, *refs))


def _attention(B, S, qt, iqt, iwt, kr, ikr, vat, g):
    tq = min(S, TQ)
    nq = S // tq
    qcol = lambda b, i: (0, b * nq + i)
    seq = lambda b, i: (b, 0)
    return pl.pallas_call(
        functools.partial(_attn_kernel, S),
        out_shape=jax.ShapeDtypeStruct((B * S, ATT_WIDTH), jnp.bfloat16),
        grid=(B, nq),
        in_specs=[pl.BlockSpec((ATT_WIDTH, tq), qcol),
                  pl.BlockSpec((IDX_HEADS * IDX_DIM, tq), qcol),
                  pl.BlockSpec((IDX_HEADS, tq), qcol),
                  pl.BlockSpec((S, HEAD_DIM), seq), pl.BlockSpec((S, HEAD_DIM), seq),
                  pl.BlockSpec((HEAD_DIM, S), lambda b, i: (0, b)),
                  pl.BlockSpec((ATT_WIDTH, 1), lambda b, i: (0, 0))],
        out_specs=pl.BlockSpec((tq, ATT_WIDTH), lambda b, i: (b * nq + i, 0)),
        scratch_shapes=[pltpu.VMEM((S, tq), jnp.int32), pltpu.VMEM((S, tq), jnp.float32),
                        pltpu.VMEM((ATT_WIDTH, tq), jnp.float32),
                        pltpu.VMEM((S, tq), jnp.int16), pltpu.VMEM((S, tq), jnp.int16)],
        compiler_params=_params("parallel", "parallel"),
        name="dsa_attention",
    )(qt, iqt, iwt, kr, ikr, vat, g)


def _split3(x):
    hi = _bf(x)
    r1 = x - hi.astype(jnp.float32)
    mid = _bf(r1)
    lo = _bf(r1 - mid.astype(jnp.float32))
    return hi, mid, lo


def _rwkv_kernel(zb_ref, mu_ref, w0_ref, w2_ref, a0_ref, a2_ref, g2_ref, kk_ref, ka_ref, rk_ref,
                 lg_ref, lb_ref, o_ref, prev_ref, state_ref):
    G, C, _ = zb_ref.shape
    W = RWKV_WIDTH
    R = G * C
    H = RWKV_HEADS

    @pl.when(pl.program_id(1) == 0)
    def _():
        prev_ref[...] = jnp.zeros_like(prev_ref)
        state_ref[...] = jnp.zeros_like(state_ref)

    x = zb_ref[...].reshape(R, B_COLS)
    rowi = lax.broadcasted_iota(jnp.int32, (R, 1), 0)
    shifted = pltpu.roll(x, 1, 0)
    for gi in range(G):
        shifted = jnp.where(rowi == gi * C, prev_ref[8 * gi:8 * gi + 1, :], shifted)
        prev_ref[8 * gi:8 * gi + 1, :] = x[gi * C + C - 1:gi * C + C, :]
    xm = x + (shifted - x) * mu_ref[...]
    r = xm[:, 0:W]
    k = xm[:, W:2 * W]
    v = xm[:, 2 * W:3 * W]
    wl = xm[:, 3 * W:3 * W + DECAY_RANK]
    al = xm[:, 3 * W + DECAY_RANK:3 * W + DECAY_RANK + ICLR_RANK]
    gl = xm[:, 3 * W + DECAY_RANK + ICLR_RANK:]

    wpre = w0_ref[...] + _dot(_bf(jnp.tanh(wl)), w2_ref[...])
    nw = -wpre
    w = -(jnp.maximum(nw, 0.0) + jnp.log(1.0 + jnp.exp(-jnp.abs(nw)))) - 0.5
    logd = -jnp.exp(w)
    a = jax.nn.sigmoid(a0_ref[...] + _dot(_bf(al), a2_ref[...]))
    g = _dot(_bf(jax.nn.sigmoid(gl)), g2_ref[...])

    rr = lax.broadcasted_iota(jnp.int32, (R, R), 0)
    cc = lax.broadcasted_iota(jnp.int32, (R, R), 1)
    tri = _bf(jnp.where((rr >= cc) & (rr // C == cc // C), 1.0, 0.0))
    hi, mid, lo = _split3(logd)
    cw = _dot(tri, hi) + _dot(tri, mid) + _dot(tri, lo)
    e_in = jnp.exp(cw)
    e_ex = jnp.exp(cw - logd)
    e_inv = jnp.exp(-cw)

    kk = k * kk_ref[...]
    k2 = k * (1.0 + (a - 1.0) * ka_ref[...])
    rk2 = r * k2 * rk_ref[...]
    ri = lax.broadcasted_iota(jnp.int32, (C, C), 0)
    ci = lax.broadcasted_iota(jnp.int32, (C, C), 1)
    strict = ri > ci
    ri2 = lax.broadcasted_iota(jnp.int32, (C, 2 * C), 0)
    ci2 = lax.broadcasted_iota(jnp.int32, (C, 2 * C), 1)
    incl2 = ri2 >= jnp.where(ci2 >= C, ci2 - C, ci2)
    eye = jnp.where(ri == ci, 1.0, 0.0)

    units = [(gi, h) for gi in range(G) for h in range(H)]

    def cut(arr, u):
        gi, h = u
        return arr[gi * C:(gi + 1) * C, h * HEAD_DIM:(h + 1) * HEAD_DIM]

    hd = (lax.broadcasted_iota(jnp.int32, (LANES, LANES), 0) // HEAD_DIM
          == lax.broadcasted_iota(jnp.int32, (LANES, LANES), 1) // HEAD_DIM)
    head_ones = _bf(jnp.where(hd, 1.0, 0.0))

    def ones_dot(zb):
        return jnp.concatenate([_dot(zb[:, t:t + LANES], head_ones) for t in range(0, W, LANES)], axis=1)

    def head_sum(z):
        return ones_dot(_bf(z))

    kkn_w = kk * lax.rsqrt(jnp.maximum(head_sum(kk * kk), 1e-24))
    at_w = _bf(-kkn_w * e_ex)
    rt_w = _bf(r * e_in)
    bt_w = _bf(kkn_w * a * e_inv)
    kt_w = _bf(k2 * e_inv)
    v_w = _bf(v)
    at = [cut(at_w, u) for u in units]
    rt = [cut(rt_w, u) for u in units]
    v16 = [cut(v_w, u) for u in units]
    bk16 = [jnp.concatenate([cut(bt_w, u), cut(kt_w, u)], axis=0) for u in units]
    g_a = [_dot_nt(a_, m_) for a_, m_ in zip(at, bk16)]
    l_ab = [jnp.where(strict, g_[:, :C], 0.0) for g_ in g_a]
    a_ak = [_bf(jnp.where(strict, g_[:, C:], 0.0)) for g_ in g_a]
    tinv = [eye + l_ for l_ in l_ab]
    pw = l_ab
    n = 2
    while n < C:
        pw = [_dot(_bf(p_), _bf(p_)) for p_ in pw]
        tinv = [t_ + _dot(_bf(t_), _bf(p_)) for t_, p_ in zip(tinv, pw)]
        n *= 2
    s0 = [state_ref[i] for i in range(len(units))]
    s016 = [_bf(s_) for s_ in s0]
    rhs = [_dot_nt(a_, s_) + _dot(m_, v_) for a_, s_, m_, v_ in zip(at, s016, a_ak, v16)]
    uu = [_dot(_bf(t_), _bf(x_)) for t_, x_ in zip(tinv, rhs)]
    uv16 = [jnp.concatenate([_bf(u_), v_], axis=0) for u_, v_ in zip(uu, v16)]
    g_r = [_bf(jnp.where(incl2, _dot_nt(r_, m_), 0.0)) for r_, m_ in zip(rt, bk16)]
    y = [_dot(m_, x_) + _dot_nt(r_, s_) for m_, x_, r_, s_ in zip(g_r, uv16, rt, s016)]
    for i, u in enumerate(units):
        state_ref[i] = (s0[i] + _dot_tn(uv16[i], bk16[i])) * cut(e_in, u)[C - 1:C, :]
    yw = jnp.concatenate([jnp.concatenate(y[gi * H:(gi + 1) * H], axis=1) for gi in range(G)], axis=0)
    inv_d = 1.0 / HEAD_DIM
    y_hi = _bf(yw)
    yc = yw - (ones_dot(y_hi) + head_sum(yw - y_hi.astype(jnp.float32))) * inv_d
    yn = yc * lax.rsqrt(head_sum(yc * yc) * inv_d + GN_EPS) * lg_ref[...] + lb_ref[...]
    out = _bf((yn + head_sum(rk2) * v) * g)
    for gi in range(G):
        o_ref[gi] = out[gi * C:(gi + 1) * C, :]


def _rwkv(B, S, zb, mu, w0, w2, a0, a2, g2, k_k, k_a, r_k, lnx_g, lnx_b):
    C = min(S, CHUNK)
    G = min(B, RWKV_ROWS)
    blk = lambda b, c: (b, c, 0)
    fix = lambda b, c: (0, 0)
    vec = lambda n: pl.BlockSpec((1, n), fix)
    return pl.pallas_call(
        _rwkv_kernel,
        out_shape=jax.ShapeDtypeStruct((B, S, RWKV_WIDTH), jnp.bfloat16),
        grid=(B // G, S // C),
        in_specs=[pl.BlockSpec((G, C, B_COLS), blk), vec(B_COLS), vec(RWKV_WIDTH),
                  pl.BlockSpec((DECAY_RANK, RWKV_WIDTH), fix), vec(RWKV_WIDTH),
                  pl.BlockSpec((ICLR_RANK, RWKV_WIDTH), fix),
                  pl.BlockSpec((GATE_RANK, RWKV_WIDTH), fix),
                  vec(RWKV_WIDTH), vec(RWKV_WIDTH), vec(RWKV_WIDTH), vec(RWKV_WIDTH),
                  vec(RWKV_WIDTH)],
        out_specs=pl.BlockSpec((G, C, RWKV_WIDTH), blk),
        scratch_shapes=[pltpu.VMEM((8 * G, B_COLS), jnp.float32),
                        pltpu.VMEM((G * RWKV_HEADS, HEAD_DIM, HEAD_DIM), jnp.float32)],
        compiler_params=_params("parallel", "arbitrary"),
        name="rwkv7",
    )(zb.reshape(B, S, B_COLS), mu, w0, w2, a0, a2, g2, k_k, k_a, r_k, lnx_g, lnx_b
      ).reshape(B * S, RWKV_WIDTH)


def _pool_kernel(zc_ref, w_ref, sc_ref, o_ref):
    S = zc_ref.shape[0]
    x = zc_ref[...]
    row = lax.broadcasted_iota(jnp.int32, (S, 1), 0)
    lane_grp = lax.broadcasted_iota(jnp.int32, (1, POOL_WIDTH), 1) // POOL_GROUP

    def lag(y, n):
        return jnp.where(row >= n, pltpu.roll(y, n, 0), 0.0)

    w2 = x + lag(x, 1)
    w4 = w2 + lag(w2, 2)
    w8 = w4 + lag(w4, 4)
    w16 = w8 + lag(w8, 8)
    wsum = jnp.where(lane_grp == 0, w2, jnp.where(lane_grp == 1, w4, jnp.where(lane_grp == 2, w8, w16)))
    win = jnp.where(lane_grp == 0, 2, jnp.where(lane_grp == 1, 4, jnp.where(lane_grp == 2, 8, 16)))
    count = jnp.minimum(row + 1, win).astype(jnp.float32)
    pooled = wsum / count - x
    o_ref[...] = _bf(_dot(_bf(pooled), w_ref[...]) * sc_ref[...])


def _pool(B, S, zc, w_bd, scale):
    return pl.pallas_call(
        _pool_kernel,
        out_shape=jax.ShapeDtypeStruct((B * S, POOL_WIDTH), jnp.bfloat16),
        grid=(B,),
        in_specs=[pl.BlockSpec((S, POOL_WIDTH), lambda b: (b, 0)),
                  pl.BlockSpec((POOL_WIDTH, POOL_WIDTH), lambda b: (0, 0)),
                  pl.BlockSpec((1, POOL_WIDTH), lambda b: (0, 0))],
        out_specs=pl.BlockSpec((S, POOL_WIDTH), lambda b: (b, 0)),
        compiler_params=_params("parallel"),
        name="pool",
    )(zc, w_bd, scale)


def _outmlp_kernel(alpha, oa_ref, ob_ref, oc_ref, x_ref, p_ref, wa_ref, wb_ref, wc_ref, g1_ref, b1_ref,
                   w1_ref, w2_ref, pg_ref, pgb_ref, pp_ref, g2_ref, b2_ref, o_ref):
    tm = x_ref.shape[0]
    halves = [slice(0, tm // 2), slice(tm // 2, tm)]
    mix = [_dot(oa_ref[h, :], wa_ref[...]) + _dot(ob_ref[h, :], wb_ref[...]) + _dot(oc_ref[h, :], wc_ref[...])
           for h in halves]
    x1 = [_ln(alpha * x_ref[h, :] + m, g1_ref[...], b1_ref[...]) for h, m in zip(halves, mix)]
    xb = [_bf(v) for v in x1]
    gate = [jax.nn.sigmoid(_dot(b_, pg_ref[...]) + pgb_ref[...]) for b_ in xb]
    pe = [_dot(_bf(p_ref[0, h, :]), pp_ref[...]) for h in halves]
    acc = [alpha * v + g_ * e_ for v, g_, e_ in zip(x1, gate, pe)]
    xb_all = jnp.concatenate(xb, axis=0)
    nk = D_FF // TF_MLP
    for kf in range(nk):
        cols = slice(kf * TF_MLP, (kf + 1) * TF_MLP)
        hk = jnp.maximum(_dot(xb_all, w1_ref[:, cols]), 0.0)
        h2 = _bf(hk * hk)
        if kf < nk - 1:
            part = _dot(h2, w2_ref[cols, :])
            acc = [a + part[h, :] for a, h in zip(acc, halves)]
        else:
            acc = [a + _dot(h2[h, :], w2_ref[cols, :]) for a, h in zip(acc, halves)]
    for h, a in zip(halves, acc):
        o_ref[h, :] = _ln(a, g2_ref[...], b2_ref[...])


def _outmlp(alpha, layer, oa, ob, oc, x, p, wa, wb, wc, g1, b1, w1, w2, pg, pgb, pp, g2, b2):
    T = x.shape[0]
    tm = min(T, TM_MLP)
    row = lambda i: (i, 0)
    fix = lambda i: (0, 0)
    once = pl.Buffered(1)
    wspec = lambda r, c: pl.BlockSpec((r, c), fix, pipeline_mode=once)
    vec = pl.BlockSpec((1, D_MODEL), fix)
    return pl.pallas_call(
        functools.partial(_outmlp_kernel, alpha),
        out_shape=jax.ShapeDtypeStruct((T, D_MODEL), jnp.float32),
        grid=(T // tm,),
        in_specs=[pl.BlockSpec((tm, ATT_WIDTH), row), pl.BlockSpec((tm, RWKV_WIDTH), row),
                  pl.BlockSpec((tm, POOL_WIDTH), row), pl.BlockSpec((tm, D_MODEL), row),
                  pl.BlockSpec((1, tm, PE_DIM), lambda i: (layer, i, 0)),
                  wspec(ATT_WIDTH, D_MODEL), wspec(RWKV_WIDTH, D_MODEL), wspec(POOL_WIDTH, D_MODEL), vec, vec,
                  wspec(D_MODEL, D_FF), wspec(D_FF, D_MODEL), wspec(D_MODEL, D_MODEL), vec,
                  wspec(PE_DIM, D_MODEL), vec, vec],
        out_specs=pl.BlockSpec((tm, D_MODEL), row),
        compiler_params=_params("parallel"),
        name="outproj_mlp",
    )(oa, ob, oc, x, p, wa, wb, wc, g1, b1, w1, w2, pg, pgb, pp, g2, b2)


def _rot_partner(w):
    half = HEAD_DIM // 2
    return jnp.concatenate([-w[..., half:], w[..., :half]], axis=-1)


def _per_head_partner(w, heads):
    r = w.reshape(w.shape[0], heads, HEAD_DIM)
    return _rot_partner(r).reshape(w.shape)


def _arrange_w_in(w):
    cq = w[:, 0:256]
    ka = w[:, 256:320]
    va = w[:, 320:384]
    ik = w[:, 384:448]
    iw = w[:, 448:456]
    rest = w[:, 456:]
    pad = jnp.zeros((w.shape[0], LANES - HEAD_DIM - IDX_HEADS), w.dtype)
    return jnp.concatenate([cq, ka, ik, _rot_partner(ka), _rot_partner(ik), va, iw, pad, rest], axis=1)


def kernel(x, p, positions, ln_emb_g, ln_emb_b, w_in, w_uq, w_uqi, attn_norm_g, rwkv_mu, rwkv_w0, rwkv_w2, rwkv_a0, rwkv_a2, rwkv_g2, rwkv_k_k, rwkv_k_a, rwkv_r_k, rwkv_lnx_g, rwkv_lnx_b, pool_w, pool_scale, w_out, ln1_g, ln1_b, mlp_w1, mlp_w2, pe_proj, pe_gate, pe_gate_b, ln2_g, ln2_b):
    B, S, _ = x.shape
    T = B * S
    depth = w_in.shape[0]
    alpha = (2 * depth) ** 0.25
    bf = lambda a: a.astype(jnp.bfloat16)
    rowv = lambda a: a.reshape(1, -1)

    cos, sin = _rope_tables(positions.astype(jnp.float32).reshape(T // LANES, LANES))
    xs = x.reshape(T, D_MODEL)
    for i in range(depth):
        win = bf(_arrange_w_in(w_in[i]))
        wq = w_uq[i] * (HEAD_DIM ** -0.5)
        wi = w_uqi[i] * (IDX_DIM ** -0.5)
        res = _inproj(xs, rowv(ln_emb_g), rowv(ln_emb_b), cos, sin, win,
                      bf(wq), bf(_per_head_partner(wq, ATT_HEADS)),
                      bf(wi), bf(_per_head_partner(wi, IDX_HEADS)), apply_ln=(i == 0))
        if i == 0:
            xs, res = res[0], res[1:]
        qt, iqt, iwt, kr, ikr, vat, zb, zc = res
        o_a = _attention(B, S, qt, iqt, iwt, kr, ikr, vat, attn_norm_g[i].reshape(-1, 1))
        o_b = _rwkv(B, S, zb, rowv(rwkv_mu[i]), rowv(rwkv_w0[i]), bf(rwkv_w2[i]), rowv(rwkv_a0[i]),
                    bf(rwkv_a2[i]), bf(rwkv_g2[i]), rowv(rwkv_k_k[i]), rowv(rwkv_k_a[i]),
                    rowv(rwkv_r_k[i]), rowv(rwkv_lnx_g[i]), rowv(rwkv_lnx_b[i]))
        w_bd = jax.scipy.linalg.block_diag(*[pool_w[i, gi] for gi in range(len(POOL_WINDOWS))])
        o_c = _pool(B, S, zc, bf(w_bd), rowv(pool_scale[i]))
        wo = bf(w_out[i])
        xs = _outmlp(alpha, i, o_a, o_b, o_c, xs, p.reshape(depth, T, PE_DIM), wo[:ATT_WIDTH],
                     wo[ATT_WIDTH:ATT_WIDTH + RWKV_WIDTH], wo[ATT_WIDTH + RWKV_WIDTH:], rowv(ln1_g[i]),
                     rowv(ln1_b[i]), bf(mlp_w1[i]), bf(mlp_w2[i]), bf(pe_gate[i]), rowv(pe_gate_b[i]),
                     bf(pe_proj[i]), rowv(ln2_g[i]), rowv(ln2_b[i]))
    return xs.reshape(B, S, D_MODEL)
```

```python
import functools

import numpy as np
import jax
import jax.numpy as jnp
from jax import lax
from jax.experimental import pallas as pl
from jax.experimental.pallas import tpu as pltpu

D_MODEL = 1024
PE_DIM = 256
HEAD_DIM = 64
ROPE_THETA = 10000.0
ATT_WIDTH = 384
ATT_HEADS = 6
Q_RANK = 256
IDX_HEADS = 8
IDX_DIM = 64
TOPK_MAX = 256
RWKV_WIDTH = 384
RWKV_HEADS = 6
DECAY_RANK = 64
ICLR_RANK = 64
GATE_RANK = 128
GN_EPS = 64e-5
POOL_WINDOWS = (2, 4, 8, 16)
POOL_WIDTH = 256
POOL_GROUP = 64
B_COLS = 3 * RWKV_WIDTH + DECAY_RANK + ICLR_RANK + GATE_RANK
D_FF = 4 * D_MODEL
LN_EPS = 1e-5
RMS_EPS = 1e-6

COL_CQ = 0
COL_KIK = 256
COL_KIK_P = 384
COL_VIW = 512
COL_B = 640
COL_C = COL_B + B_COLS
N_IN_P = COL_C + POOL_WIDTH

LANES = 128
INT_MIN = -2 ** 31
LOG2E = 1.4426950408889634
VMEM_LIMIT = 56 * 1024 * 1024

TM_PROJ = 1024
TQ = 256
ATT_KEY_CHUNK = 512
ATT_GROUP = 3
CHUNK = 64
RWKV_ROWS = 8
TM_MLP = 512
TF_MLP = 1024

_NT = (((1,), (1,)), ((), ()))
_TN = (((0,), (0,)), ((), ()))


def _params(*sem):
    return pltpu.CompilerParams(dimension_semantics=sem, vmem_limit_bytes=VMEM_LIMIT)


def _ln(x, g, b):
    mu = jnp.mean(x, axis=-1, keepdims=True)
    xc = x - mu
    var = jnp.mean(xc * xc, axis=-1, keepdims=True)
    return xc * lax.rsqrt(var + LN_EPS) * g + b


def _bf(x):
    return x.astype(jnp.bfloat16)


def _dot(a, b):
    return jnp.dot(a, b, preferred_element_type=jnp.float32)


def _dot_nt(a, b):
    return lax.dot_general(a, b, _NT, preferred_element_type=jnp.float32)


def _dot_tn(a, b):
    return lax.dot_general(a, b, _TN, preferred_element_type=jnp.float32)


def _rope_kernel(pos_ref, inv_ref, cos_ref, sin_ref, col_ref):
    quarter = LANES // 4
    grp = lax.broadcasted_iota(jnp.int32, (1, LANES), 1) // quarter
    for r in range(pos_ref.shape[0]):
        col_ref[...] = jnp.broadcast_to(pos_ref[r:r + 1, :], (LANES, LANES)).T
        p4 = col_ref[pl.ds(0, quarter, stride=4), :]
        for j in range(1, 4):
            p4 = jnp.where(grp == j, col_ref[pl.ds(j, quarter, stride=4), :], p4)
        ang4 = p4 * inv_ref[...]
        for tab_ref, fn in ((cos_ref, jnp.cos), (sin_ref, jnp.sin)):
            t4 = fn(ang4)
            rolls = [t4] + [pltpu.roll(t4, quarter * k, 1) for k in range(1, 4)]
            for j in range(4):
                y = rolls[(-j) % 4]
                for g in range(1, 4):
                    y = jnp.where(grp == g, rolls[(g - j) % 4], y)
                tab_ref[pl.ds(r * LANES + j, quarter, stride=4), :] = y


def _rope_tables(pos_f32):
    T = pos_f32.shape[0] * LANES
    rows = min(pos_f32.shape[0], 16)
    inv = ROPE_THETA ** (-np.arange(0, HEAD_DIM, 2, dtype=np.float32) / HEAD_DIM)
    inv128 = jnp.asarray(np.tile(inv, LANES // (HEAD_DIM // 2))[None, :], jnp.float32)
    return pl.pallas_call(
        _rope_kernel,
        out_shape=(jax.ShapeDtypeStruct((T, LANES), jnp.float32),) * 2,
        grid=(pos_f32.shape[0] // rows,),
        in_specs=[pl.BlockSpec((rows, LANES), lambda i: (i, 0)),
                  pl.BlockSpec((1, LANES), lambda i: (0, 0))],
        out_specs=(pl.BlockSpec((rows * LANES, LANES), lambda i: (i, 0)),) * 2,
        scratch_shapes=[pltpu.VMEM((LANES, LANES), jnp.float32)],
        compiler_params=_params("parallel"),
        name="rope_tables",
    )(pos_f32, inv128)


def _inproj_kernel(apply_ln, x_ref, g_ref, b_ref, cos_ref, sin_ref, win_ref, wq_ref, wqp_ref,
                   wi_ref, wip_ref, *out_refs):
    if apply_ln:
        xn_ref, qt_ref, iqt_ref, iwt_ref, kr_ref, ikr_ref, vat_ref, zb_ref, zc_ref = out_refs
        x = _ln(x_ref[...], g_ref[...], b_ref[...])
        xn_ref[...] = x
    else:
        qt_ref, iqt_ref, iwt_ref, kr_ref, ikr_ref, vat_ref, zb_ref, zc_ref = out_refs
        x = x_ref[...]
    z = _dot(_bf(x), win_ref[...])
    cos = cos_ref[...]
    sin = sin_ref[...]
    cq = _bf(z[:, COL_CQ:COL_CQ + Q_RANK])
    cos3 = jnp.concatenate([cos] * 3, axis=1)
    sin3 = jnp.concatenate([sin] * 3, axis=1)
    qt_ref[...] = _bf(((_dot(cq, wq_ref[...]) * cos3 + _dot(cq, wqp_ref[...]) * sin3) * LOG2E).T)
    cos4 = jnp.concatenate([cos] * 4, axis=1)
    sin4 = jnp.concatenate([sin] * 4, axis=1)
    iqt_ref[...] = _bf((_dot(cq, wi_ref[...]) * cos4 + _dot(cq, wip_ref[...]) * sin4).T)
    kik = z[:, COL_KIK:COL_KIK + LANES] * cos + z[:, COL_KIK_P:COL_KIK_P + LANES] * sin
    kr_ref[...] = _bf(kik[:, :HEAD_DIM])
    ikr_ref[...] = _bf(kik[:, HEAD_DIM:])
    viw_t = z[:, COL_VIW:COL_VIW + LANES].T
    vat_ref[...] = _bf(viw_t[:HEAD_DIM])
    iwt_ref[...] = viw_t[HEAD_DIM:HEAD_DIM + IDX_HEADS] * (IDX_HEADS ** -0.5)
    zb_ref[...] = z[:, COL_B:COL_B + B_COLS]
    zc_ref[...] = z[:, COL_C:COL_C + POOL_WIDTH]


def _inproj(x, g, b, cos, sin, win, wq, wqp, wi, wip, apply_ln):
    T = x.shape[0]
    tm = min(T, TM_PROJ)
    row = lambda i: (i, 0)
    fix = lambda i: (0, 0)
    bf16, f32 = jnp.bfloat16, jnp.float32
    col = lambda i: (0, i)
    tok = lambda n, dt: (jax.ShapeDtypeStruct((T, n), dt), pl.BlockSpec((tm, n), row))
    feat = lambda n, dt: (jax.ShapeDtypeStruct((n, T), dt), pl.BlockSpec((n, tm), col))
    outs = [feat(ATT_WIDTH, bf16), feat(IDX_HEADS * IDX_DIM, bf16), feat(IDX_HEADS, f32), tok(HEAD_DIM, bf16),
            tok(HEAD_DIM, bf16), feat(HEAD_DIM, bf16), tok(B_COLS, f32), tok(POOL_WIDTH, f32)]
    shapes = [o[0] for o in outs]
    specs = [o[1] for o in outs]
    if apply_ln:
        shapes = [jax.ShapeDtypeStruct((T, D_MODEL), f32)] + shapes
        specs = [pl.BlockSpec((tm, D_MODEL), row)] + specs
    return pl.pallas_call(
        functools.partial(_inproj_kernel, apply_ln),
        out_shape=tuple(shapes),
        grid=(T // tm,),
        in_specs=[pl.BlockSpec((tm, D_MODEL), row),
                  pl.BlockSpec((1, D_MODEL), fix), pl.BlockSpec((1, D_MODEL), fix),
                  pl.BlockSpec((tm, LANES), row), pl.BlockSpec((tm, LANES), row),
                  pl.BlockSpec((D_MODEL, N_IN_P), fix),
                  pl.BlockSpec((Q_RANK, ATT_WIDTH), fix), pl.BlockSpec((Q_RANK, ATT_WIDTH), fix),
                  pl.BlockSpec((Q_RANK, IDX_HEADS * IDX_DIM), fix),
                  pl.BlockSpec((Q_RANK, IDX_HEADS * IDX_DIM), fix)],
        out_specs=tuple(specs),
        compiler_params=_params("parallel"),
        name="inproj",
    )(x, g, b, cos, sin, win, wq, wqp, wi, wip)


def _attn_block(kend, search, topk, qt_ref, iqt_ref, iwt_ref, kr_ref, ikr_ref, vat_ref, g_ref, o_ref,
                key_ref, bias_ref, ot_ref, hi_ref, lo_ref):
    tq = qt_ref.shape[1]
    q_idx = pl.program_id(1) * tq + lax.broadcasted_iota(jnp.int32, (1, tq), 1)
    k_idx = lax.broadcasted_iota(jnp.int32, (kend, 1), 0)
    causal = k_idx <= q_idx

    if not search:
        bias_ref[:kend, :] = jnp.where(causal, 0.0, -jnp.inf)
    else:
        def idx_pair(j, first):
            ik = ikr_ref[0:kend, :]
            heads = [2 * j, 2 * j + 1]
            rows = [h * IDX_DIM if first else pl.multiple_of(h * IDX_DIM, IDX_DIM) for h in heads]
            lgs = [_dot(ik, iqt_ref[pl.ds(r, IDX_DIM), :]) for r in rows]
            acc = None if first else bias_ref[:kend, :]
            for h, lg in zip(heads, lgs):
                term = jnp.maximum(lg, 0.0) * iwt_ref[pl.ds(h, 1), :]
                acc = term if acc is None else acc + term
            bias_ref[:kend, :] = acc

        idx_pair(0, True)
        lax.fori_loop(1, IDX_HEADS // 2, lambda j, c: (idx_pair(j, False), c)[1], 0)
        score = bias_ref[:kend, :] + 0.0
        bits = pltpu.bitcast(score, jnp.int32)
        key = bits ^ ((bits >> 31) & 0x7FFFFFFF)
        key_ref[:kend, :] = jnp.where(causal, key, INT_MIN)

        i16 = jnp.int16
        pack = 16
        one_b = jnp.ones((pack, tq), jnp.bfloat16)
        zero_b = jnp.zeros((pack, tq), jnp.bfloat16)

        def count16(ref, cand):
            c16 = jnp.broadcast_to(cand, (pack, tq)).astype(i16)
            accs = [zero_b] * 4
            for n, i in enumerate(range(0, kend, pack)):
                accs[n % 4] = accs[n % 4] + jnp.where(ref[i:i + pack, :] >= c16, one_b, zero_b)
            acc = (accs[0] + accs[1]) + (accs[2] + accs[3])
            return jnp.sum(acc.astype(jnp.float32), axis=0, keepdims=True)

        def kth_largest16(ref, k):
            t = jnp.where(count16(ref, jnp.zeros((1, tq), jnp.int32)) >= k, 0, -32768).astype(jnp.int32)

            def bit_step(i, t):
                cand = t | (jnp.int32(1) << (14 - i))
                return jnp.where(count16(ref, cand) >= k, cand, t)

            return lax.fori_loop(0, 15, bit_step, t)

        keyv = key_ref[:kend, :]
        hi_ref[:kend, :] = (keyv >> 16).astype(i16)
        tau_hi = kth_largest16(hi_ref, jnp.full((1, tq), float(topk), jnp.float32))
        above = jnp.where(tau_hi < 32767, count16(hi_ref, jnp.minimum(tau_hi + 1, 32767)), 0.0)
        lo = ((keyv & 0xFFFF) - 32768).astype(i16)
        same_hi = hi_ref[:kend, :] == jnp.broadcast_to(tau_hi, (kend, tq)).astype(i16)
        lo_ref[:kend, :] = jnp.where(same_hi, lo, -32768)
        tau_lo = kth_largest16(lo_ref, topk - above)
        tau = (tau_hi << 16) | (tau_lo + 32768)
        tau = jnp.maximum(tau, INT_MIN + 1)
        ge = key_ref[:kend, :] >= tau
        n_ge = jnp.sum(jnp.where(ge, 1.0, 0.0), axis=0, keepdims=True)
        bias_ref[:kend, :] = jnp.where(ge, 0.0, -jnp.inf)

        @pl.when(jnp.max(n_ge) > topk)
        def _():
            need = topk - jnp.sum(jnp.where(key_ref[:kend, :] > tau, 1.0, 0.0), axis=0, keepdims=True)
            ri = lax.broadcasted_iota(jnp.int32, (LANES, LANES), 0)
            ci = lax.broadcasted_iota(jnp.int32, (LANES, LANES), 1)
            lower = _bf(jnp.where(ri >= ci, 1.0, 0.0))
            before = jnp.zeros((1, tq), jnp.float32)
            for c in range(0, kend, LANES):
                keyc = key_ref[c:c + LANES, :]
                eq = keyc == tau
                rank = _dot(lower, _bf(jnp.where(eq, 1.0, 0.0))) + before
                keep = (keyc > tau) | (eq & (rank <= need))
                bias_ref[c:c + LANES, :] = jnp.where(keep, 0.0, -jnp.inf)
                before = rank[LANES - 1:LANES, :]

    kc = min(kend, ATT_KEY_CHUNK)
    chunks = range(0, kend, kc)

    def att_pair(j, carry):
        heads = [ATT_GROUP * j + i for i in range(ATT_GROUP)]
        qs = [qt_ref[pl.ds(pl.multiple_of(h * HEAD_DIM, HEAD_DIM), HEAD_DIM), :] for h in heads]
        s_h = [[_dot(kr_ref[c:c + kc, :], q_) + bias_ref[c:c + kc, :] for c in chunks] for q_ in qs]
        for h, s_c in zip(heads, s_h):
            m = jnp.max(functools.reduce(jnp.maximum, s_c), axis=0, keepdims=True)
            p_c = [jnp.exp2(s - m) for s in s_c]
            l = jnp.sum(sum(p_c), axis=0, keepdims=True)
            acc = sum(_dot(vat_ref[:, c:c + kc], _bf(p)) for p, c in zip(p_c, chunks))
            ot_ref[pl.ds(pl.multiple_of(h * HEAD_DIM, HEAD_DIM), HEAD_DIM), :] = acc / l
        return carry

    lax.fori_loop(0, ATT_HEADS // ATT_GROUP, att_pair, 0)
    ot = ot_ref[...]
    ot = ot * lax.rsqrt(jnp.mean(ot * ot, axis=0, keepdims=True) + RMS_EPS) * g_ref[...]
    o_ref[...] = _bf(ot.T)


def _key_extents(S, tq, topk):
    out = []
    for v in range(S // tq):
        need = (v + 1) * tq
        if need <= topk:
            out.append((need, False))
        else:
            out.append((min(S, -(-need // (2 * tq)) * 2 * tq), True))
    return out


def _attn_kernel(S, *refs):
    tq = refs[0].shape[1]
    topk = min(TOPK_MAX, S // 4)
    qi = pl.program_id(1)
    extents = _key_extents(S, tq, topk)
    for ext in sorted(set(extents)):
        blocks = [v for v, e in enumerate(extents) if e == ext]
        pl.when((qi >= blocks[0]) & (qi <= blocks[-1]))(functools.partial(_attn_block, *ext, topk, *refs))


def _attention(B, S, qt, iqt, iwt, kr, ikr, vat, g):
    tq = min(S, TQ)
    nq = S // tq
    qcol = lambda b, i: (0, b * nq + i)
    seq = lambda b, i: (b, 0)
    return pl.pallas_call(
        functools.partial(_attn_kernel, S),
        out_shape=jax.ShapeDtypeStruct((B * S, ATT_WIDTH), jnp.bfloat16),
        grid=(B, nq),
        in_specs=[pl.BlockSpec((ATT_WIDTH, tq), qcol),
                  pl.BlockSpec((IDX_HEADS * IDX_DIM, tq), qcol),
                  pl.BlockSpec((IDX_HEADS, tq), qcol),
                  pl.BlockSpec((S, HEAD_DIM), seq), pl.BlockSpec((S, HEAD_DIM), seq),
                  pl.BlockSpec((HEAD_DIM, S), lambda b, i: (0, b)),
                  pl.BlockSpec((ATT_WIDTH, 1), lambda b, i: (0, 0))],
        out_specs=pl.BlockSpec((tq, ATT_WIDTH), lambda b, i: (b * nq + i, 0)),
        scratch_shapes=[pltpu.VMEM((S, tq), jnp.int32), pltpu.VMEM((S, tq), jnp.float32),
                        pltpu.VMEM((ATT_WIDTH, tq), jnp.float32),
                        pltpu.VMEM((S, tq), jnp.int16), pltpu.VMEM((S, tq), jnp.int16)],
        compiler_params=_params("parallel", "parallel"),
        name="dsa_attention",
    )(qt, iqt, iwt, kr, ikr, vat, g)


def _split3(x):
    hi = _bf(x)
    r1 = x - hi.astype(jnp.float32)
    mid = _bf(r1)
    lo = _bf(r1 - mid.astype(jnp.float32))
    return hi, mid, lo


def _rwkv_kernel(zb_ref, mu_ref, w0_ref, w2_ref, a0_ref, a2_ref, g2_ref, kk_ref, ka_ref, rk_ref,
                 lg_ref, lb_ref, o_ref, prev_ref, state_ref):
    G, C, _ = zb_ref.shape
    W = RWKV_WIDTH
    R = G * C
    H = RWKV_HEADS

    @pl.when(pl.program_id(1) == 0)
    def _():
        prev_ref[...] = jnp.zeros_like(prev_ref)
        state_ref[...] = jnp.zeros_like(state_ref)

    x = zb_ref[...].reshape(R, B_COLS)
    rowi = lax.broadcasted_iota(jnp.int32, (R, 1), 0)
    shifted = pltpu.roll(x, 1, 0)
    for gi in range(G):
        shifted = jnp.where(rowi == gi * C, prev_ref[8 * gi:8 * gi + 1, :], shifted)
        prev_ref[8 * gi:8 * gi + 1, :] = x[gi * C + C - 1:gi * C + C, :]
    xm = x + (shifted - x) * mu_ref[...]
    r = xm[:, 0:W]
    k = xm[:, W:2 * W]
    v = xm[:, 2 * W:3 * W]
    wl = xm[:, 3 * W:3 * W + DECAY_RANK]
    al = xm[:, 3 * W + DECAY_RANK:3 * W + DECAY_RANK + ICLR_RANK]
    gl = xm[:, 3 * W + DECAY_RANK + ICLR_RANK:]

    wpre = w0_ref[...] + _dot(_bf(jnp.tanh(wl)), w2_ref[...])
    nw = -wpre
    w = -(jnp.maximum(nw, 0.0) + jnp.log(1.0 + jnp.exp(-jnp.abs(nw)))) - 0.5
    logd = -jnp.exp(w)
    a = jax.nn.sigmoid(a0_ref[...] + _dot(_bf(al), a2_ref[...]))
    g = _dot(_bf(jax.nn.sigmoid(gl)), g2_ref[...])

    rr = lax.broadcasted_iota(jnp.int32, (R, R), 0)
    cc = lax.broadcasted_iota(jnp.int32, (R, R), 1)
    tri = _bf(jnp.where((rr >= cc) & (rr // C == cc // C), 1.0, 0.0))
    hi, mid, lo = _split3(logd)
    cw = _dot(tri, hi) + _dot(tri, mid) + _dot(tri, lo)
    e_in = jnp.exp(cw)
    e_ex = jnp.exp(cw - logd)
    e_inv = jnp.exp(-cw)

    kk = k * kk_ref[...]
    k2 = k * (1.0 + (a - 1.0) * ka_ref[...])
    rk2 = r * k2 * rk_ref[...]
    ri = lax.broadcasted_iota(jnp.int32, (C, C), 0)
    ci = lax.broadcasted_iota(jnp.int32, (C, C), 1)
    strict = ri > ci
    ri2 = lax.broadcasted_iota(jnp.int32, (C, 2 * C), 0)
    ci2 = lax.broadcasted_iota(jnp.int32, (C, 2 * C), 1)
    incl2 = ri2 >= jnp.where(ci2 >= C, ci2 - C, ci2)
    eye = jnp.where(ri == ci, 1.0, 0.0)

    units = [(gi, h) for gi in range(G) for h in range(H)]

    def cut(arr, u):
        gi, h = u
        return arr[gi * C:(gi + 1) * C, h * HEAD_DIM:(h + 1) * HEAD_DIM]

    hd = (lax.broadcasted_iota(jnp.int32, (LANES, LANES), 0) // HEAD_DIM
          == lax.broadcasted_iota(jnp.int32, (LANES, LANES), 1) // HEAD_DIM)
    head_ones = _bf(jnp.where(hd, 1.0, 0.0))

    def ones_dot(zb):
        return jnp.concatenate([_dot(zb[:, t:t + LANES], head_ones) for t in range(0, W, LANES)], axis=1)

    def head_sum(z):
        return ones_dot(_bf(z))

    kkn_w = kk * lax.rsqrt(jnp.maximum(head_sum(kk * kk), 1e-24))
    at_w = _bf(-kkn_w * e_ex)
    rt_w = _bf(r * e_in)
    bt_w = _bf(kkn_w * a * e_inv)
    kt_w = _bf(k2 * e_inv)
    v_w = _bf(v)
    at = [cut(at_w, u) for u in units]
    rt = [cut(rt_w, u) for u in units]
    v16 = [cut(v_w, u) for u in units]
    bk16 = [jnp.concatenate([cut(bt_w, u), cut(kt_w, u)], axis=0) for u in units]
    g_a = [_dot_nt(a_, m_) for a_, m_ in zip(at, bk16)]
    l_ab = [jnp.where(strict, g_[:, :C], 0.0) for g_ in g_a]
    a_ak = [_bf(jnp.where(strict, g_[:, C:], 0.0)) for g_ in g_a]
    tinv = [eye + l_ for l_ in l_ab]
    pw = l_ab
    n = 2
    while n < C:
        pw = [_dot(_bf(p_), _bf(p_)) for p_ in pw]
        tinv = [t_ + _dot(_bf(t_), _bf(p_)) for t_, p_ in zip(tinv, pw)]
        n *= 2
    s0 = [state_ref[i] for i in range(len(units))]
    s016 = [_bf(s_) for s_ in s0]
    rhs = [_dot_nt(a_, s_) + _dot(m_, v_) for a_, s_, m_, v_ in zip(at, s016, a_ak, v16)]
    uu = [_dot(_bf(t_), _bf(x_)) for t_, x_ in zip(tinv, rhs)]
    uv16 = [jnp.concatenate([_bf(u_), v_], axis=0) for u_, v_ in zip(uu, v16)]
    g_r = [_bf(jnp.where(incl2, _dot_nt(r_, m_), 0.0)) for r_, m_ in zip(rt, bk16)]
    y = [_dot(m_, x_) + _dot_nt(r_, s_) for m_, x_, r_, s_ in zip(g_r, uv16, rt, s016)]
    for i, u in enumerate(units):
        state_ref[i] = (s0[i] + _dot_tn(uv16[i], bk16[i])) * cut(e_in, u)[C - 1:C, :]
    yw = jnp.concatenate([jnp.concatenate(y[gi * H:(gi + 1) * H], axis=1) for gi in range(G)], axis=0)
    inv_d = 1.0 / HEAD_DIM
    y_hi = _bf(yw)
    yc = yw - (ones_dot(y_hi) + head_sum(yw - y_hi.astype(jnp.float32))) * inv_d
    yn = yc * lax.rsqrt(head_sum(yc * yc) * inv_d + GN_EPS) * lg_ref[...] + lb_ref[...]
    out = _bf((yn + head_sum(rk2) * v) * g)
    for gi in range(G):
        o_ref[gi] = out[gi * C:(gi + 1) * C, :]


def _rwkv(B, S, zb, mu, w0, w2, a0, a2, g2, k_k, k_a, r_k, lnx_g, lnx_b):
    C = min(S, CHUNK)
    G = min(B, RWKV_ROWS)
    blk = lambda b, c: (b, c, 0)
    fix = lambda b, c: (0, 0)
    vec = lambda n: pl.BlockSpec((1, n), fix)
    return pl.pallas_call(
        _rwkv_kernel,
        out_shape=jax.ShapeDtypeStruct((B, S, RWKV_WIDTH), jnp.bfloat16),
        grid=(B // G, S // C),
        in_specs=[pl.BlockSpec((G, C, B_COLS), blk), vec(B_COLS), vec(RWKV_WIDTH),
                  pl.BlockSpec((DECAY_RANK, RWKV_WIDTH), fix), vec(RWKV_WIDTH),
                  pl.BlockSpec((ICLR_RANK, RWKV_WIDTH), fix),
                  pl.BlockSpec((GATE_RANK, RWKV_WIDTH), fix),
                  vec(RWKV_WIDTH), vec(RWKV_WIDTH), vec(RWKV_WIDTH), vec(RWKV_WIDTH),
                  vec(RWKV_WIDTH)],
        out_specs=pl.BlockSpec((G, C, RWKV_WIDTH), blk),
        scratch_shapes=[pltpu.VMEM((8 * G, B_COLS), jnp.float32),
                        pltpu.VMEM((G * RWKV_HEADS, HEAD_DIM, HEAD_DIM), jnp.float32)],
        compiler_params=_params("parallel", "arbitrary"),
        name="rwkv7",
    )(zb.reshape(B, S, B_COLS), mu, w0, w2, a0, a2, g2, k_k, k_a, r_k, lnx_g, lnx_b
      ).reshape(B * S, RWKV_WIDTH)


def _pool_kernel(zc_ref, w_ref, sc_ref, o_ref):
    S = zc_ref.shape[0]
    x = zc_ref[...]
    row = lax.broadcasted_iota(jnp.int32, (S, 1), 0)
    lane_grp = lax.broadcasted_iota(jnp.int32, (1, POOL_WIDTH), 1) // POOL_GROUP

    def lag(y, n):
        return jnp.where(row >= n, pltpu.roll(y, n, 0), 0.0)

    w2 = x + lag(x, 1)
    w4 = w2 + lag(w2, 2)
    w8 = w4 + lag(w4, 4)
    w16 = w8 + lag(w8, 8)
    wsum = jnp.where(lane_grp == 0, w2, jnp.where(lane_grp == 1, w4, jnp.where(lane_grp == 2, w8, w16)))
    win = jnp.where(lane_grp == 0, 2, jnp.where(lane_grp == 1, 4, jnp.where(lane_grp == 2, 8, 16)))
    count = jnp.minimum(row + 1, win).astype(jnp.float32)
    pooled = wsum / count - x
    o_ref[...] = _bf(_dot(_bf(pooled), w_ref[...]) * sc_ref[...])


def _pool(B, S, zc, w_bd, scale):
    return pl.pallas_call(
        _pool_kernel,
        out_shape=jax.ShapeDtypeStruct((B * S, POOL_WIDTH), jnp.bfloat16),
        grid=(B,),
        in_specs=[pl.BlockSpec((S, POOL_WIDTH), lambda b: (b, 0)),
                  pl.BlockSpec((POOL_WIDTH, POOL_WIDTH), lambda b: (0, 0)),
                  pl.BlockSpec((1, POOL_WIDTH), lambda b: (0, 0))],
        out_specs=pl.BlockSpec((S, POOL_WIDTH), lambda b: (b, 0)),
        compiler_params=_params("parallel"),
        name="pool",
    )(zc, w_bd, scale)


def _outmlp_kernel(alpha, oa_ref, ob_ref, oc_ref, x_ref, p_ref, wa_ref, wb_ref, wc_ref, g1_ref, b1_ref,
                   w1_ref, w2_ref, pg_ref, pgb_ref, pp_ref, g2_ref, b2_ref, o_ref):
    tm = x_ref.shape[0]
    halves = [slice(0, tm // 2), slice(tm // 2, tm)]
    mix = [_dot(oa_ref[h, :], wa_ref[...]) + _dot(ob_ref[h, :], wb_ref[...]) + _dot(oc_ref[h, :], wc_ref[...])
           for h in halves]
    x1 = [_ln(alpha * x_ref[h, :] + m, g1_ref[...], b1_ref[...]) for h, m in zip(halves, mix)]
    xb = [_bf(v) for v in x1]
    gate = [jax.nn.sigmoid(_dot(b_, pg_ref[...]) + pgb_ref[...]) for b_ in xb]
    pe = [_dot(_bf(p_ref[0, h, :]), pp_ref[...]) for h in halves]
    acc = [alpha * v + g_ * e_ for v, g_, e_ in zip(x1, gate, pe)]
    xb_all = jnp.concatenate(xb, axis=0)
    nk = D_FF // TF_MLP
    for kf in range(nk):
        cols = slice(kf * TF_MLP, (kf + 1) * TF_MLP)
        hk = jnp.maximum(_dot(xb_all, w1_ref[:, cols]), 0.0)
        h2 = _bf(hk * hk)
        if kf < nk - 1:
            part = _dot(h2, w2_ref[cols, :])
            acc = [a + part[h, :] for a, h in zip(acc, halves)]
        else:
            acc = [a + _dot(h2[h, :], w2_ref[cols, :]) for a, h in zip(acc, halves)]
    for h, a in zip(halves, acc):
        o_ref[h, :] = _ln(a, g2_ref[...], b2_ref[...])


def _outmlp(alpha, layer, oa, ob, oc, x, p, wa, wb, wc, g1, b1, w1, w2, pg, pgb, pp, g2, b2):
    T = x.shape[0]
    tm = min(T, TM_MLP)
    row = lambda i: (i, 0)
    fix = lambda i: (0, 0)
    once = pl.Buffered(1)
    wspec = lambda r, c: pl.BlockSpec((r, c), fix, pipeline_mode=once)
    vec = pl.BlockSpec((1, D_MODEL), fix)
    return pl.pallas_call(
        functools.partial(_outmlp_kernel, alpha),
        out_shape=jax.ShapeDtypeStruct((T, D_MODEL), jnp.float32),
        grid=(T // tm,),
        in_specs=[pl.BlockSpec((tm, ATT_WIDTH), row), pl.BlockSpec((tm, RWKV_WIDTH), row),
                  pl.BlockSpec((tm, POOL_WIDTH), row), pl.BlockSpec((tm, D_MODEL), row),
                  pl.BlockSpec((1, tm, PE_DIM), lambda i: (layer, i, 0)),
                  wspec(ATT_WIDTH, D_MODEL), wspec(RWKV_WIDTH, D_MODEL), wspec(POOL_WIDTH, D_MODEL), vec, vec,
                  wspec(D_MODEL, D_FF), wspec(D_FF, D_MODEL), wspec(D_MODEL, D_MODEL), vec,
                  wspec(PE_DIM, D_MODEL), vec, vec],
        out_specs=pl.BlockSpec((tm, D_MODEL), row),
        compiler_params=_params("parallel"),
        name="outproj_mlp",
    )(oa, ob, oc, x, p, wa, wb, wc, g1, b1, w1, w2, pg, pgb, pp, g2, b2)


def _rot_partner(w):
    half = HEAD_DIM // 2
    return jnp.concatenate([-w[..., half:], w[..., :half]], axis=-1)


def _per_head_partner(w, heads):
    r = w.reshape(w.shape[0], heads, HEAD_DIM)
    return _rot_partner(r).reshape(w.shape)


def _arrange_w_in(w):
    cq = w[:, 0:256]
    ka = w[:, 256:320]
    va = w[:, 320:384]
    ik = w[:, 384:448]
    iw = w[:, 448:456]
    rest = w[:, 456:]
    pad = jnp.zeros((w.shape[0], LANES - HEAD_DIM - IDX_HEADS), w.dtype)
    return jnp.concatenate([cq, ka, ik, _rot_partner(ka), _rot_partner(ik), va, iw, pad, rest], axis=1)


def kernel(x, p, positions, ln_emb_g, ln_emb_b, w_in, w_uq, w_uqi, attn_norm_g, rwkv_mu, rwkv_w0, rwkv_w2, rwkv_a0, rwkv_a2, rwkv_g2, rwkv_k_k, rwkv_k_a, rwkv_r_k, rwkv_lnx_g, rwkv_lnx_b, pool_w, pool_scale, w_out, ln1_g, ln1_b, mlp_w1, mlp_w2, pe_proj, pe_gate, pe_gate_b, ln2_g, ln2_b):
    B, S, _ = x.shape
    T = B * S
    depth = w_in.shape[0]
    alpha = (2 * depth) ** 0.25
    bf = lambda a: a.astype(jnp.bfloat16)
    rowv = lambda a: a.reshape(1, -1)

    cos, sin = _rope_tables(positions.astype(jnp.float32).reshape(T // LANES, LANES))
    xs = x.reshape(T, D_MODEL)
    for i in range(depth):
        win = bf(_arrange_w_in(w_in[i]))
        wq = w_uq[i] * (HEAD_DIM ** -0.5)
        wi = w_uqi[i] * (IDX_DIM ** -0.5)
        res = _inproj(xs, rowv(ln_emb_g), rowv(ln_emb_b), cos, sin, win,
                      bf(wq), bf(_per_head_partner(wq, ATT_HEADS)),
                      bf(wi), bf(_per_head_partner(wi, IDX_HEADS)), apply_ln=(i == 0))
        if i == 0:
            xs, res = res[0], res[1:]
        qt, iqt, iwt, kr, ikr, vat, zb, zc = res
        o_a = _attention(B, S, qt, iqt, iwt, kr, ikr, vat, attn_norm_g[i].reshape(-1, 1))
        o_b = _rwkv(B, S, zb, rowv(rwkv_mu[i]), rowv(rwkv_w0[i]), bf(rwkv_w2[i]), rowv(rwkv_a0[i]),
                    bf(rwkv_a2[i]), bf(rwkv_g2[i]), rowv(rwkv_k_k[i]), rowv(rwkv_k_a[i]),
                    rowv(rwkv_r_k[i]), rowv(rwkv_lnx_g[i]), rowv(rwkv_lnx_b[i]))
        w_bd = jax.scipy.linalg.block_diag(*[pool_w[i, gi] for gi in range(len(POOL_WINDOWS))])
        o_c = _pool(B, S, zc, bf(w_bd), rowv(pool_scale[i]))
        wo = bf(w_out[i])
        xs = _outmlp(alpha, i, o_a, o_b, o_c, xs, p.reshape(depth, T, PE_DIM), wo[:ATT_WIDTH],
                     wo[ATT_WIDTH:ATT_WIDTH + RWKV_WIDTH], wo[ATT_WIDTH + RWKV_WIDTH:], rowv(ln1_g[i]),
                     rowv(ln1_b[i]), bf(mlp_w1[i]), bf(mlp_w2[i]), bf(pe_gate[i]), rowv(pe_gate_b[i]),
                     bf(pe_proj[i]), rowv(ln2_g[i]), rowv(ln2_b[i]))
    return xs.reshape(B, S, D_MODEL)
```
